```python
import math
import jax, jax.numpy as jnp
from jax import lax
import numpy as np

D_MODEL = 4096
BATCH = 4
SEQ = 2048
DEPTH = 2
DEC_BATCH = 8
DEC_SEQ = 1
PAST_LEN = 16384
PAGE_SIZE = 128

HEAD_DIM = 128
N_HEADS = (D_MODEL // 2) // HEAD_DIM
N_KV = max(1, N_HEADS // 4)
HPG = N_HEADS // N_KV
D_ATTN = N_HEADS * HEAD_DIM
CMP_BLOCK = 64
N_SELECT = 16
WINDOW = 512
SEL_QBLK = 64
WIN_QBLK = 128
FORCED_SCORE = 1e4
D_RNN = D_MODEL // 2
RNN_BLOCKS = 16
RNN_BW = D_RNN // RNN_BLOCKS
CONV_W = 4
RG_C = 8.0
D_FF = 4 * D_MODEL
ROPE_THETA = 10000.0
EPS = 1e-6
N_KV_COLS = 6 * N_KV * HEAD_DIM
O_Q = 0
O_KV = D_ATTN
O_GN = O_KV + N_KV_COLS
O_RX = O_GN + 3 * N_HEADS
O_RG = O_RX + D_RNN
O_MG = O_RG + D_RNN
N_IN = O_MG + 2 * D_MODEL

kernel_name = 'hybrid_nsa_rglru_decoder_step'


def _rmsnorm(x, g):
    xf = x.astype(jnp.float32)
    y = xf * lax.rsqrt(jnp.mean(xf * xf, axis=-1, keepdims=True) + EPS)
    return (y * g.astype(jnp.float32)).astype(x.dtype)


def _rope(x, pos0):
    T = x.shape[1]
    half = HEAD_DIM // 2
    inv = ROPE_THETA ** (-jnp.arange(half, dtype=jnp.float32) / half)
    ang = (pos0 + jnp.arange(T, dtype=jnp.float32))[:, None] * inv[None, :]
    cos = jnp.cos(ang)[None, :, None, :]
    sin = jnp.sin(ang)[None, :, None, :]
    xf = x.astype(jnp.float32)
    x1, x2 = xf[..., :half], xf[..., half:]
    return jnp.concatenate([x1 * cos - x2 * sin, x2 * cos + x1 * sin], axis=-1).astype(x.dtype)


def _masked_softmax(s, mask):
    s = jnp.where(mask, s.astype(jnp.float32), -jnp.inf)
    m = jnp.max(s, axis=-1, keepdims=True)
    m = jnp.where(jnp.isfinite(m), m, 0.0)
    e = jnp.exp(s - m)
    d = jnp.sum(e, axis=-1, keepdims=True)
    return e / jnp.where(d > 0, d, 1.0)


def _map_query_blocks(fn, qb, *xs):
    Tq = xs[0].shape[1]
    if Tq % qb != 0 or Tq <= qb:
        return fn(*xs)
    n = Tq // qb
    split = lambda a: jnp.moveaxis(a.reshape(a.shape[0], n, qb, *a.shape[2:]), 1, 0)
    out = lax.map(lambda args: fn(*args), tuple(split(a) for a in xs))
    out = jnp.moveaxis(out, 0, 1)
    return out.reshape(out.shape[0], Tq, *out.shape[3:])


def _nsa(q, k_cmp, v_cmp, k_sel, v_sel, k_win, v_win, gates, w_phi, pe_cmp, pos0):
    B, Tq = q.shape[:2]
    Lk = k_cmp.shape[1]
    nblk = -(-Lk // CMP_BLOCK)
    pad = nblk * CMP_BLOCK - Lk
    scale = HEAD_DIM ** -0.5
    t = pos0 + jnp.arange(Tq)
    qg = q.reshape(B, Tq, N_KV, HPG, HEAD_DIM)
    blk = jnp.arange(nblk)

    def to_blocks(rows):
        return jnp.pad(rows, ((0, 0), (0, pad), (0, 0), (0, 0))).reshape(B, nblk, CMP_BLOCK, N_KV, HEAD_DIM)

    kc = jnp.einsum('bnlgd,lde->bnge', to_blocks(k_cmp) + pe_cmp[0][:, None, :], w_phi[0])
    vc = jnp.einsum('bnlgd,lde->bnge', to_blocks(v_cmp) + pe_cmp[1][:, None, :], w_phi[1])
    s_c = jnp.einsum('btghd,bngd->btghn', qg, kc) * scale
    complete = (blk[None, :] + 1) * CMP_BLOCK - 1 <= t[:, None]
    p_c = _masked_softmax(s_c, complete[None, :, None, None, :])
    o_cmp = jnp.einsum('btghn,bngd->btghd', p_c.astype(vc.dtype), vc)

    cur = (t // CMP_BLOCK)[:, None]
    imp = jnp.sum(p_c, axis=3)
    forced = (blk[None, :] == 0) | (blk[None, :] == cur) | (blk[None, :] == cur - 1)
    score = jnp.where(forced[None, :, None, :], FORCED_SCORE, imp)
    score = jnp.where((blk[None, :] <= cur)[None, :, None, :], score, -1.0)
    n_sel = min(N_SELECT, nblk)
    top_v, top_i = lax.top_k(score, n_sel)
    valid = top_v > -0.5
    ks_b = jnp.moveaxis(to_blocks(k_sel), 3, 1)
    vs_b = jnp.moveaxis(to_blocks(v_sel), 3, 1)
    gather = jax.vmap(jax.vmap(lambda kb, ix: kb[ix]))

    def sel_attend(qc, tc, ic, okc):
        QB = qc.shape[1]
        ic_t = jnp.swapaxes(ic, 1, 2)
        kg = gather(ks_b, ic_t).reshape(B, N_KV, QB, n_sel * CMP_BLOCK, HEAD_DIM)
        vg = gather(vs_b, ic_t).reshape(B, N_KV, QB, n_sel * CMP_BLOCK, HEAD_DIM)
        s = jnp.einsum('bqghd,bgqmd->bqghm', qc, kg) * scale
        kpos = ic[..., None] * CMP_BLOCK + jnp.arange(CMP_BLOCK)
        mask = (okc[..., None] & (kpos <= tc[0][None, :, None, None, None]))
        mask = mask.reshape(B, QB, N_KV, 1, n_sel * CMP_BLOCK)
        p = _masked_softmax(s, mask)
        return jnp.einsum('bqghm,bgqmd->bqghd', p.astype(vg.dtype), vg)

    o_sel = _map_query_blocks(sel_attend, SEL_QBLK, qg, t[None, :], top_i, valid)

    qb = WIN_QBLK if Tq % WIN_QBLK == 0 else Tq
    nq = Tq // qb
    span = WINDOW + qb
    idx = jnp.arange(nq)[:, None] * qb + jnp.arange(span)[None, :]
    kpos_w = pos0 - WINDOW + idx
    tq = t.reshape(nq, qb)
    kw = k_win[:, idx]
    vw = v_win[:, idx]
    s_w = jnp.einsum('bnqghd,bnkgd->bnqghk', qg.reshape(B, nq, qb, N_KV, HPG, HEAD_DIM), kw) * scale
    dist = tq[:, :, None] - kpos_w[:, None, :]
    mask_w = (dist >= 0) & (dist < WINDOW) & (kpos_w[:, None, :] >= 0)
    p_w = _masked_softmax(s_w, mask_w[None, :, :, None, None, :])
    o_win = jnp.einsum('bnqghk,bnkgd->bnqghd', p_w.astype(vw.dtype), vw).reshape(B, Tq, N_KV, HPG, HEAD_DIM)

    g = jax.nn.sigmoid(gates.astype(jnp.float32)).reshape(B, Tq, N_KV, HPG, 3).astype(q.dtype)
    o = g[..., 0:1] * o_cmp + g[..., 1:2] * o_sel + g[..., 2:3] * o_win
    return o.reshape(B, Tq, D_ATTN)


def _rg_lru(xr, zg, conv0, h0, conv_w, conv_b, w_rg, b_rg, lam):
    B, T, _ = xr.shape
    xp = jnp.concatenate([conv0.astype(xr.dtype), xr], axis=1)
    u = conv_b
    for k in range(CONV_W):
        u = u + xp[:, k:k + T] * conv_w[k]
    new_conv = xp[:, -(CONV_W - 1):]
    ub = u.reshape(B, T, RNN_BLOCKS, RNN_BW)
    gts = jnp.einsum('btnd,knde->kbtne', ub, w_rg).reshape(2, B, T, D_RNN)
    r = jax.nn.sigmoid((gts[0] + b_rg[0]).astype(jnp.float32))
    i = jax.nn.sigmoid((gts[1] + b_rg[1]).astype(jnp.float32))
    log_a = -RG_C * r * jax.nn.softplus(-lam.astype(jnp.float32))
    a = jnp.exp(log_a)
    b = jnp.sqrt(-jnp.expm1(2.0 * log_a)) * i * u.astype(jnp.float32)
    b = b.at[:, 0].add(a[:, 0] * h0.astype(jnp.float32))
    comb = lambda l, rr: (l[0] * rr[0], rr[0] * l[1] + rr[1])
    _, h = lax.associative_scan(comb, (a, b), axis=1)
    y = (jax.nn.gelu(zg.astype(jnp.float32)) * h).astype(xr.dtype)
    return y, h[:, -1].astype(h0.dtype), new_conv


def _layer(x, c, pos0, past_kv, win_prefix, h0, conv0, n_keep,
           w_ada, b_ada, g_ln, w_in, qk_g, w_phi, pe_cmp, conv_w, conv_b, w_rg, b_rg, lam,
           w_branch_nsa, w_branch_rnn, w_out, w_mlp1, w_mlp2):
    B, T, _ = x.shape
    mod = (jax.nn.silu(c) @ w_ada + b_ada).reshape(B, 6, D_MODEL)[:, :, None, :]
    sh1, sc1, ga1, sh2, sc2, ga2 = mod[:, 0], mod[:, 1], mod[:, 2], mod[:, 3], mod[:, 4], mod[:, 5]
    h = _rmsnorm(x, g_ln[0]) * (1 + sc1) + sh1
    z = h @ w_in
    q = _rope(_rmsnorm(z[..., O_Q:O_KV].reshape(B, T, N_HEADS, HEAD_DIM), qk_g[0]), pos0)
    kv = z[..., O_KV:O_GN].reshape(B, T, 3, 2, N_KV, HEAD_DIM)
    keys = [_rope(_rmsnorm(kv[:, :, j, 0], qk_g[1 + j]), pos0) for j in range(3)]
    vals = [kv[:, :, j, 1] for j in range(3)]
    new_rows = jnp.stack([keys[0], vals[0], keys[1], vals[1]], axis=2)
    win_rows = jnp.stack([keys[2], vals[2]], axis=2)
    full = new_rows if past_kv is None else jnp.concatenate([past_kv.astype(new_rows.dtype), new_rows], axis=1)
    win_full = jnp.concatenate([win_prefix.astype(win_rows.dtype), win_rows], axis=1)
    o_nsa = _nsa(q, full[:, :, 0], full[:, :, 1], full[:, :, 2], full[:, :, 3],
                 win_full[:, :, 0], win_full[:, :, 1], z[..., O_GN:O_RX], w_phi, pe_cmp, pos0)
    o_rnn, h_last, conv_last = _rg_lru(z[..., O_RX:O_RG], z[..., O_RG:O_MG], conv0, h0,
                                       conv_w, conv_b, w_rg, b_rg, lam)
    gm = jax.nn.sigmoid(z[..., O_MG:].astype(jnp.float32)).astype(x.dtype)
    merged = gm[..., :D_MODEL] * (o_nsa @ w_branch_nsa) + gm[..., D_MODEL:] * (o_rnn @ w_branch_rnn)
    x = x + ga1 * (merged @ w_out)
    h2 = _rmsnorm(x, g_ln[1]) * (1 + sc2) + sh2
    x = x + ga2 * (jnp.square(jax.nn.relu(h2 @ w_mlp1)) @ w_mlp2)
    return x, new_rows, win_full[:, -n_keep:], h_last, conv_last


def setup_inputs(seed: int = 0) -> dict:
    key = jax.random.key(seed)
    ks = jax.random.split(key, 32)
    n_pages = PAST_LEN // PAGE_SIZE
    n_used = DEC_BATCH * n_pages
    n_phys = n_used + max(1, n_used // 4)
    wbuf = min(WINDOW, PAST_LEN)
    nrm = lambda k, shape, s: jax.random.normal(k, shape, jnp.float32) * s
    u = jax.random.uniform(ks[20], (DEPTH, D_RNN), jnp.float32, minval=0.9, maxval=0.999)
    page_table = jax.random.permutation(ks[7], n_phys)[:n_used].reshape(DEC_BATCH, n_pages).astype(jnp.int32)
    return {
        'x_prompt': nrm(ks[0], (BATCH, SEQ, D_MODEL), 1.0),
        'x_sample': nrm(ks[1], (DEC_BATCH, DEC_SEQ, D_MODEL), 1.0),
        'cache_kv': nrm(ks[2], (DEPTH, n_phys, PAGE_SIZE, 4, N_KV, HEAD_DIM), 1.0),
        'state_win': nrm(ks[3], (DEPTH, DEC_BATCH, wbuf, 2, N_KV, HEAD_DIM), 1.0),
        'state_rnn': nrm(ks[4], (DEPTH, DEC_BATCH, D_RNN), 0.5),
        'state_conv': nrm(ks[5], (DEPTH, DEC_BATCH, CONV_W - 1, D_RNN), 1.0),
        'page_table': page_table,
        'c_prompt': nrm(ks[8], (BATCH, D_MODEL), 1.0),
        'c_sample': nrm(ks[9], (DEC_BATCH, D_MODEL), 1.0),
        'w_ada': nrm(ks[10], (DEPTH, D_MODEL, 6 * D_MODEL), 0.5 * D_MODEL ** -0.5),
        'b_ada': nrm(ks[11], (DEPTH, 6 * D_MODEL), 0.01),
        'g_ln': 1.0 + nrm(ks[12], (DEPTH, 2, D_MODEL), 0.01),
        'w_in': nrm(ks[13], (DEPTH, D_MODEL, N_IN), D_MODEL ** -0.5),
        'qk_g': 1.0 + nrm(ks[14], (DEPTH, 4, HEAD_DIM), 0.01),
        'w_phi': nrm(ks[15], (DEPTH, 2, CMP_BLOCK, HEAD_DIM, HEAD_DIM), (CMP_BLOCK * HEAD_DIM) ** -0.5),
        'pe_cmp': nrm(ks[16], (DEPTH, 2, CMP_BLOCK, HEAD_DIM), 0.02),
        'conv_w': nrm(ks[17], (DEPTH, CONV_W, D_RNN), CONV_W ** -0.5),
        'conv_b': nrm(ks[18], (DEPTH, D_RNN), 0.01),
        'w_rg': nrm(ks[19], (DEPTH, 2, RNN_BLOCKS, RNN_BW, RNN_BW), RNN_BW ** -0.5),
        'b_rg': nrm(ks[21], (DEPTH, 2, D_RNN), 0.01),
        'lam': jnp.log(u / (1.0 - u)),
        'w_branch_nsa': nrm(ks[22], (DEPTH, D_ATTN, D_MODEL), D_ATTN ** -0.5),
        'w_branch_rnn': nrm(ks[23], (DEPTH, D_RNN, D_MODEL), D_RNN ** -0.5),
        'w_out': nrm(ks[24], (DEPTH, D_MODEL, D_MODEL), D_MODEL ** -0.5),
        'w_mlp1': nrm(ks[25], (DEPTH, D_MODEL, D_FF), D_MODEL ** -0.5),
        'w_mlp2': nrm(ks[26], (DEPTH, D_FF, D_MODEL), D_FF ** -0.5),
    }


def reference(x_prompt, x_sample, cache_kv, state_win, state_rnn, state_conv, page_table, c_prompt, c_sample,
              w_ada, b_ada, g_ln, w_in, qk_g, w_phi, pe_cmp, conv_w, conv_b, w_rg, b_rg, lam,
              w_branch_nsa, w_branch_rnn, w_out, w_mlp1, w_mlp2):
    past_len = page_table.shape[1] * cache_kv.shape[2]
    n_db = x_sample.shape[0]
    yp, ys = x_prompt, x_sample
    kv_p, kv_s, win_p, win_s, h_p, h_s, cv_p, cv_s = [], [], [], [], [], [], [], []
    for l in range(DEPTH):
        wl = (w_ada[l], b_ada[l], g_ln[l], w_in[l], qk_g[l], w_phi[l], pe_cmp[l], conv_w[l], conv_b[l],
              w_rg[l], b_rg[l], lam[l], w_branch_nsa[l], w_branch_rnn[l], w_out[l], w_mlp1[l], w_mlp2[l])
        B, T = yp.shape[0], yp.shape[1]
        yp, a1, a2, a3, a4 = _layer(
            yp, c_prompt, 0, None,
            jnp.zeros((B, WINDOW, 2, N_KV, HEAD_DIM), yp.dtype),
            jnp.zeros((B, D_RNN), yp.dtype),
            jnp.zeros((B, CONV_W - 1, D_RNN), yp.dtype),
            min(WINDOW, T), *wl)
        kv_p.append(a1); win_p.append(a2); h_p.append(a3); cv_p.append(a4)
        past = cache_kv[l][page_table].reshape(n_db, past_len, 4, N_KV, HEAD_DIM)
        wbuf = state_win[l]
        n_keep = wbuf.shape[1]
        prefix = jnp.pad(wbuf, ((0, 0), (WINDOW - n_keep, 0), (0, 0), (0, 0), (0, 0)))
        ys, b1, b2, b3, b4 = _layer(ys, c_sample, past_len, past, prefix, state_rnn[l], state_conv[l], n_keep, *wl)
        kv_s.append(b1); win_s.append(b2); h_s.append(b3); cv_s.append(b4)
    return (yp, ys, jnp.stack(kv_p), jnp.stack(kv_s), jnp.stack(win_p), jnp.stack(win_s),
            jnp.stack(h_p), jnp.stack(h_s), jnp.stack(cv_p), jnp.stack(cv_s))
```

```python
import functools
import math

import jax
import jax.numpy as jnp
from jax import lax
from jax.experimental import pallas as pl
from jax.experimental.pallas import tpu as pltpu

BF = jnp.bfloat16
F32 = jnp.float32

HEAD_DIM = 128
N_KV = 4
HPG = 4
N_HEADS = N_KV * HPG
CMP_BLOCK = 64
CMP_SHIFT = 6
N_SELECT = 16
WINDOW = 512
FORCED_SCORE = 1e4
RNN_BW = 128
CONV_W = 4
RG_C = 8.0
ROPE_THETA = 10000.0
EPS = 1e-6
NEG = -1e30
LANES = 128
VMEM_LIMIT = 56 * 1024 * 1024


def _cparams(sem):
    return pltpu.CompilerParams(dimension_semantics=sem, vmem_limit_bytes=VMEM_LIMIT)


def _dot(a, b):
    return jnp.dot(a, b, preferred_element_type=F32)


def _dot_nt(a, b):
    return lax.dot_general(a, b, (((1,), (1,)), ((), ())), preferred_element_type=F32)


def _rep_lanes(x, n):
    return x if n == LANES else jnp.concatenate([x] * (n // LANES), axis=1)


def _mm_kernel(*refs, nk, epi):
    if epi == "resid":
        a_ref, w_ref, x_ref, ga_ref, o_ref = refs[:5]
        scratch = refs[5:]
    elif epi == "silu_bias":
        a_ref, w_ref, bias_ref, o_ref = refs[:4]
        scratch = refs[4:]
    else:
        a_ref, w_ref, o_ref = refs[:3]
        scratch = refs[3:]

    def finish(acc):
        if epi == "relu2":
            r = jnp.maximum(acc, 0.0)
            acc = r * r
        elif epi == "resid":
            acc = x_ref[...] + ga_ref[...] * acc
        elif epi == "silu_bias":
            acc = acc + bias_ref[...]
        o_ref[...] = acc.astype(o_ref.dtype)

    a = a_ref[...]
    if epi == "silu_bias":
        a = (a * jax.nn.sigmoid(a)).astype(BF)
    part = _dot(a, w_ref[...].astype(BF))
    if nk == 1:
        finish(part)
    else:
        acc_ref = scratch[0]
        k = pl.program_id(2)

        @pl.when(k == 0)
        def _():
            acc_ref[...] = part

        @pl.when(k > 0)
        def _():
            acc_ref[...] += part

        @pl.when(k == nk - 1)
        def _():
            finish(acc_ref[...])


def _mm(a, w, *, tm, tn, tk=None, out_dtype=F32, epi=None, x_res=None, ga=None, ga_map=None, bias=None):
    M, K = a.shape
    N = w.shape[1]
    tk = K if tk is None else tk
    nk = K // tk
    grid = (M // tm, N // tn, nk)
    in_specs = [pl.BlockSpec((tm, tk), lambda i, j, k: (i, k)),
                pl.BlockSpec((tk, tn), lambda i, j, k: (k, j))]
    args = [a, w]
    if epi == "resid":
        in_specs.append(pl.BlockSpec((tm, tn), lambda i, j, k: (i, j)))
        in_specs.append(pl.BlockSpec((None, ga.shape[1], tn), ga_map))
        args += [x_res, ga]
    elif epi == "silu_bias":
        in_specs.append(pl.BlockSpec((1, tn), lambda i, j, k: (0, j)))
        args.append(bias)
    scratch =[pltpu.VMEM((tm, tn), F32)] if nk > 1 else []
    return pl.pallas_call(
        functools.partial(_mm_kernel, nk=nk, epi=epi),
        grid=grid, in_specs=in_specs,
        out_specs=pl.BlockSpec((tm, tn), lambda i, j, k: (i, j)),
        out_shape=jax.ShapeDtypeStruct((M, N), out_dtype),
        scratch_shapes=scratch,
        compiler_params=_cparams(("parallel", "parallel", "arbitrary")),
    )(*args)


def _merge_kernel(an_ref, wn_ref, ar_ref, wr_ref, gn_ref, gr_ref, o_ref):
    yn = _dot(an_ref[...], wn_ref[...])
    yr = _dot(ar_ref[...], wr_ref[...])
    o = jax.nn.sigmoid(gn_ref[...]) * yn + jax.nn.sigmoid(gr_ref[...]) * yr
    o_ref[...] = o.astype(o_ref.dtype)


def _merge(o_nsa, w_nsa, o_rnn, w_rnn, z_mg, *, tm, tn):
    M, K = o_nsa.shape
    D = w_nsa.shape[1]
    nj = D // tn
    return pl.pallas_call(
        _merge_kernel,
        grid=(M // tm, nj),
        in_specs=[pl.BlockSpec((tm, K), lambda i, j: (i, 0)),
                  pl.BlockSpec((K, tn), lambda i, j: (0, j)),
                  pl.BlockSpec((tm, K), lambda i, j: (i, 0)),
                  pl.BlockSpec((K, tn), lambda i, j: (0, j)),
                  pl.BlockSpec((tm, tn), lambda i, j: (i, j)),
                  pl.BlockSpec((tm, tn), lambda i, j: (i, j + nj))],
        out_specs=pl.BlockSpec((tm, tn), lambda i, j: (i, j)),
        out_shape=jax.ShapeDtypeStruct((M, D), BF),
        compiler_params=_cparams(("parallel", "arbitrary")),
    )(o_nsa, w_nsa, o_rnn, w_rnn, z_mg, z_mg)


def _ln_kernel(x_ref, g_ref, sc_ref, sh_ref, o_ref):
    x = x_ref[...]
    ms = jnp.mean(x * x, axis=-1, keepdims=True)
    y = x * lax.rsqrt(ms + EPS) * g_ref[...]
    o_ref[...] = (y * (1.0 + sc_ref[...]) + sh_ref[...]).astype(o_ref.dtype)


def _ln_mod(x, g, sc, sh, *, tt):
    B, T, D = x.shape
    R = sc.shape[1]
    rr = tt if R == T else 1
    mod_map = (lambda b, i: (b, i, 0)) if R == T else (lambda b, i: (b, 0, 0))
    return pl.pallas_call(
        _ln_kernel,
        grid=(B, T // tt),
        in_specs=[pl.BlockSpec((None, tt, D), lambda b, i: (b, i, 0)),
                  pl.BlockSpec((1, D), lambda b, i: (0, 0)),
                  pl.BlockSpec((None, rr, D), mod_map),
                  pl.BlockSpec((None, rr, D), mod_map)],
        out_specs=pl.BlockSpec((None, tt, D), lambda b, i: (b, i, 0)),
        out_shape=jax.ShapeDtypeStruct((B, T, D), BF),
        compiler_params=_cparams(("parallel", "arbitrary")),
    )(x, g, sc, sh)


def _qk_kernel(z_ref, c_ref, s_ref, g_ref, q_ref, kv_ref, win_ref, kvb_ref, *, scale):
    cosf = c_ref[...]
    sinf = s_ref[...]

    def norm_rope(col, gi):
        x = z_ref[:, col:col + HEAD_DIM]
        ms = jnp.mean(x * x, axis=-1, keepdims=True)
        y = x * lax.rsqrt(ms + EPS) * g_ref[gi:gi + 1, :]
        return y * cosf + pltpu.roll(y, HEAD_DIM // 2, 1) * sinf

    nq = N_HEADS * HEAD_DIM
    grp = N_KV * HEAD_DIM
    for h in range(N_HEADS):
        q_ref[:, h * HEAD_DIM:(h + 1) * HEAD_DIM] = (norm_rope(h * HEAD_DIM, 0) * scale).astype(BF)
    for j in range(3):
        for gi in range(N_KV):
            kcol = nq + (2 * j) * grp + gi * HEAD_DIM
            vcol = nq + (2 * j + 1) * grp + gi * HEAD_DIM
            k = norm_rope(kcol, 1 + j)
            v = z_ref[:, vcol:vcol + HEAD_DIM]
            ko = (2 * j) * grp + gi * HEAD_DIM
            vo = (2 * j + 1) * grp + gi * HEAD_DIM
            kvb_ref[:, ko:ko + HEAD_DIM] = k.astype(BF)
            kvb_ref[:, vo:vo + HEAD_DIM] = v.astype(BF)
            if j < 2:
                kv_ref[:, ko:ko + HEAD_DIM] = k
                kv_ref[:, vo:vo + HEAD_DIM] = v
            else:
                win_ref[:, gi * HEAD_DIM:(gi + 1) * HEAD_DIM] = k
                win_ref[:, grp + gi * HEAD_DIM:grp + (gi + 1) * HEAD_DIM] = v


def _qk_prep(z, cosf, sinf, qk_g, *, tt):
    B, T, NZ = z.shape
    nq = N_HEADS * HEAD_DIM
    grp = N_KV * HEAD_DIM
    row = lambda b, i: (b, i, 0)
    return pl.pallas_call(
        functools.partial(_qk_kernel, scale=HEAD_DIM ** -0.5),
        grid=(B, T // tt),
        in_specs=[pl.BlockSpec((None, tt, NZ), row),
                  pl.BlockSpec((tt, HEAD_DIM), lambda b, i: (i, 0)),
                  pl.BlockSpec((tt, HEAD_DIM), lambda b, i: (i, 0)),
                  pl.BlockSpec((4, HEAD_DIM), lambda b, i: (0, 0))],
        out_specs=[pl.BlockSpec((None, tt, nq), row),
                   pl.BlockSpec((None, tt, 4 * grp), row),
                   pl.BlockSpec((None, tt, 2 * grp), row),
                   pl.BlockSpec((None, tt, 6 * grp), row)],
        out_shape=[jax.ShapeDtypeStruct((B, T, nq), BF),
                   jax.ShapeDtypeStruct((B, T, 4 * grp), F32),
                   jax.ShapeDtypeStruct((B, T, 2 * grp), F32),
                   jax.ShapeDtypeStruct((B, T, 6 * grp), BF)],
        compiler_params=_cparams(("parallel", "arbitrary")),
    )(z, cosf, sinf, qk_g)


def _rope_tables(pos0, T):
    half = HEAD_DIM // 2
    inv = ROPE_THETA ** (-jnp.arange(half, dtype=F32) / half)
    ang = (pos0 + jnp.arange(T, dtype=F32))[:, None] * inv[None, :]
    cos, sin = jnp.cos(ang), jnp.sin(ang)
    return jnp.concatenate([cos, cos], axis=1), jnp.concatenate([-sin, sin], axis=1)


def _compress_acc(load_pair, pe_ref, w_ref, rows):
    acc = jnp.zeros((rows, 2 * HEAD_DIM), F32)
    for lp in range(CMP_BLOCK // 2):
        xs = []
        for l in (2 * lp, 2 * lp + 1):
            x = load_pair(l) + pe_ref[l][None]
            xs.append(x.reshape(rows, HEAD_DIM).astype(BF))
        acc = acc + _dot(jnp.concatenate(xs, axis=1), w_ref[lp])
    is_k = (lax.broadcasted_iota(jnp.int32, (rows, HEAD_DIM), 0) % 8) < N_KV
    return jnp.where(is_k, acc[:, :HEAD_DIM], acc[:, HEAD_DIM:])


def _compress_kernel(x_ref, pe_ref, w_ref, o_ref, *, nblk):
    rows = nblk * 8
    out = _compress_acc(lambda l: x_ref[pl.ds(l, nblk, stride=CMP_BLOCK), :, :], pe_ref, w_ref, rows)
    o_ref[...] = out.reshape(nblk, 8, HEAD_DIM)


def _compress(kvrows, pe_t, w2):
    B, T, _ = kvrows.shape
    nblk = T // CMP_BLOCK
    x = kvrows.reshape(B, T, 16, HEAD_DIM)
    return pl.pallas_call(
        functools.partial(_compress_kernel, nblk=nblk),
        grid=(B,),
        in_specs=[pl.BlockSpec((None, T, 8, HEAD_DIM), lambda b: (b, 0, 0, 0)),
                  pl.BlockSpec((CMP_BLOCK, 8, HEAD_DIM), lambda b: (0, 0, 0)),
                  pl.BlockSpec((CMP_BLOCK // 2, 2 * HEAD_DIM, 2 * HEAD_DIM), lambda b: (0, 0, 0))],
        out_specs=pl.BlockSpec((None, nblk, 8, HEAD_DIM), lambda b: (b, 0, 0, 0)),
        out_shape=jax.ShapeDtypeStruct((B, nblk, 8, HEAD_DIM), F32),
        compiler_params=_cparams(("parallel",)),
    )(x, pe_t, w2)


def _compress_paged_kernel(pt_ref, *refs, pb):
    page_refs = refs[:pb]
    pe_ref, w_ref, o_ref = refs[pb:pb + 3]
    rows = pb * 2 * 8

    def load(l):
        tiles = []
        for p in range(pb):
            for h in range(2):
                tiles.append(page_refs[p][h * CMP_BLOCK + l])
        return jnp.stack(tiles, axis=0)

    out = _compress_acc(load, pe_ref, w_ref, rows)
    o_ref[...] = out.reshape(pb * 2, 8, HEAD_DIM)


def _compress_paged(cache5, page_table, layer, pe_t, w2, *, pb):
    n_phys, page = cache5.shape[1], cache5.shape[2]
    B, n_pages = page_table.shape
    assert page == 2 * CMP_BLOCK and n_pages % pb == 0

    def page_spec(p):
        return pl.BlockSpec((None, None, page, 8, HEAD_DIM),
                            lambda b, i, pt: (layer, pt[b, i * pb + p], 0, 0, 0))

    grid_spec = pltpu.PrefetchScalarGridSpec(
        num_scalar_prefetch=1,
        grid=(B, n_pages // pb),
        in_specs=[page_spec(p) for p in range(pb)] + [
            pl.BlockSpec((CMP_BLOCK, 8, HEAD_DIM), lambda b, i, pt: (0, 0, 0)),
            pl.BlockSpec((CMP_BLOCK // 2, 2 * HEAD_DIM, 2 * HEAD_DIM), lambda b, i, pt: (0, 0, 0))],
        out_specs=pl.BlockSpec((None, pb * 2, 8, HEAD_DIM), lambda b, i, pt: (b, i, 0, 0)),
    )
    return pl.pallas_call(
        functools.partial(_compress_paged_kernel, pb=pb),
        grid_spec=grid_spec,
        out_shape=jax.ShapeDtypeStruct((B, n_pages * 2, 8, HEAD_DIM), F32),
        compiler_params=_cparams(("parallel", "arbitrary")),
    )(page_table, *([cache5] * pb), pe_t, w2)


def _online_step(s2, v, m_ref, l_ref, acc_ref):
    tk = s2.shape[1]
    m_prev = m_ref[...]
    m_next = jnp.maximum(m_prev, jnp.max(s2, axis=1, keepdims=True))
    alpha = jnp.exp(m_prev - m_next)
    p = jnp.exp(s2 - _rep_lanes(m_next, tk))
    l_ref[...] = alpha * l_ref[...] + jnp.sum(p, axis=1, keepdims=True)
    m_ref[...] = m_next
    acc_ref[...] = alpha * acc_ref[...] + _dot(p.astype(BF), v)


def _nsa_kernel(q_ref, kc_ref, vc_ref, ks_ref, vs_ref, kw_ref, vw_ref, g_ref, o_ref,
                m_ref, l_ref, acc_ref, *, tq, nblk):
    qi = pl.program_id(2)
    q0 = qi * tq
    rows = HPG * tq
    q = q_ref[...]
    qcat = jnp.concatenate([q[:, h * HEAD_DIM:(h + 1) * HEAD_DIM] for h in range(HPG)], axis=0)

    kc = kc_ref[...].astype(BF)
    vc = vc_ref[...].astype(BF)
    t_b = q0 + lax.broadcasted_iota(jnp.int32, (tq, nblk), 0)
    blk = lax.broadcasted_iota(jnp.int32, (tq, nblk), 1)
    complete = ((blk + 1) * CMP_BLOCK - 1 <= t_b)[None]
    s3 = jnp.where(complete, _dot_nt(qcat, kc).reshape(HPG, tq, nblk), NEG)
    mx = jnp.max(s3, axis=-1, keepdims=True)
    e = jnp.where(complete, jnp.exp(s3 - mx), 0.0)
    d = jnp.sum(e, axis=-1, keepdims=True)
    p3 = e / jnp.where(d > 0, d, 1.0)
    o_cmp = _dot(p3.reshape(rows, nblk).astype(BF), vc)

    imp = jnp.sum(p3, axis=0)
    cur = jnp.right_shift(t_b, CMP_SHIFT)
    forced = (blk == 0) | (blk == cur) | (blk == cur - 1)
    score = jnp.where(forced, FORCED_SCORE, imp)
    score = jnp.where(blk <= cur, score, -1.0)
    rank = jnp.zeros((tq, nblk), F32)
    for i in range(nblk):
        si = score[:, i:i + 1]
        beats = (si > score) | ((si == score) & (blk > i))
        rank = rank + jnp.where(beats, 1.0, 0.0)
    sel = jnp.where((rank < min(N_SELECT, nblk)) & (score > -0.5), 1.0, 0.0).astype(BF)

    t_k = q0 + lax.broadcasted_iota(jnp.int32, (tq, tq), 0)
    k_off = lax.broadcasted_iota(jnp.int32, (tq, tq), 1)
    e_blk = lax.broadcasted_iota(jnp.int32, (nblk, tq), 0)
    e_off = lax.broadcasted_iota(jnp.int32, (nblk, tq), 1)

    def reset():
        m_ref[...] = jnp.full((rows, LANES), NEG, F32)
        l_ref[...] = jnp.zeros((rows, LANES), F32)
        acc_ref[...] = jnp.zeros((rows, HEAD_DIM), F32)

    def attend(c, k_ref, v_ref, mask):
        start = pl.multiple_of(c * tq, tq)
        k = k_ref[pl.ds(start, tq), :]
        v = v_ref[pl.ds(start, tq), :]
        s = jnp.where(mask[None], _dot_nt(qcat, k).reshape(HPG, tq, tq), NEG)
        _online_step(s.reshape(rows, tq), v, m_ref, l_ref, acc_ref)

    def sel_body(c, carry):
        expand = jnp.where(jnp.right_shift(c * tq + e_off, CMP_SHIFT) == e_blk, 1.0, 0.0).astype(BF)
        chosen = _dot(sel, expand) > 0.5
        attend(c, ks_ref, vs_ref, chosen & (c * tq + k_off <= t_k))
        return carry

    reset()
    lax.fori_loop(0, qi + 1, sel_body, 0)
    o_sel = acc_ref[...] / l_ref[...]

    def win_body(c, carry):
        dist = t_k - (c * tq + k_off)
        attend(c, kw_ref, vw_ref, (dist >= 0) & (dist < WINDOW))
        return carry

    reset()
    lax.fori_loop(jnp.maximum(qi - WINDOW // tq, 0), qi + 1, win_body, 0)
    o_win = acc_ref[...] / l_ref[...]

    g = jax.nn.sigmoid(g_ref[...])
    for h in range(HPG):
        r = slice(h * tq, (h + 1) * tq)
        o = (g[:, 3 * h:3 * h + 1] * o_cmp[r] + g[:, 3 * h + 1:3 * h + 2] * o_sel[r]
             + g[:, 3 * h + 2:3 * h + 3] * o_win[r])
        o_ref[:, h * HEAD_DIM:(h + 1) * HEAD_DIM] = o.astype(BF)


def _nsa_prompt(q, kcv, kvb, zg, *, tq):
    B, T, _ = q.shape
    nblk = kcv.shape[1]
    col = lambda c: pl.BlockSpec((None, T, HEAD_DIM), lambda b, g, i, c=c: (b, 0, c + g))
    rows = HPG * tq
    return pl.pallas_call(
        functools.partial(_nsa_kernel, tq=tq, nblk=nblk),
        grid=(B, N_KV, T // tq),
        in_specs=[pl.BlockSpec((None, tq, HPG * HEAD_DIM), lambda b, g, i: (b, i, g)),
                  pl.BlockSpec((None, nblk, HEAD_DIM), lambda b, g, i: (b, 0, g)),
                  pl.BlockSpec((None, nblk, HEAD_DIM), lambda b, g, i: (b, 0, N_KV + g)),
                  col(2 * N_KV), col(3 * N_KV), col(4 * N_KV), col(5 * N_KV),
                  pl.BlockSpec((None, tq, LANES), lambda b, g, i: (b, i, g))],
        out_specs=pl.BlockSpec((None, tq, HPG * HEAD_DIM), lambda b, g, i: (b, i, g)),
        out_shape=jax.ShapeDtypeStruct((B, T, N_HEADS * HEAD_DIM), BF),
        scratch_shapes=[pltpu.VMEM((rows, LANES), F32), pltpu.VMEM((rows, LANES), F32),
                        pltpu.VMEM((rows, HEAD_DIM), F32)],
        compiler_params=_cparams(("parallel", "parallel", "arbitrary")),
    )(q, kcv, kcv, kvb, kvb, kvb, kvb, zg)


def _softplus(y):
    return jnp.maximum(y, 0.0) + jnp.log1p(jnp.exp(-jnp.abs(y)))


def _rg_gates(u, w_ref, brg_ref, lam_ref):
    ub = u.astype(BF)
    r = jax.nn.sigmoid(_dot(ub, w_ref[0].astype(BF)) + brg_ref[0:1, :])
    i = jax.nn.sigmoid(_dot(ub, w_ref[1].astype(BF)) + brg_ref[1:2, :])
    log_a = -RG_C * r * _softplus(-lam_ref[...])
    a = jnp.exp(log_a)
    th = jnp.tanh(log_a)
    b = jnp.sqrt(-2.0 * th / (1.0 - th)) * i * u
    return a, b


def _rglru_kernel(xr_ref, zg_ref, c0_ref, h0_ref, cw_ref, cb_ref, w_ref, brg_ref, lam_ref,
                  y_ref, hl_ref, cl_ref, xp_ref, a_ref, b_ref, h_ref, *, T):
    pad = 8
    xp_ref[0:pad, :] = jnp.zeros((pad, RNN_BW), F32)
    xp_ref[pad - (CONV_W - 1):pad, :] = c0_ref[...]
    xp_ref[pad:pad + T, :] = xr_ref[...]
    u = cb_ref[...] + xp_ref[pad:pad + T, :] * cw_ref[CONV_W - 1:CONV_W, :]
    for k in range(CONV_W - 1):
        off = pad - (CONV_W - 1) + k
        u = u + xp_ref[off:off + T, :] * cw_ref[k:k + 1, :]
    cl_ref[...] = xp_ref[pad + T - (CONV_W - 1):pad + T, :]

    a, b = _rg_gates(u, w_ref, brg_ref, lam_ref)
    a_ref[0:T, :] = a
    b_ref[0:T, :] = b

    levels = []
    n, off = T, 0
    while True:
        levels.append((n, off))
        if n <= 8:
            break
        off += n
        n //= 8
    for (n, off), (n2, off2) in zip(levels[:-1], levels[1:]):
        A = a_ref[pl.ds(off, n2, stride=8), :]
        Bv = b_ref[pl.ds(off, n2, stride=8), :]
        for r in range(1, 8):
            ar = a_ref[pl.ds(off + r, n2, stride=8), :]
            br = b_ref[pl.ds(off + r, n2, stride=8), :]
            Bv = ar * Bv + br
            A = ar * A
        a_ref[off2:off2 + n2, :] = A
        b_ref[off2:off2 + n2, :] = Bv
    n, off = levels[-1]
    h = h0_ref[...]
    for t in range(n):
        h = a_ref[off + t:off + t + 1, :] * h + b_ref[off + t:off + t + 1, :]
        h_ref[off + 8 + t:off + 9 + t, :] = h
    for (n, off), (n2, off2) in reversed(list(zip(levels[:-1], levels[1:]))):
        h_ref[off2 + 7:off2 + 8, :] = h0_ref[...]
        hp = h_ref[off2 + 7:off2 + 7 + n2, :]
        for r in range(8):
            ar = a_ref[pl.ds(off + r, n2, stride=8), :]
            br = b_ref[pl.ds(off + r, n2, stride=8), :]
            hp = ar * hp + br
            h_ref[pl.ds(off + 8 + r, n2, stride=8), :] = hp
    hfull = h_ref[8:8 + T, :]
    y_ref[...] = (jax.nn.gelu(zg_ref[...]) * hfull).astype(BF)
    hl_ref[...] = h_ref[8 + T - 1:8 + T, :]


def _rglru(zr, conv0, h0, conv_w, conv_b, w_rg, b_rg, lam):
    B, T, two_d = zr.shape
    d_rnn = two_d // 2
    nb = d_rnn // RNN_BW
    assert T % 8 == 0
    total = T + T // 4 + 64
    return pl.pallas_call(
        functools.partial(_rglru_kernel, T=T),
        grid=(B, nb),
        in_specs=[pl.BlockSpec((None, T, RNN_BW), lambda b, n: (b, 0, n)),
                  pl.BlockSpec((None, T, RNN_BW), lambda b, n: (b, 0, nb + n)),
                  pl.BlockSpec((None, CONV_W - 1, RNN_BW), lambda b, n: (b, 0, n)),
                  pl.BlockSpec((None, 1, RNN_BW), lambda b, n: (b, 0, n)),
                  pl.BlockSpec((CONV_W, RNN_BW), lambda b, n: (0, n)),
                  pl.BlockSpec((1, RNN_BW), lambda b, n: (0, n)),
                  pl.BlockSpec((2, None, RNN_BW, RNN_BW), lambda b, n: (0, n, 0, 0)),
                  pl.BlockSpec((2, RNN_BW), lambda b, n: (0, n)),
                  pl.BlockSpec((1, RNN_BW), lambda b, n: (0, n))],
        out_specs=[pl.BlockSpec((None, T, RNN_BW), lambda b, n: (b, 0, n)),
                   pl.BlockSpec((None, 1, RNN_BW), lambda b, n: (b, 0, n)),
                   pl.BlockSpec((None, CONV_W - 1, RNN_BW), lambda b, n: (b, 0, n))],
        out_shape=[jax.ShapeDtypeStruct((B, T, d_rnn), BF),
                   jax.ShapeDtypeStruct((B, 1, d_rnn), F32),
                   jax.ShapeDtypeStruct((B, CONV_W - 1, d_rnn), F32)],
        scratch_shapes=[pltpu.VMEM((T + 8, RNN_BW), F32), pltpu.VMEM((total, RNN_BW), F32),
                        pltpu.VMEM((total, RNN_BW), F32), pltpu.VMEM((total + 8, RNN_BW), F32)],
        compiler_params=_cparams(("parallel", "arbitrary")),
    )(zr, zr, conv0, h0, conv_w, conv_b[None], w_rg, b_rg, lam[None])


def _rglru_step_kernel(xr_ref, zg_ref, c0_ref, h0_ref, cw_ref, cb_ref, w_ref, brg_ref, lam_ref,
                       y_ref, hl_ref):
    u = cb_ref[...] + xr_ref[...] * cw_ref[CONV_W - 1:CONV_W, :]
    for k in range(CONV_W - 1):
        u = u + c0_ref[k] * cw_ref[k:k + 1, :]
    a, b = _rg_gates(u, w_ref, brg_ref, lam_ref)
    h = a * h0_ref[...] + b
    hl_ref[...] = h
    y_ref[...] = (jax.nn.gelu(zg_ref[...]) * h).astype(BF)


def _rglru_step(zr, conv0_t, h0, conv_w, conv_b, w_rg, b_rg, lam):
    R, two_d = zr.shape
    d_rnn = two_d // 2
    nb = d_rnn // RNN_BW
    return pl.pallas_call(
        _rglru_step_kernel,
        grid=(nb,),
        in_specs=[pl.BlockSpec((R, RNN_BW), lambda n: (0, n)),
                  pl.BlockSpec((R, RNN_BW), lambda n: (0, nb + n)),
                  pl.BlockSpec((CONV_W - 1, R, RNN_BW), lambda n: (0, 0, n)),
                  pl.BlockSpec((R, RNN_BW), lambda n: (0, n)),
                  pl.BlockSpec((CONV_W, RNN_BW), lambda n: (0, n)),
                  pl.BlockSpec((1, RNN_BW), lambda n: (0, n)),
                  pl.BlockSpec((2, None, RNN_BW, RNN_BW), lambda n: (0, n, 0, 0)),
                  pl.BlockSpec((2, RNN_BW), lambda n: (0, n)),
                  pl.BlockSpec((1, RNN_BW), lambda n: (0, n))],
        out_specs=[pl.BlockSpec((R, RNN_BW), lambda n: (0, n)),
                   pl.BlockSpec((R, RNN_BW), lambda n: (0, n))],
        out_shape=[jax.ShapeDtypeStruct((R, d_rnn), BF), jax.ShapeDtypeStruct((R, d_rnn), F32)],
        compiler_params=_cparams(("arbitrary",)),
    )(zr, zr, conv0_t, h0, conv_w, conv_b[None], w_rg, b_rg, lam[None])


def _dec_cmp_kernel(q_ref, kc_ref, vc_ref, o_ref, idx_ref, *, n_past, t_pos):
    q = q_ref[...]
    nlan = idx_ref.shape[-1] * ((n_past + 1 + LANES - 1) // LANES)
    blk = lax.broadcasted_iota(jnp.int32, (1, n_past), 1)
    complete = (blk + 1) * CMP_BLOCK - 1 <= t_pos
    s = jnp.where(complete, _dot_nt(q, kc_ref[...].astype(BF)), NEG)
    mx = jnp.max(s, axis=-1, keepdims=True)
    e = jnp.where(complete, jnp.exp(s - mx), 0.0)
    d = jnp.sum(e, axis=-1, keepdims=True)
    p = e / jnp.where(d > 0, d, 1.0)
    o_ref[...] = _dot(p.astype(BF), vc_ref[...].astype(BF))
    head = lax.broadcasted_iota(jnp.int32, p.shape, 0) < HPG
    imp = jnp.sum(jnp.where(head, p, 0.0), axis=0, keepdims=True)

    cur = t_pos // CMP_BLOCK
    imp_all = jnp.concatenate([imp, jnp.zeros((1, nlan - n_past), F32)], axis=1)
    blk_all = lax.broadcasted_iota(jnp.int32, (1, nlan), 1)
    forced = (blk_all == 0) | (blk_all == cur) | (blk_all == cur - 1)
    score = jnp.where(forced, FORCED_SCORE, imp_all)
    score = jnp.where(blk_all <= cur, score, -1.0)
    score = jnp.where(blk_all <= n_past, score, -2.0)
    blk_f = blk_all.astype(F32)
    slot = lax.broadcasted_iota(jnp.int32, (1, LANES), 1)
    res = jnp.full((1, LANES), float(n_past), F32)
    for k in range(min(N_SELECT, n_past + 1)):
        top = jnp.max(score, axis=-1, keepdims=True)
        arg = jnp.min(jnp.where(score == top, blk_f, float(nlan)), axis=-1, keepdims=True)
        res = jnp.where(slot == k, jnp.where(top > -0.5, arg, float(n_past)), res)
        score = jnp.where(blk_f == arg, -3.0, score)
    idx_ref[...] = res.astype(jnp.int32)


def _dec_cmp(q16, kcv, *, t_pos):
    B = q16.shape[0]
    n_past = kcv.shape[1]
    return pl.pallas_call(
        functools.partial(_dec_cmp_kernel, n_past=n_past, t_pos=t_pos),
        grid=(B, N_KV),
        in_specs=[pl.BlockSpec((None, None, 16, HEAD_DIM), lambda b, g: (b, g, 0, 0)),
                  pl.BlockSpec((None, n_past, HEAD_DIM), lambda b, g: (b, 0, g)),
                  pl.BlockSpec((None, n_past, HEAD_DIM), lambda b, g: (b, 0, N_KV + g))],
        out_specs=[pl.BlockSpec((None, None, 16, HEAD_DIM), lambda b, g: (b, g, 0, 0)),
                   pl.BlockSpec((None, None, 1, LANES), lambda b, g: (b, g, 0, 0))],
        out_shape=[jax.ShapeDtypeStruct((B, N_KV, 16, HEAD_DIM), F32),
                   jax.ShapeDtypeStruct((B, N_KV, 1, LANES), jnp.int32)],
        compiler_params=_cparams(("parallel", "arbitrary")),
    )(q16, kcv, kcv)


def _dec_attn_kernel(pt_ref, ti_ref, q_ref, ks_ref, vs_ref, new_ref, kw_ref, vw_ref, oc_ref, g_ref,
                     o_ref, m_ref, l_ref, acc_ref, *, n_past):
    b, g, s = pl.program_id(0), pl.program_id(1), pl.program_id(2)
    q = q_ref[...]
    qf = q.astype(F32)

    @pl.when(s == 0)
    def _():
        m_ref[...] = jnp.full(m_ref.shape, NEG, F32)
        l_ref[...] = jnp.zeros(l_ref.shape, F32)
        acc_ref[...] = jnp.zeros(acc_ref.shape, F32)

    blk = ti_ref[b * N_KV + g, s]
    live = blk < n_past
    sc = jnp.where(live, _dot_nt(q, ks_ref[...].astype(BF)), NEG)
    m_prev = m_ref[...]
    m_next = jnp.maximum(m_prev, jnp.max(sc, axis=1, keepdims=True))
    alpha = jnp.exp(m_prev - m_next)
    p = jnp.where(live, jnp.exp(sc - m_next[:, :CMP_BLOCK]), 0.0)
    l_ref[...] = alpha * l_ref[...] + jnp.sum(p, axis=1, keepdims=True)
    m_ref[...] = m_next
    acc_ref[...] = alpha * acc_ref[...] + _dot(p.astype(BF), vs_ref[...].astype(BF))

    @pl.when(s == pl.num_programs(2) - 1)
    def _():
        def self_score(k_row):
            return jnp.sum(qf * k_row.astype(BF).astype(F32), axis=1, keepdims=True)

        s_self = self_score(new_ref[2:3, :])
        v_self = new_ref[3:4, :].astype(BF).astype(F32)
        m_prev = m_ref[...]
        m_fin = jnp.maximum(m_prev, s_self)
        alpha = jnp.exp(m_prev - m_fin)
        p_self = jnp.exp(s_self - m_fin)
        o_sel = (alpha * acc_ref[...] + p_self * v_self) / (alpha * l_ref[...] + p_self)

        n_win = kw_ref.shape[0]
        keep = lax.broadcasted_iota(jnp.int32, (1, n_win), 1) >= n_win + 1 - WINDOW
        sw = jnp.where(keep, _dot_nt(q, kw_ref[...].astype(BF)), NEG)
        sw_self = self_score(new_ref[4:5, :])
        mw = jnp.maximum(jnp.max(sw, axis=1, keepdims=True), sw_self)
        pw = jnp.where(keep, jnp.exp(sw - mw), 0.0)
        pw_self = jnp.exp(sw_self - mw)
        num = _dot(pw.astype(BF), vw_ref[...].astype(BF)) + pw_self * new_ref[5:6, :].astype(BF).astype(F32)
        o_win = num / (jnp.sum(pw, axis=1, keepdims=True) + pw_self)

        gt = jax.nn.sigmoid(g_ref[...])
        o_ref[...] = gt[:, 0:1] * oc_ref[...] + gt[:, 1:2] * o_sel + gt[:, 2:3] * o_win


def _dec_attn(q16, cache3, page_table, top_i, newkv, win3, o_cmp, g16, *, layer, n_phys, n_past):
    B = q16.shape[0]
    n_win = win3.shape[1]
    n_slots = top_i.shape[1]

    def half_page(b, g, s, pt, ti):
        blk = jnp.minimum(ti[b * N_KV + g, s], n_past - 1)
        return (layer * n_phys + pt[b, blk // 2]) * 2 + blk % 2

    bg = lambda b, g, s, pt, ti: (b, g, 0, 0)
    grid_spec = pltpu.PrefetchScalarGridSpec(
        num_scalar_prefetch=2,
        grid=(B, N_KV, n_slots),
        in_specs=[pl.BlockSpec((None, None, 16, HEAD_DIM), bg),
                  pl.BlockSpec((None, CMP_BLOCK, HEAD_DIM),
                               lambda b, g, s, pt, ti: (half_page(b, g, s, pt, ti), 0, 2 * N_KV + g)),
                  pl.BlockSpec((None, CMP_BLOCK, HEAD_DIM),
                               lambda b, g, s, pt, ti: (half_page(b, g, s, pt, ti), 0, 3 * N_KV + g)),
                  pl.BlockSpec((None, None, 6, HEAD_DIM), bg),
                  pl.BlockSpec((None, n_win, HEAD_DIM), lambda b, g, s, pt, ti: (layer * B + b, 0, g)),
                  pl.BlockSpec((None, n_win, HEAD_DIM), lambda b, g, s, pt, ti: (layer * B + b, 0, N_KV + g)),
                  pl.BlockSpec((None, None, 16, HEAD_DIM), bg),
                  pl.BlockSpec((None, None, 16, LANES), bg)],
        out_specs=pl.BlockSpec((None, None, 16, HEAD_DIM), bg),
        scratch_shapes=[pltpu.VMEM((16, LANES), F32), pltpu.VMEM((16, LANES), F32),
                        pltpu.VMEM((16, HEAD_DIM), F32)],
    )
    return pl.pallas_call(
        functools.partial(_dec_attn_kernel, n_past=n_past),
        grid_spec=grid_spec,
        out_shape=jax.ShapeDtypeStruct((B, N_KV, 16, HEAD_DIM), F32),
        compiler_params=_cparams(("parallel", "parallel", "arbitrary")),
    )(page_table, top_i, q16, cache3, cache3, newkv, win3, win3, o_cmp, g16)


def _layer_weights(l, w_in, qk_g, w_phi, pe_cmp, w_branch_nsa, w_branch_rnn, w_out, w_mlp1, w_mlp2):
    d_attn = N_HEADS * HEAD_DIM
    o_gn = d_attn + 6 * N_KV * HEAD_DIM
    o_rx = o_gn + 3 * N_HEADS
    d_model = w_in.shape[1]
    o_mg = w_in.shape[2] - 2 * d_model
    wi = w_in[l]
    wg = wi[:, o_gn:o_rx].reshape(d_model, N_KV, 3 * HPG)
    wg = jnp.pad(wg, ((0, 0), (0, 0), (0, LANES - 3 * HPG))).reshape(d_model, N_KV * LANES)
    w2 = jnp.concatenate([w_phi[l, 0], w_phi[l, 1]], axis=-1).reshape(CMP_BLOCK // 2, 2 * HEAD_DIM, 2 * HEAD_DIM)
    pe_t = jnp.concatenate([jnp.broadcast_to(pe_cmp[l, j][:, None, :], (CMP_BLOCK, N_KV, HEAD_DIM))
                            for j in range(2)], axis=1)
    return dict(
        w_qkv=wi[:, :o_gn].astype(BF), w_g=wg.astype(BF), w_r=wi[:, o_rx:o_mg].astype(BF),
        w_mg=wi[:, o_mg:].astype(BF), w2=w2.astype(BF), pe_t=pe_t, qk_g=qk_g[l],
        w_nsa=w_branch_nsa[l].astype(BF), w_rnn=w_branch_rnn[l].astype(BF), w_out=w_out[l].astype(BF),
        w_mlp1=w_mlp1[l].astype(BF), w_mlp2=w_mlp2[l].astype(BF))


def _prompt_layer(x, mod, W, g_ln, rnn, rope):
    B, T, D = x.shape
    M = B * T
    sh1, sc1, ga1, sh2, sc2, ga2 = mod
    tm = 1024
    bmap = lambda i, j, k: (i // (T // tm), 0, j)

    h = _ln_mod(x, g_ln[0:1], sc1, sh1, tt=256).reshape(M, D)
    z_qkv = _mm(h, W["w_qkv"], tm=tm, tn=1024)
    z_g = _mm(h, W["w_g"], tm=tm, tn=512)
    z_r = _mm(h, W["w_r"], tm=tm, tn=1024)
    z_mg = _mm(h, W["w_mg"], tm=tm, tn=1024)

    q, kvrows, winrows, kvb = _qk_prep(z_qkv.reshape(B, T, -1), rope[0], rope[1], W["qk_g"], tt=256)
    kcv = _compress(kvrows, W["pe_t"], W["w2"])
    o_nsa = _nsa_prompt(q, kcv.reshape(B, T // CMP_BLOCK, 8 * HEAD_DIM), kvb,
                        z_g.reshape(B, T, -1), tq=256)

    conv0 = jnp.zeros((B, CONV_W - 1, D // 2), F32)
    h0 = jnp.zeros((B, 1, D // 2), F32)
    y_rnn, h_last, conv_last = _rglru(z_r.reshape(B, T, -1), conv0, h0, *rnn)

    merged = _merge(o_nsa.reshape(M, -1), W["w_nsa"], y_rnn.reshape(M, -1), W["w_rnn"], z_mg,
                    tm=tm, tn=512)
    x1 = _mm(merged, W["w_out"], tm=tm, tn=512, epi="resid", x_res=x.reshape(M, D), ga=ga1, ga_map=bmap)
    h2 = _ln_mod(x1.reshape(B, T, D), g_ln[1:2], sc2, sh2, tt=256).reshape(M, D)
    u = _mm(h2, W["w_mlp1"], tm=tm, tn=1024, out_dtype=BF, epi="relu2")
    x2 = _mm(u, W["w_mlp2"], tm=tm, tn=512, tk=4096, epi="resid", x_res=x1, ga=ga2, ga_map=bmap)

    n_keep = min(WINDOW, T)
    return (x2.reshape(B, T, D), kvrows.reshape(B, T, 4, N_KV, HEAD_DIM),
            winrows[:, T - n_keep:].reshape(B, n_keep, 2, N_KV, HEAD_DIM),
            h_last.reshape(B, -1), conv_last)


def _decode_layer(x, mod, W, g_ln, rnn, rope, l, cache_kv, page_table, state_win, state_rnn, state_conv):
    R, D = x.shape
    nb = page_table.shape[0]
    depth, n_phys, page = cache_kv.shape[:3]
    n_past = page_table.shape[1] * page // CMP_BLOCK
    t_pos = page_table.shape[1] * page
    sh1, sc1, ga1, sh2, sc2, ga2 = mod
    rmap = lambda i, j, k: (0, 0, j)

    h = _ln_mod(x[None], g_ln[0:1], sc1, sh1, tt=R).reshape(R, D)
    z_qkv = _mm(h, W["w_qkv"], tm=R, tn=1024)
    z_g = _mm(h, W["w_g"], tm=R, tn=512)
    z_r = _mm(h, W["w_r"], tm=R, tn=1024)
    z_mg = _mm(h, W["w_mg"], tm=R, tn=1024)

    q, kvrows, winrows, _ = _qk_prep(z_qkv[None], rope[0], rope[1], W["qk_g"], tt=R)
    q16 = jnp.pad(q[0, :nb].reshape(nb, N_KV, HPG, HEAD_DIM), ((0, 0), (0, 0), (0, 16 - HPG), (0, 0)))
    g16 = jnp.pad(z_g[:nb].reshape(nb, N_KV, LANES)[:, :, :3 * HPG].reshape(nb, N_KV, HPG, 3),
                  ((0, 0), (0, 0), (0, 16 - HPG), (0, LANES - 3)))
    new6 = jnp.concatenate([kvrows[0, :nb], winrows[0, :nb]], axis=-1)
    new6 = new6.reshape(nb, 6, N_KV, HEAD_DIM).transpose(0, 2, 1, 3)

    cache5 = cache_kv.reshape(depth, n_phys, page, 4 * N_KV, HEAD_DIM)
    kcv = _compress_paged(cache5, page_table, l, W["pe_t"], W["w2"], pb=32)
    o_cmp, top = _dec_cmp(q16, kcv.reshape(nb, n_past, 8 * HEAD_DIM), t_pos=t_pos)
    top_i = top[:, :, 0, :N_SELECT].reshape(nb * N_KV, N_SELECT)
    cache3 = cache_kv.reshape(depth * n_phys * 2, CMP_BLOCK, 4 * N_KV * HEAD_DIM)
    n_win = state_win.shape[2]
    win3 = state_win.reshape(depth * nb, n_win, 2 * N_KV * HEAD_DIM)
    o16 = _dec_attn(q16, cache3, page_table, top_i, new6, win3, o_cmp, g16,
                    layer=l, n_phys=n_phys, n_past=n_past)
    o_nsa = jnp.pad(o16[:, :, :HPG].reshape(nb, -1), ((0, R - nb), (0, 0))).astype(BF)

    pad_r = lambda a: jnp.pad(a, ((0, R - nb),) + ((0, 0),) * (a.ndim - 1))
    conv0_t = pad_r(state_conv[l]).transpose(1, 0, 2)
    y_rnn, h_new = _rglru_step(z_r, conv0_t, pad_r(state_rnn[l]), *rnn)

    merged = _merge(o_nsa, W["w_nsa"], y_rnn, W["w_rnn"], z_mg, tm=R, tn=1024)
    x1 = _mm(merged, W["w_out"], tm=R, tn=1024, epi="resid", x_res=x, ga=ga1, ga_map=rmap)
    h2 = _ln_mod(x1[None], g_ln[1:2], sc2, sh2, tt=R).reshape(R, D)
    u = _mm(h2, W["w_mlp1"], tm=R, tn=1024, out_dtype=BF, epi="relu2")
    x2 = _mm(u, W["w_mlp2"], tm=R, tn=1024, tk=4096, epi="resid", x_res=x1, ga=ga2, ga_map=rmap)

    kv_new = kvrows[0, :nb].reshape(nb, 1, 4, N_KV, HEAD_DIM)
    win_new = jnp.concatenate([state_win[l][:, 1:], winrows[0, :nb].reshape(nb, 1, 2, N_KV, HEAD_DIM)], axis=1)
    xr = z_r[:nb, :D // 2]
    conv_new = jnp.concatenate([state_conv[l][:, 1:], xr[:, None, :]], axis=1)
    return x2, kv_new, win_new, h_new[:nb], conv_new


def kernel(x_prompt, x_sample, cache_kv, state_win, state_rnn, state_conv, page_table, c_prompt, c_sample,
           w_ada, b_ada, g_ln, w_in, qk_g, w_phi, pe_cmp, conv_w, conv_b, w_rg, b_rg, lam,
           w_branch_nsa, w_branch_rnn, w_out, w_mlp1, w_mlp2):
    depth = w_ada.shape[0]
    B, T, D = x_prompt.shape
    nb = x_sample.shape[0]
    R = 16
    past_len = page_table.shape[1] * cache_kv.shape[2]

    c_all = jnp.zeros((2 * R, D), F32).at[:B].set(c_prompt).at[R:R + nb].set(c_sample)
    rope_p = _rope_tables(0.0, T)
    rope_s = tuple(jnp.broadcast_to(t, (R, HEAD_DIM)) for t in _rope_tables(float(past_len), 1))

    yp = x_prompt
    ys = jnp.pad(x_sample.reshape(nb, D), ((0, R - nb), (0, 0)))
    outs = [[] for _ in range(8)]
    for l in range(depth):
        mod = _mm(c_all, w_ada[l], tm=2 * R, tn=512, epi="silu_bias", bias=b_ada[l][None])
        mod = mod.reshape(2 * R, 6, D)
        mod_p = [mod[:B, k][:, None, :] for k in range(6)]
        mod_s = [mod[R:, k][None] for k in range(6)]
        W = _layer_weights(l, w_in, qk_g, w_phi, pe_cmp, w_branch_nsa, w_branch_rnn, w_out, w_mlp1, w_mlp2)
        rnn = (conv_w[l], conv_b[l], w_rg[l], b_rg[l], lam[l])
        yp, a1, a2, a3, a4 = _prompt_layer(yp, mod_p, W, g_ln[l], rnn, rope_p)
        ys, b1, b2, b3, b4 = _decode_layer(ys, mod_s, W, g_ln[l], rnn, rope_s, l, cache_kv, page_table,
                                           state_win, state_rnn, state_conv)
        for lst, v in zip(outs, (a1, b1, a2, b2, a3, b3, a4, b4)):
            lst.append(v)
    kv_p, kv_s, win_p, win_s, h_p, h_s, cv_p, cv_s = (jnp.stack(v) for v in outs)
    return (yp, ys[:nb].reshape(nb, 1, D), kv_p, kv_s, win_p, win_s, h_p, h_s, cv_p, cv_s)
```

```python
import functools
import math

import jax
import jax.numpy as jnp
from jax import lax
from jax.experimental import pallas as pl
from jax.experimental.pallas import tpu as pltpu

BF = jnp.bfloat16
F32 = jnp.float32

HEAD_DIM = 128
N_KV = 4
HPG = 4
N_HEADS = N_KV * HPG
CMP_BLOCK = 64
CMP_SHIFT = 6
N_SELECT = 16
WINDOW = 512
FORCED_SCORE = 1e4
RNN_BW = 128
CONV_W = 4
RG_C = 8.0
ROPE_THETA = 10000.0
EPS = 1e-6
NEG = -1e30
LANES = 128
VMEM_LIMIT = 56 * 1024 * 1024


def _cparams(sem):
    return pltpu.CompilerParams(dimension_semantics=sem, vmem_limit_bytes=VMEM_LIMIT)


def _dot(a, b):
    return jnp.dot(a, b, preferred_element_type=F32)


def _dot_nt(a, b):
    return lax.dot_general(a, b, (((1,), (1,)), ((), ())), preferred_element_type=F32)


def _rep_lanes(x, n):
    return x if n == LANES else jnp.concatenate([x] * (n // LANES), axis=1)


def _mm_kernel(*refs, nk, epi):
    if epi == "resid":
        a_ref, w_ref, x_ref, ga_ref, o_ref = refs[:5]
        scratch = refs[5:]
    elif epi == "silu_bias":
        a_ref, w_ref, bias_ref, o_ref = refs[:4]
        scratch = refs[4:]
    else:
        a_ref, w_ref, o_ref = refs[:3]
        scratch = refs[3:]

    def finish(acc):
        if epi == "relu2":
            r = jnp.maximum(acc, 0.0)
            acc = r * r
        elif epi == "resid":
            acc = x_ref[...] + ga_ref[...] * acc
        elif epi == "silu_bias":
            acc = acc + bias_ref[...]
        o_ref[...] = acc.astype(o_ref.dtype)

    a = a_ref[...]
    if epi == "silu_bias":
        a = (a * jax.nn.sigmoid(a)).astype(BF)
    part = _dot(a, w_ref[...].astype(BF))
    if nk == 1:
        finish(part)
    else:
        acc_ref = scratch[0]
        k = pl.program_id(2)

        @pl.when(k == 0)
        def _():
            acc_ref[...] = part

        @pl.when(k > 0)
        def _():
            acc_ref[...] += part

        @pl.when(k == nk - 1)
        def _():
            finish(acc_ref[...])


def _mm(a, w, *, tm, tn, tk=None, out_dtype=F32, epi=None, x_res=None, ga=None, ga_map=None, bias=None,
        layer=0):
    M, K = a.shape
    N = w.shape[2]
    tk = K if tk is None else tk
    nk = K // tk
    grid = (M // tm, N // tn, nk)
    in_specs = [pl.BlockSpec((tm, tk), lambda i, j, k: (i, k)),
                pl.BlockSpec((None, tk, tn), lambda i, j, k: (layer, k, j))]
    args = [a, w]
    if epi == "resid":
        in_specs.append(pl.BlockSpec((tm, tn), lambda i, j, k: (i, j)))
        in_specs.append(pl.BlockSpec((None, ga.shape[1], tn), ga_map))
        args += [x_res, ga]
    elif epi == "silu_bias":
        in_specs.append(pl.BlockSpec((1, tn), lambda i, j, k: (0, j)))
        args.append(bias)
    scratch =[pltpu.VMEM((tm, tn), F32)] if nk > 1 else []
    return pl.pallas_call(
        functools.partial(_mm_kernel, nk=nk, epi=epi),
        grid=grid, in_specs=in_specs,
        out_specs=pl.BlockSpec((tm, tn), lambda i, j, k: (i, j)),
        out_shape=jax.ShapeDtypeStruct((M, N), out_dtype),
        scratch_shapes=scratch,
        compiler_params=_cparams(("parallel", "parallel", "arbitrary")),
    )(*args)


def _cached_bf16(w_ref, wbf_ref, first):
    if wbf_ref is None:
        return w_ref[...]

    @pl.when(first)
    def _():
        wbf_ref[...] = w_ref[...].astype(BF)

    return wbf_ref[...]


def _mm2_kernel(*refs, epi, cast_w):
    n_in = 7 if epi == "resid" else 3
    a_ref, a2_ref, w_ref = refs[:3]
    o_ref, o2_ref = refs[n_in:n_in + 2]
    wbf_ref = refs[n_in + 2] if cast_w else None
    first = pl.program_id(1) == 0
    w = _cached_bf16(w_ref, wbf_ref, first)

    def finish(acc, x_ref, ga_ref, out_ref):
        if epi == "relu2":
            r = jnp.maximum(acc, 0.0)
            acc = r * r
        elif epi == "resid":
            acc = x_ref[...] + ga_ref[...] * acc
        out_ref[...] = acc.astype(out_ref.dtype)

    extra = refs[3:7] if epi == "resid" else (None,) * 4
    finish(_dot(a_ref[...], w), extra[0], extra[1], o_ref)

    @pl.when(first)
    def _():
        finish(_dot(a2_ref[...], w), extra[2], extra[3], o2_ref)


def _mm2(a, a2, w, *, layer, tm, tn, out_dtype=F32, epi=None, res=None):
    M, K = a.shape
    R2 = a2.shape[0]
    N = w.shape[2]
    cast_w = w.dtype != BF
    in_specs = [pl.BlockSpec((tm, K), lambda j, i: (i, 0)),
                pl.BlockSpec((R2, K), lambda j, i: (0, 0)),
                pl.BlockSpec((None, K, tn), lambda j, i: (layer, 0, j))]
    args = [a, a2, w]
    if epi == "resid":
        x, ga, ga_map, x2, ga2 = res
        in_specs += [pl.BlockSpec((tm, tn), lambda j, i: (i, j)),
                     pl.BlockSpec((None, ga.shape[1], tn), ga_map),
                     pl.BlockSpec((R2, tn), lambda j, i: (0, j)),
                     pl.BlockSpec((None, R2, tn), lambda j, i: (0, 0, j))]
        args += [x, ga, x2, ga2]
    return pl.pallas_call(
        functools.partial(_mm2_kernel, epi=epi, cast_w=cast_w),
        grid=(N // tn, M // tm), in_specs=in_specs,
        out_specs=[pl.BlockSpec((tm, tn), lambda j, i: (i, j)),
                   pl.BlockSpec((R2, tn), lambda j, i: (0, j))],
        out_shape=[jax.ShapeDtypeStruct((M, N), out_dtype), jax.ShapeDtypeStruct((R2, N), out_dtype)],
        scratch_shapes=[pltpu.VMEM((K, tn), BF)] if cast_w else [],
        compiler_params=_cparams(("arbitrary", "arbitrary")),
    )(*args)


def _merge2_kernel(an_ref, ar_ref, gn_ref, gr_ref, an2_ref, ar2_ref, gn2_ref, gr2_ref, wn_ref, wr_ref,
                   o_ref, o2_ref, wnb_ref, wrb_ref):
    first = pl.program_id(1) == 0
    wn = _cached_bf16(wn_ref, wnb_ref, first)
    wr = _cached_bf16(wr_ref, wrb_ref, first)

    def finish(a_n, a_r, g_n, g_r, out_ref):
        o = jax.nn.sigmoid(g_n[...]) * _dot(a_n[...], wn) + jax.nn.sigmoid(g_r[...]) * _dot(a_r[...], wr)
        out_ref[...] = o.astype(out_ref.dtype)

    finish(an_ref, ar_ref, gn_ref, gr_ref, o_ref)

    @pl.when(first)
    def _():
        finish(an2_ref, ar2_ref, gn2_ref, gr2_ref, o2_ref)


def _merge2(o_nsa, o_rnn, z, o_nsa2, o_rnn2, z2, w_nsa, w_rnn, *, layer, col0, tm, tn):
    M, K = o_nsa.shape
    R2 = o_nsa2.shape[0]
    D = w_nsa.shape[2]
    c0, c1 = col0 // tn, (col0 + D) // tn
    row = lambda j, i: (i, 0)
    one = lambda j, i: (0, 0)
    return pl.pallas_call(
        _merge2_kernel,
        grid=(D // tn, M // tm),
        in_specs=[pl.BlockSpec((tm, K), row), pl.BlockSpec((tm, K), row),
                  pl.BlockSpec((tm, tn), lambda j, i: (i, j + c0)),
                  pl.BlockSpec((tm, tn), lambda j, i: (i, j + c1)),
                  pl.BlockSpec((R2, K), one), pl.BlockSpec((R2, K), one),
                  pl.BlockSpec((R2, tn), lambda j, i: (0, j + c0)),
                  pl.BlockSpec((R2, tn), lambda j, i: (0, j + c1)),
                  pl.BlockSpec((None, K, tn), lambda j, i: (layer, 0, j)),
                  pl.BlockSpec((None, K, tn), lambda j, i: (layer, 0, j))],
        out_specs=[pl.BlockSpec((tm, tn), lambda j, i: (i, j)),
                   pl.BlockSpec((R2, tn), lambda j, i: (0, j))],
        out_shape=[jax.ShapeDtypeStruct((M, D), BF), jax.ShapeDtypeStruct((R2, D), BF)],
        scratch_shapes=[pltpu.VMEM((K, tn), BF), pltpu.VMEM((K, tn), BF)],
        compiler_params=_cparams(("arbitrary", "arbitrary")),
    )(o_nsa, o_rnn, z, z, o_nsa2, o_rnn2, z2, z2, w_nsa, w_rnn)


def _ln_kernel(x_ref, g_ref, sc_ref, sh_ref, o_ref):
    x = x_ref[...]
    ms = jnp.mean(x * x, axis=-1, keepdims=True)
    y = x * lax.rsqrt(ms + EPS) * g_ref[...]
    o_ref[...] = (y * (1.0 + sc_ref[...]) + sh_ref[...]).astype(o_ref.dtype)


def _ln_mod(x, g, sc, sh, *, tt):
    B, T, D = x.shape
    R = sc.shape[1]
    rr = tt if R == T else 1
    mod_map = (lambda b, i: (b, i, 0)) if R == T else (lambda b, i: (b, 0, 0))
    return pl.pallas_call(
        _ln_kernel,
        grid=(B, T // tt),
        in_specs=[pl.BlockSpec((None, tt, D), lambda b, i: (b, i, 0)),
                  pl.BlockSpec((1, D), lambda b, i: (0, 0)),
                  pl.BlockSpec((None, rr, D), mod_map),
                  pl.BlockSpec((None, rr, D), mod_map)],
        out_specs=pl.BlockSpec((None, tt, D), lambda b, i: (b, i, 0)),
        out_shape=jax.ShapeDtypeStruct((B, T, D), BF),
        compiler_params=_cparams(("parallel", "arbitrary")),
    )(x, g, sc, sh)


def _qk_kernel(z_ref, c_ref, s_ref, g_ref, q_ref, kv_ref, win_ref, kvb_ref, *, scale):
    cosf = c_ref[...]
    sinf = s_ref[...]

    def norm_rope(col, gi):
        x = z_ref[:, col:col + HEAD_DIM]
        ms = jnp.mean(x * x, axis=-1, keepdims=True)
        y = x * lax.rsqrt(ms + EPS) * g_ref[gi:gi + 1, :]
        return y * cosf + pltpu.roll(y, HEAD_DIM // 2, 1) * sinf

    nq = N_HEADS * HEAD_DIM
    grp = N_KV * HEAD_DIM
    for h in range(N_HEADS):
        q_ref[:, h * HEAD_DIM:(h + 1) * HEAD_DIM] = (norm_rope(h * HEAD_DIM, 0) * scale).astype(BF)
    for j in range(3):
        for gi in range(N_KV):
            kcol = nq + (2 * j) * grp + gi * HEAD_DIM
            vcol = nq + (2 * j + 1) * grp + gi * HEAD_DIM
            k = norm_rope(kcol, 1 + j)
            v = z_ref[:, vcol:vcol + HEAD_DIM]
            ko = (2 * j) * grp + gi * HEAD_DIM
            vo = (2 * j + 1) * grp + gi * HEAD_DIM
            kvb_ref[:, ko:ko + HEAD_DIM] = k.astype(BF)
            kvb_ref[:, vo:vo + HEAD_DIM] = v.astype(BF)
            if j < 2:
                kv_ref[:, ko:ko + HEAD_DIM] = k
                kv_ref[:, vo:vo + HEAD_DIM] = v
            else:
                win_ref[:, gi * HEAD_DIM:(gi + 1) * HEAD_DIM] = k
                win_ref[:, grp + gi * HEAD_DIM:grp + (gi + 1) * HEAD_DIM] = v


def _qk_prep(z, cosf, sinf, qk_g, *, tt, nz):
    B, T, _ = z.shape
    NZ = nz
    nq = N_HEADS * HEAD_DIM
    grp = N_KV * HEAD_DIM
    row = lambda b, i: (b, i, 0)
    return pl.pallas_call(
        functools.partial(_qk_kernel, scale=HEAD_DIM ** -0.5),
        grid=(B, T // tt),
        in_specs=[pl.BlockSpec((None, tt, NZ), row),
                  pl.BlockSpec((tt, HEAD_DIM), lambda b, i: (i, 0)),
                  pl.BlockSpec((tt, HEAD_DIM), lambda b, i: (i, 0)),
                  pl.BlockSpec((4, HEAD_DIM), lambda b, i: (0, 0))],
        out_specs=[pl.BlockSpec((None, tt, nq), row),
                   pl.BlockSpec((None, tt, 4 * grp), row),
                   pl.BlockSpec((None, tt, 2 * grp), row),
                   pl.BlockSpec((None, tt, 6 * grp), row)],
        out_shape=[jax.ShapeDtypeStruct((B, T, nq), BF),
                   jax.ShapeDtypeStruct((B, T, 4 * grp), F32),
                   jax.ShapeDtypeStruct((B, T, 2 * grp), F32),
                   jax.ShapeDtypeStruct((B, T, 6 * grp), BF)],
        compiler_params=_cparams(("parallel", "arbitrary")),
    )(z, cosf, sinf, qk_g)


def _rope_tables(pos0, T):
    half = HEAD_DIM // 2
    inv = ROPE_THETA ** (-jnp.arange(half, dtype=F32) / half)
    ang = (pos0 + jnp.arange(T, dtype=F32))[:, None] * inv[None, :]
    cos, sin = jnp.cos(ang), jnp.sin(ang)
    return jnp.concatenate([cos, cos], axis=1), jnp.concatenate([-sin, sin], axis=1)


def _compress_acc(load_pair, pe_ref, w_ref, rows):
    acc = jnp.zeros((rows, 2 * HEAD_DIM), F32)
    for lp in range(CMP_BLOCK // 2):
        xs = []
        for l in (2 * lp, 2 * lp + 1):
            x = load_pair(l) + pe_ref[l][None]
            xs.append(x.reshape(rows, HEAD_DIM).astype(BF))
        acc = acc + _dot(jnp.concatenate(xs, axis=1), w_ref[lp])
    is_k = (lax.broadcasted_iota(jnp.int32, (rows, HEAD_DIM), 0) % 8) < N_KV
    return jnp.where(is_k, acc[:, :HEAD_DIM], acc[:, HEAD_DIM:])


def _compress_kernel(x_ref, pe_ref, w_ref, o_ref, *, nblk):
    rows = nblk * 8
    out = _compress_acc(lambda l: x_ref[pl.ds(l, nblk, stride=CMP_BLOCK), :, :], pe_ref, w_ref, rows)
    o_ref[...] = out.reshape(nblk, 8, HEAD_DIM)


def _compress(kvrows, pe_t, w2):
    B, T, _ = kvrows.shape
    nblk = T // CMP_BLOCK
    x = kvrows.reshape(B, T, 16, HEAD_DIM)
    return pl.pallas_call(
        functools.partial(_compress_kernel, nblk=nblk),
        grid=(B,),
        in_specs=[pl.BlockSpec((None, T, 8, HEAD_DIM), lambda b: (b, 0, 0, 0)),
                  pl.BlockSpec((CMP_BLOCK, 8, HEAD_DIM), lambda b: (0, 0, 0)),
                  pl.BlockSpec((CMP_BLOCK // 2, 2 * HEAD_DIM, 2 * HEAD_DIM), lambda b: (0, 0, 0))],
        out_specs=pl.BlockSpec((None, nblk, 8, HEAD_DIM), lambda b: (b, 0, 0, 0)),
        out_shape=jax.ShapeDtypeStruct((B, nblk, 8, HEAD_DIM), F32),
        compiler_params=_cparams(("parallel",)),
    )(x, pe_t, w2)


def _compress_paged_kernel(pt_ref, *refs, pb):
    page_refs = refs[:pb]
    pe_ref, w_ref, o_ref = refs[pb:pb + 3]
    rows = pb * 2 * 8

    def load(l):
        tiles = []
        for p in range(pb):
            for h in range(2):
                tiles.append(page_refs[p][h * CMP_BLOCK + l])
        return jnp.stack(tiles, axis=0)

    out = _compress_acc(load, pe_ref, w_ref, rows)
    o_ref[...] = out.reshape(pb * 2, 8, HEAD_DIM)


def _compress_paged(cache5, page_table, layer, pe_t, w2, *, pb):
    n_phys, page = cache5.shape[1], cache5.shape[2]
    B, n_pages = page_table.shape
    assert page == 2 * CMP_BLOCK and n_pages % pb == 0

    def page_spec(p):
        return pl.BlockSpec((None, None, page, 8, HEAD_DIM),
                            lambda b, i, pt: (layer, pt[b, i * pb + p], 0, 0, 0))

    grid_spec = pltpu.PrefetchScalarGridSpec(
        num_scalar_prefetch=1,
        grid=(B, n_pages // pb),
        in_specs=[page_spec(p) for p in range(pb)] + [
            pl.BlockSpec((CMP_BLOCK, 8, HEAD_DIM), lambda b, i, pt: (0, 0, 0)),
            pl.BlockSpec((CMP_BLOCK // 2, 2 * HEAD_DIM, 2 * HEAD_DIM), lambda b, i, pt: (0, 0, 0))],
        out_specs=pl.BlockSpec((None, pb * 2, 8, HEAD_DIM), lambda b, i, pt: (b, i, 0, 0)),
    )
    return pl.pallas_call(
        functools.partial(_compress_paged_kernel, pb=pb),
        grid_spec=grid_spec,
        out_shape=jax.ShapeDtypeStruct((B, n_pages * 2, 8, HEAD_DIM), F32),
        compiler_params=_cparams(("parallel", "arbitrary")),
    )(page_table, *([cache5] * pb), pe_t, w2)


def _online_step(s2, v, m_ref, l_ref, acc_ref):
    tk = s2.shape[1]
    m_prev = m_ref[...]
    m_next = jnp.maximum(m_prev, jnp.max(s2, axis=1, keepdims=True))
    alpha = jnp.exp(m_prev - m_next)
    p = jnp.exp(s2 - _rep_lanes(m_next, tk))
    l_ref[...] = alpha * l_ref[...] + jnp.sum(p, axis=1, keepdims=True)
    m_ref[...] = m_next
    acc_ref[...] = alpha * acc_ref[...] + _dot(p.astype(BF), v)


def _nsa_kernel(q_ref, kc_ref, vc_ref, ks_ref, vs_ref, kw_ref, vw_ref, g_ref, o_ref,
                m_ref, l_ref, acc_ref, *, tq, nblk):
    qi = pl.program_id(2)
    q0 = qi * tq
    rows = HPG * tq
    q = q_ref[...]
    qcat = jnp.concatenate([q[:, h * HEAD_DIM:(h + 1) * HEAD_DIM] for h in range(HPG)], axis=0)

    kc = kc_ref[...].astype(BF)
    vc = vc_ref[...].astype(BF)
    t_b = q0 + lax.broadcasted_iota(jnp.int32, (nblk, tq), 1)
    blk = lax.broadcasted_iota(jnp.int32, (nblk, tq), 0)
    complete = jnp.concatenate([(blk + 1) * CMP_BLOCK - 1 <= t_b] * HPG, axis=1)
    s_t = jnp.where(complete, _dot_nt(kc, qcat), NEG)
    mx = jnp.max(s_t, axis=0, keepdims=True)
    e = jnp.where(complete, jnp.exp(s_t - mx), 0.0)
    d = jnp.sum(e, axis=0, keepdims=True)
    p_t = e / jnp.where(d > 0, d, 1.0)

    imp = p_t[:, 0:tq]
    for h in range(1, HPG):
        imp = imp + p_t[:, h * tq:(h + 1) * tq]
    cur = jnp.right_shift(t_b, CMP_SHIFT)
    forced = (blk == 0) | (blk == cur) | (blk == cur - 1)
    score = jnp.where(forced, FORCED_SCORE, imp)
    score = jnp.where(blk <= cur, score, -1.0)
    rank = jnp.zeros((nblk, tq), F32)
    for i in range(nblk):
        si = score[i:i + 1, :]
        beats = (si > score) | ((si == score) & (blk > i))
        rank = rank + jnp.where(beats, 1.0, 0.0)
    sel_t = jnp.where((rank < min(N_SELECT, nblk)) & (score > -0.5), 1.0, 0.0).astype(BF)

    eye = jnp.where(lax.broadcasted_iota(jnp.int32, (tq, tq), 0)
                    == lax.broadcasted_iota(jnp.int32, (tq, tq), 1), 1.0, 0.0).astype(BF)
    p_bf = p_t.astype(BF)
    p_rows = jnp.concatenate([_dot_nt(eye, p_bf[:, h * tq:(h + 1) * tq]) for h in range(HPG)], axis=0)
    o_cmp = _dot(p_rows.astype(BF), vc)
    sel = _dot_nt(eye, sel_t).astype(BF)

    t_k = q0 + lax.broadcasted_iota(jnp.int32, (tq, tq), 0)
    k_off = lax.broadcasted_iota(jnp.int32, (tq, tq), 1)
    e_blk = lax.broadcasted_iota(jnp.int32, (nblk, tq), 0)
    e_off = lax.broadcasted_iota(jnp.int32, (nblk, tq), 1)

    def reset():
        m_ref[...] = jnp.full((rows, LANES), NEG, F32)
        l_ref[...] = jnp.zeros((rows, LANES), F32)
        acc_ref[...] = jnp.zeros((rows, HEAD_DIM), F32)

    def attend(c, k_ref, v_ref, mask):
        start = pl.multiple_of(c * tq, tq)
        k = k_ref[pl.ds(start, tq), :]
        v = v_ref[pl.ds(start, tq), :]
        s = jnp.where(mask[None], _dot_nt(qcat, k).reshape(HPG, tq, tq), NEG)
        _online_step(s.reshape(rows, tq), v, m_ref, l_ref, acc_ref)

    def sel_body(c, carry):
        expand = jnp.where(jnp.right_shift(c * tq + e_off, CMP_SHIFT) == e_blk, 1.0, 0.0).astype(BF)
        chosen = _dot(sel, expand) > 0.5
        attend(c, ks_ref, vs_ref, chosen & (c * tq + k_off <= t_k))
        return carry

    reset()
    lax.fori_loop(0, qi + 1, sel_body, 0)
    o_sel = acc_ref[...] / l_ref[...]

    def win_body(c, carry):
        dist = t_k - (c * tq + k_off)
        attend(c, kw_ref, vw_ref, (dist >= 0) & (dist < WINDOW))
        return carry

    reset()
    lax.fori_loop(jnp.maximum(qi - WINDOW // tq, 0), qi + 1, win_body, 0)
    o_win = acc_ref[...] / l_ref[...]

    g = jax.nn.sigmoid(g_ref[...])
    for h in range(HPG):
        r = slice(h * tq, (h + 1) * tq)
        o = (g[:, 3 * h:3 * h + 1] * o_cmp[r] + g[:, 3 * h + 1:3 * h + 2] * o_sel[r]
             + g[:, 3 * h + 2:3 * h + 3] * o_win[r])
        o_ref[:, h * HEAD_DIM:(h + 1) * HEAD_DIM] = o.astype(BF)


def _nsa_prompt(q, kcv, kvb, zg, *, tq, gate_col):
    B, T, _ = q.shape
    nblk = kcv.shape[1]
    col = lambda c: pl.BlockSpec((None, T, HEAD_DIM), lambda b, g, i, c=c: (b, 0, c + g))
    rows = HPG * tq
    return pl.pallas_call(
        functools.partial(_nsa_kernel, tq=tq, nblk=nblk),
        grid=(B, N_KV, T // tq),
        in_specs=[pl.BlockSpec((None, tq, HPG * HEAD_DIM), lambda b, g, i: (b, i, g)),
                  pl.BlockSpec((None, nblk, HEAD_DIM), lambda b, g, i: (b, 0, g)),
                  pl.BlockSpec((None, nblk, HEAD_DIM), lambda b, g, i: (b, 0, N_KV + g)),
                  col(2 * N_KV), col(3 * N_KV), col(4 * N_KV), col(5 * N_KV),
                  pl.BlockSpec((None, tq, LANES), lambda b, g, i: (b, i, gate_col + g))],
        out_specs=pl.BlockSpec((None, tq, HPG * HEAD_DIM), lambda b, g, i: (b, i, g)),
        out_shape=jax.ShapeDtypeStruct((B, T, N_HEADS * HEAD_DIM), BF),
        scratch_shapes=[pltpu.VMEM((rows, LANES), F32), pltpu.VMEM((rows, LANES), F32),
                        pltpu.VMEM((rows, HEAD_DIM), F32)],
        compiler_params=_cparams(("parallel", "parallel", "arbitrary")),
    )(q, kcv, kcv, kvb, kvb, kvb, kvb, zg)


def _softplus(y):
    return jnp.maximum(y, 0.0) + jnp.log1p(jnp.exp(-jnp.abs(y)))


def _rg_gates(u, w_ref, brg_ref, lam_ref):
    ub = u.astype(BF)
    r = jax.nn.sigmoid(_dot(ub, w_ref[0].astype(BF)) + brg_ref[0:1, :])
    i = jax.nn.sigmoid(_dot(ub, w_ref[1].astype(BF)) + brg_ref[1:2, :])
    log_a = -RG_C * r * _softplus(-lam_ref[...])
    a = jnp.exp(log_a)
    th = jnp.tanh(log_a)
    b = jnp.sqrt(-2.0 * th / (1.0 - th)) * i * u
    return a, b


def _rglru_kernel(xr_ref, zg_ref, c0_ref, h0_ref, cw_ref, cb_ref, w_ref, brg_ref, lam_ref,
                  y_ref, hl_ref, cl_ref, xp_ref, a_ref, b_ref, h_ref, *, T):
    pad = 8
    xp_ref[0:pad, :] = jnp.zeros((pad, RNN_BW), F32)
    xp_ref[pad - (CONV_W - 1):pad, :] = c0_ref[...]
    xp_ref[pad:pad + T, :] = xr_ref[...]
    u = cb_ref[...] + xp_ref[pad:pad + T, :] * cw_ref[CONV_W - 1:CONV_W, :]
    for k in range(CONV_W - 1):
        off = pad - (CONV_W - 1) + k
        u = u + xp_ref[off:off + T, :] * cw_ref[k:k + 1, :]
    cl_ref[...] = xp_ref[pad + T - (CONV_W - 1):pad + T, :]

    a, b = _rg_gates(u, w_ref, brg_ref, lam_ref)
    a_ref[0:T, :] = a
    b_ref[0:T, :] = b

    levels = []
    n, off = T, 0
    while True:
        levels.append((n, off))
        if n <= 8:
            break
        off += n
        n //= 8
    for (n, off), (n2, off2) in zip(levels[:-1], levels[1:]):
        A = a_ref[pl.ds(off, n2, stride=8), :]
        Bv = b_ref[pl.ds(off, n2, stride=8), :]
        for r in range(1, 8):
            ar = a_ref[pl.ds(off + r, n2, stride=8), :]
            br = b_ref[pl.ds(off + r, n2, stride=8), :]
            Bv = ar * Bv + br
            A = ar * A
        a_ref[off2:off2 + n2, :] = A
        b_ref[off2:off2 + n2, :] = Bv
    n, off = levels[-1]
    h = h0_ref[...]
    for t in range(n):
        h = a_ref[off + t:off + t + 1, :] * h + b_ref[off + t:off + t + 1, :]
        h_ref[off + 8 + t:off + 9 + t, :] = h
    for (n, off), (n2, off2) in reversed(list(zip(levels[:-1], levels[1:]))):
        h_ref[off2 + 7:off2 + 8, :] = h0_ref[...]
        hp = h_ref[off2 + 7:off2 + 7 + n2, :]
        for r in range(8):
            ar = a_ref[pl.ds(off + r, n2, stride=8), :]
            br = b_ref[pl.ds(off + r, n2, stride=8), :]
            hp = ar * hp + br
            h_ref[pl.ds(off + 8 + r, n2, stride=8), :] = hp
    hfull = h_ref[8:8 + T, :]
    y_ref[...] = (jax.nn.gelu(zg_ref[...]) * hfull).astype(BF)
    hl_ref[...] = h_ref[8 + T - 1:8 + T, :]


def _rglru(zr, conv0, h0, conv_w, conv_b, w_rg, b_rg, lam, *, d_rnn, col0):
    B, T, _ = zr.shape
    nb = d_rnn // RNN_BW
    assert T % 8 == 0
    total = T + T // 4 + 64
    return pl.pallas_call(
        functools.partial(_rglru_kernel, T=T),
        grid=(B, nb),
        in_specs=[pl.BlockSpec((None, T, RNN_BW), lambda b, n: (b, 0, col0 + n)),
                  pl.BlockSpec((None, T, RNN_BW), lambda b, n: (b, 0, col0 + nb + n)),
                  pl.BlockSpec((None, CONV_W - 1, RNN_BW), lambda b, n: (b, 0, n)),
                  pl.BlockSpec((None, 1, RNN_BW), lambda b, n: (b, 0, n)),
                  pl.BlockSpec((CONV_W, RNN_BW), lambda b, n: (0, n)),
                  pl.BlockSpec((1, RNN_BW), lambda b, n: (0, n)),
                  pl.BlockSpec((2, None, RNN_BW, RNN_BW), lambda b, n: (0, n, 0, 0)),
                  pl.BlockSpec((2, RNN_BW), lambda b, n: (0, n)),
                  pl.BlockSpec((1, RNN_BW), lambda b, n: (0, n))],
        out_specs=[pl.BlockSpec((None, T, RNN_BW), lambda b, n: (b, 0, n)),
                   pl.BlockSpec((None, 1, RNN_BW), lambda b, n: (b, 0, n)),
                   pl.BlockSpec((None, CONV_W - 1, RNN_BW), lambda b, n: (b, 0, n))],
        out_shape=[jax.ShapeDtypeStruct((B, T, d_rnn), BF),
                   jax.ShapeDtypeStruct((B, 1, d_rnn), F32),
                   jax.ShapeDtypeStruct((B, CONV_W - 1, d_rnn), F32)],
        scratch_shapes=[pltpu.VMEM((T + 8, RNN_BW), F32), pltpu.VMEM((total, RNN_BW), F32),
                        pltpu.VMEM((total, RNN_BW), F32), pltpu.VMEM((total + 8, RNN_BW), F32)],
        compiler_params=_cparams(("parallel", "arbitrary")),
    )(zr, zr, conv0, h0, conv_w, conv_b[None], w_rg, b_rg, lam[None])


def _rglru_step_kernel(xr_ref, zg_ref, c0_ref, h0_ref, cw_ref, cb_ref, w_ref, brg_ref, lam_ref,
                       y_ref, hl_ref):
    u = cb_ref[...] + xr_ref[...] * cw_ref[CONV_W - 1:CONV_W, :]
    for k in range(CONV_W - 1):
        u = u + c0_ref[k] * cw_ref[k:k + 1, :]
    a, b = _rg_gates(u, w_ref, brg_ref, lam_ref)
    h = a * h0_ref[...] + b
    hl_ref[...] = h
    y_ref[...] = (jax.nn.gelu(zg_ref[...]) * h).astype(BF)


def _rglru_step(zr, conv0_t, h0, conv_w, conv_b, w_rg, b_rg, lam, *, d_rnn, col0):
    R = zr.shape[0]
    nb = d_rnn // RNN_BW
    return pl.pallas_call(
        _rglru_step_kernel,
        grid=(nb,),
        in_specs=[pl.BlockSpec((R, RNN_BW), lambda n: (0, col0 + n)),
                  pl.BlockSpec((R, RNN_BW), lambda n: (0, col0 + nb + n)),
                  pl.BlockSpec((CONV_W - 1, R, RNN_BW), lambda n: (0, 0, n)),
                  pl.BlockSpec((R, RNN_BW), lambda n: (0, n)),
                  pl.BlockSpec((CONV_W, RNN_BW), lambda n: (0, n)),
                  pl.BlockSpec((1, RNN_BW), lambda n: (0, n)),
                  pl.BlockSpec((2, None, RNN_BW, RNN_BW), lambda n: (0, n, 0, 0)),
                  pl.BlockSpec((2, RNN_BW), lambda n: (0, n)),
                  pl.BlockSpec((1, RNN_BW), lambda n: (0, n))],
        out_specs=[pl.BlockSpec((R, RNN_BW), lambda n: (0, n)),
                   pl.BlockSpec((R, RNN_BW), lambda n: (0, n))],
        out_shape=[jax.ShapeDtypeStruct((R, d_rnn), BF), jax.ShapeDtypeStruct((R, d_rnn), F32)],
        compiler_params=_cparams(("arbitrary",)),
    )(zr, zr, conv0_t, h0, conv_w, conv_b[None], w_rg, b_rg, lam[None])


def _dec_cmp_kernel(q_ref, kc_ref, vc_ref, o_ref, idx_ref, *, n_past, t_pos):
    q = q_ref[...]
    nlan = idx_ref.shape[-1] * ((n_past + 1 + LANES - 1) // LANES)
    blk = lax.broadcasted_iota(jnp.int32, (1, n_past), 1)
    complete = (blk + 1) * CMP_BLOCK - 1 <= t_pos
    s = jnp.where(complete, _dot_nt(q, kc_ref[...].astype(BF)), NEG)
    mx = jnp.max(s, axis=-1, keepdims=True)
    e = jnp.where(complete, jnp.exp(s - mx), 0.0)
    d = jnp.sum(e, axis=-1, keepdims=True)
    p = e / jnp.where(d > 0, d, 1.0)
    o_ref[...] = _dot(p.astype(BF), vc_ref[...].astype(BF))
    head = lax.broadcasted_iota(jnp.int32, p.shape, 0) < HPG
    imp = jnp.sum(jnp.where(head, p, 0.0), axis=0, keepdims=True)

    cur = t_pos // CMP_BLOCK
    imp_all = jnp.concatenate([imp, jnp.zeros((1, nlan - n_past), F32)], axis=1)
    blk_all = lax.broadcasted_iota(jnp.int32, (1, nlan), 1)
    forced = (blk_all == 0) | (blk_all == cur) | (blk_all == cur - 1)
    score = jnp.where(forced, FORCED_SCORE, imp_all)
    score = jnp.where(blk_all <= cur, score, -1.0)
    score = jnp.where(blk_all <= n_past, score, -2.0)
    blk_f = blk_all.astype(F32)
    slot = lax.broadcasted_iota(jnp.int32, (1, LANES), 1)
    res = jnp.full((1, LANES), float(n_past), F32)
    for k in range(min(N_SELECT, n_past + 1)):
        top = jnp.max(score, axis=-1, keepdims=True)
        arg = jnp.min(jnp.where(score == top, blk_f, float(nlan)), axis=-1, keepdims=True)
        res = jnp.where(slot == k, jnp.where(top > -0.5, arg, float(n_past)), res)
        score = jnp.where(blk_f == arg, -3.0, score)
    idx_ref[...] = res.astype(jnp.int32)


def _dec_cmp(q16, kcv, *, t_pos):
    B = q16.shape[0]
    n_past = kcv.shape[1]
    return pl.pallas_call(
        functools.partial(_dec_cmp_kernel, n_past=n_past, t_pos=t_pos),
        grid=(B, N_KV),
        in_specs=[pl.BlockSpec((None, None, 16, HEAD_DIM), lambda b, g: (b, g, 0, 0)),
                  pl.BlockSpec((None, n_past, HEAD_DIM), lambda b, g: (b, 0, g)),
                  pl.BlockSpec((None, n_past, HEAD_DIM), lambda b, g: (b, 0, N_KV + g))],
        out_specs=[pl.BlockSpec((None, None, 16, HEAD_DIM), lambda b, g: (b, g, 0, 0)),
                   pl.BlockSpec((None, None, 1, LANES), lambda b, g: (b, g, 0, 0))],
        out_shape=[jax.ShapeDtypeStruct((B, N_KV, 16, HEAD_DIM), F32),
                   jax.ShapeDtypeStruct((B, N_KV, 1, LANES), jnp.int32)],
        compiler_params=_cparams(("parallel", "arbitrary")),
    )(q16, kcv, kcv)


def _dec_attn_kernel(pt_ref, ti_ref, q_ref, c0_ref, c1_ref, c2_ref, c3_ref, new_ref, win_ref, oc_ref, g_ref,
                     o_ref, m_ref, l_ref, acc_ref, *, n_past):
    b, s = pl.program_id(0), pl.program_id(1)
    cache_refs = (c0_ref, c1_ref, c2_ref, c3_ref)
    planes = 4 * N_KV

    @pl.when(s == 0)
    def _():
        m_ref[...] = jnp.full(m_ref.shape, NEG, F32)
        l_ref[...] = jnp.zeros(l_ref.shape, F32)
        acc_ref[...] = jnp.zeros(acc_ref.shape, F32)

    for g in range(N_KV):
        live = ti_ref[b * N_KV + g, s] < n_past
        k = cache_refs[g][pl.ds(2 * N_KV + g, CMP_BLOCK, stride=planes), :].astype(BF)
        v = cache_refs[g][pl.ds(3 * N_KV + g, CMP_BLOCK, stride=planes), :].astype(BF)
        sc = jnp.where(live, _dot_nt(q_ref[g], k), NEG)
        m_prev = m_ref[g]
        m_next = jnp.maximum(m_prev, jnp.max(sc, axis=1, keepdims=True))
        alpha = jnp.exp(m_prev - m_next)
        p = jnp.where(live, jnp.exp(sc - m_next[:, :CMP_BLOCK]), 0.0)
        l_ref[g] = alpha * l_ref[g] + jnp.sum(p, axis=1, keepdims=True)
        m_ref[g] = m_next
        acc_ref[g] = alpha * acc_ref[g] + _dot(p.astype(BF), v)

    @pl.when(s == pl.num_programs(1) - 1)
    def _():
        n_win = win_ref.shape[0] // (2 * N_KV)
        keep = lax.broadcasted_iota(jnp.int32, (1, n_win), 1) >= n_win + 1 - WINDOW
        for g in range(N_KV):
            q = q_ref[g]
            qf = q.astype(F32)

            def self_score(k_row):
                return jnp.sum(qf * k_row.astype(BF).astype(F32), axis=1, keepdims=True)

            s_self = self_score(new_ref[g, 2:3, :])
            v_self = new_ref[g, 3:4, :].astype(BF).astype(F32)
            m_prev = m_ref[g]
            m_fin = jnp.maximum(m_prev, s_self)
            alpha = jnp.exp(m_prev - m_fin)
            p_self = jnp.exp(s_self - m_fin)
            o_sel = (alpha * acc_ref[g] + p_self * v_self) / (alpha * l_ref[g] + p_self)

            kw = win_ref[pl.ds(g, n_win, stride=2 * N_KV), :].astype(BF)
            vw = win_ref[pl.ds(N_KV + g, n_win, stride=2 * N_KV), :].astype(BF)
            sw = jnp.where(keep, _dot_nt(q, kw), NEG)
            sw_self = self_score(new_ref[g, 4:5, :])
            mw = jnp.maximum(jnp.max(sw, axis=1, keepdims=True), sw_self)
            pw = jnp.where(keep, jnp.exp(sw - mw), 0.0)
            pw_self = jnp.exp(sw_self - mw)
            num = _dot(pw.astype(BF), vw) + pw_self * new_ref[g, 5:6, :].astype(BF).astype(F32)
            o_win = num / (jnp.sum(pw, axis=1, keepdims=True) + pw_self)

            gt = jax.nn.sigmoid(g_ref[g])
            o_ref[g] = gt[:, 0:1] * oc_ref[g] + gt[:, 1:2] * o_sel + gt[:, 2:3] * o_win


def _dec_attn(q16, cache4, page_table, top_i, newkv, win3, o_cmp, g16, *, layer, n_past):
    B = q16.shape[0]
    n_slots = top_i.shape[1]
    half_rows = CMP_BLOCK * 4 * N_KV

    def half_page(g):
        def index(b, s, pt, ti):
            blk = jnp.minimum(ti[b * N_KV + g, s], n_past - 1)
            return (layer, pt[b, blk // 2], blk % 2, 0)
        return pl.BlockSpec((None, None, half_rows, HEAD_DIM), index)

    per_seq = lambda b, s, pt, ti: (b, 0, 0, 0)
    grid_spec = pltpu.PrefetchScalarGridSpec(
        num_scalar_prefetch=2,
        grid=(B, n_slots),
        in_specs=[pl.BlockSpec((None, N_KV, 16, HEAD_DIM), per_seq)]
        + [half_page(g) for g in range(N_KV)]
        + [pl.BlockSpec((None, N_KV, 6, HEAD_DIM), per_seq),
           pl.BlockSpec((None, win3.shape[1], HEAD_DIM), lambda b, s, pt, ti: (layer * B + b, 0, 0)),
           pl.BlockSpec((None, N_KV, 16, HEAD_DIM), per_seq),
           pl.BlockSpec((None, N_KV, 16, LANES), per_seq)],
        out_specs=pl.BlockSpec((None, N_KV, 16, HEAD_DIM), per_seq),
        scratch_shapes=[pltpu.VMEM((N_KV, 16, LANES), F32), pltpu.VMEM((N_KV, 16, LANES), F32),
                        pltpu.VMEM((N_KV, 16, HEAD_DIM), F32)],
    )
    return pl.pallas_call(
        functools.partial(_dec_attn_kernel, n_past=n_past),
        grid_spec=grid_spec,
        out_shape=jax.ShapeDtypeStruct((B, N_KV, 16, HEAD_DIM), F32),
        compiler_params=_cparams(("parallel", "arbitrary")),
    )(page_table, top_i, q16, *([cache4] * N_KV), newkv, win3, o_cmp, g16)


def _regroup_w_in(w_in):
    L, D, _ = w_in.shape
    o_gn = N_HEADS * HEAD_DIM + 6 * N_KV * HEAD_DIM
    o_rx = o_gn + 3 * N_HEADS
    wg = w_in[:, :, o_gn:o_rx].reshape(L, D, N_KV, 3 * HPG)
    wg = jnp.pad(wg, ((0, 0), (0, 0), (0, 0), (0, LANES - 3 * HPG))).reshape(L, D, N_KV * LANES)
    return jnp.concatenate([w_in[:, :, :o_gn], wg, w_in[:, :, o_rx:]], axis=2).astype(BF)


def _row_tile(m):
    return 1024 if m % 1024 == 0 else 256


def _largest_divisor(n, cap):
    return max(d for d in range(1, cap + 1) if n % d == 0)


def _layer(l, xp, xs, mod_p, mod_s, P, rope_p, rope_s, cache_kv, page_table, state_win, state_rnn, state_conv):
    B, T, D = xp.shape
    M = B * T
    R = xs.shape[0]
    nb, n_pages = page_table.shape
    depth, n_phys, page = cache_kv.shape[:3]
    n_past = n_pages * page // CMP_BLOCK
    d_attn = N_HEADS * HEAD_DIM
    d_rnn = P["conv_w"].shape[2]
    o_gn = d_attn + 6 * N_KV * HEAD_DIM
    o_rx = o_gn + 3 * N_HEADS
    tm, tn = _row_tile(M), 512
    shp1, scp1, gap1, shp2, scp2, gap2 = mod_p
    shs1, scs1, gas1, shs2, scs2, gas2 = mod_s
    bmap = lambda j, i: (i // (T // tm), 0, j)
    g_ln = P["g_ln"][l]

    c_g = o_gn
    c_r = c_g + N_KV * LANES
    c_mg = c_r + 2 * d_rnn
    w2 = jnp.concatenate([P["w_phi"][l, 0], P["w_phi"][l, 1]], axis=-1)
    w2 = w2.reshape(CMP_BLOCK // 2, 2 * HEAD_DIM, 2 * HEAD_DIM).astype(BF)
    pe_t = jnp.concatenate([jnp.broadcast_to(P["pe_cmp"][l, j][:, None, :], (CMP_BLOCK, N_KV, HEAD_DIM))
                            for j in range(2)], axis=1)
    rnn = (P["conv_w"][l], P["conv_b"][l], P["w_rg"][l], P["b_rg"][l], P["lam"][l])

    hp = _ln_mod(xp, g_ln[0:1], scp1, shp1, tt=256).reshape(M, D)
    hs = _ln_mod(xs[None], g_ln[0:1], scs1, shs1, tt=R).reshape(R, D)
    zp, zs = _mm2(hp, hs, P["w_in_bf"], layer=l, tm=tm, tn=tn)
    zp = zp.reshape(B, T, -1)

    q, kvrows, winrows, kvb = _qk_prep(zp, rope_p[0], rope_p[1], P["qk_g"][l], tt=256, nz=o_gn)
    kcv = _compress(kvrows, pe_t, w2)
    o_nsa_p = _nsa_prompt(q, kcv.reshape(B, T // CMP_BLOCK, 8 * HEAD_DIM), kvb, zp, tq=256,
                          gate_col=c_g // LANES)
    conv0 = jnp.zeros((B, CONV_W - 1, d_rnn), F32)
    h0 = jnp.zeros((B, 1, d_rnn), F32)
    y_rnn_p, h_last, conv_last = _rglru(zp, conv0, h0, *rnn, d_rnn=d_rnn, col0=c_r // RNN_BW)

    qs, kvrows_s, winrows_s, _ = _qk_prep(zs[None], rope_s[0], rope_s[1], P["qk_g"][l], tt=R, nz=o_gn)
    q16 = jnp.pad(qs[0, :nb].reshape(nb, N_KV, HPG, HEAD_DIM), ((0, 0), (0, 0), (0, 16 - HPG), (0, 0)))
    g16 = jnp.pad(zs[:nb, c_g:c_r].reshape(nb, N_KV, LANES)[:, :, :3 * HPG].reshape(nb, N_KV, HPG, 3),
                  ((0, 0), (0, 0), (0, 16 - HPG), (0, LANES - 3)))
    new6 = jnp.concatenate([kvrows_s[0, :nb], winrows_s[0, :nb]], axis=-1)
    new6 = new6.reshape(nb, 6, N_KV, HEAD_DIM).transpose(0, 2, 1, 3)
    cache5 = cache_kv.reshape(depth, n_phys, page, 4 * N_KV, HEAD_DIM)
    kcv_s = _compress_paged(cache5, page_table, l, pe_t, w2, pb=_largest_divisor(n_pages, 32))
    o_cmp, top = _dec_cmp(q16, kcv_s.reshape(nb, n_past, 8 * HEAD_DIM), t_pos=n_pages * page)
    top_i = top[:, :, 0, :N_SELECT].reshape(nb * N_KV, N_SELECT)
    cache4 = cache_kv.reshape(depth, n_phys, page * 4 * N_KV, HEAD_DIM)
    n_win = state_win.shape[2]
    win3 = state_win.reshape(depth * nb, n_win * 2 * N_KV, HEAD_DIM)
    o16 = _dec_attn(q16, cache4, page_table, top_i, new6, win3, o_cmp, g16, layer=l, n_past=n_past)
    o_nsa_s = jnp.pad(o16[:, :, :HPG].reshape(nb, -1), ((0, R - nb), (0, 0))).astype(BF)
    pad_r = lambda a: jnp.pad(a, ((0, R - nb),) + ((0, 0),) * (a.ndim - 1))
    conv0_t = pad_r(state_conv[l]).transpose(1, 0, 2)
    y_rnn_s, h_new = _rglru_step(zs, conv0_t, pad_r(state_rnn[l]), *rnn, d_rnn=d_rnn, col0=c_r // RNN_BW)

    mg_p, mg_s = _merge2(o_nsa_p.reshape(M, -1), y_rnn_p.reshape(M, -1), zp.reshape(M, -1), o_nsa_s, y_rnn_s, zs,
                         P["w_branch_nsa"], P["w_branch_rnn"], layer=l, col0=c_mg, tm=tm, tn=tn)
    x1p, x1s = _mm2(mg_p, mg_s, P["w_out"], layer=l, tm=tm, tn=tn, epi="resid",
                    res=(xp.reshape(M, D), gap1, bmap, xs, gas1))
    h2p = _ln_mod(x1p.reshape(B, T, D), g_ln[1:2], scp2, shp2, tt=256).reshape(M, D)
    h2s = _ln_mod(x1s[None], g_ln[1:2], scs2, shs2, tt=R).reshape(R, D)
    up, us = _mm2(h2p, h2s, P["w_mlp1"], layer=l, tm=tm, tn=tn, out_dtype=BF, epi="relu2")
    tk2 = min(4096, up.shape[1])
    x2p = _mm(up, P["w_mlp2_bf"], layer=l, tm=tm, tn=tn, tk=tk2, epi="resid", x_res=x1p, ga=gap2,
              ga_map=lambda i, j, k: (i // (T // tm), 0, j))
    x2s = _mm(us, P["w_mlp2_bf"], layer=l, tm=R, tn=1024, tk=tk2, epi="resid", x_res=x1s, ga=gas2,
              ga_map=lambda i, j, k: (0, 0, j))

    n_keep = min(WINDOW, T)
    outs_p = (kvrows.reshape(B, T, 4, N_KV, HEAD_DIM),
              winrows[:, T - n_keep:].reshape(B, n_keep, 2, N_KV, HEAD_DIM),
              h_last.reshape(B, -1), conv_last)
    outs_s = (kvrows_s[0, :nb].reshape(nb, 1, 4, N_KV, HEAD_DIM),
              jnp.concatenate([state_win[l][:, 1:], winrows_s[0, :nb].reshape(nb, 1, 2, N_KV, HEAD_DIM)], axis=1),
              h_new[:nb],
              jnp.concatenate([state_conv[l][:, 1:], zs[:nb, None, c_r:c_r + d_rnn]], axis=1))
    return x2p.reshape(B, T, D), x2s, outs_p, outs_s


def kernel(x_prompt, x_sample, cache_kv, state_win, state_rnn, state_conv, page_table, c_prompt, c_sample,
           w_ada, b_ada, g_ln, w_in, qk_g, w_phi, pe_cmp, conv_w, conv_b, w_rg, b_rg, lam,
           w_branch_nsa, w_branch_rnn, w_out, w_mlp1, w_mlp2):
    depth = w_ada.shape[0]
    B, T, D = x_prompt.shape
    nb = x_sample.shape[0]
    R = 16
    past_len = page_table.shape[1] * cache_kv.shape[2]

    c_all = jnp.zeros((2 * R, D), F32).at[:B].set(c_prompt).at[R:R + nb].set(c_sample)
    rope_p = _rope_tables(0.0, T)
    rope_s = tuple(jnp.broadcast_to(t, (R, HEAD_DIM)) for t in _rope_tables(float(past_len), 1))

    P = dict(g_ln=g_ln, w_in_bf=_regroup_w_in(w_in), qk_g=qk_g, w_phi=w_phi, pe_cmp=pe_cmp, conv_w=conv_w, conv_b=conv_b,
             w_rg=w_rg, b_rg=b_rg, lam=lam, w_branch_nsa=w_branch_nsa, w_branch_rnn=w_branch_rnn,
             w_out=w_out, w_mlp1=w_mlp1, w_mlp2_bf=w_mlp2.astype(BF))
    yp = x_prompt
    ys = jnp.pad(x_sample.reshape(nb, D), ((0, R - nb), (0, 0)))
    outs = [[] for _ in range(8)]
    for l in range(depth):
        mod = _mm(c_all, w_ada, layer=l, tm=2 * R, tn=512, epi="silu_bias", bias=b_ada[l][None])
        mod = mod.reshape(2 * R, 6, D)
        mod_p = [mod[:B, k][:, None, :] for k in range(6)]
        mod_s = [mod[R:, k][None] for k in range(6)]
        yp, ys, (a1, a2, a3, a4), (b1, b2, b3, b4) = _layer(
            l, yp, ys, mod_p, mod_s, P, rope_p, rope_s, cache_kv, page_table, state_win, state_rnn, state_conv)
        for lst, v in zip(outs, (a1, b1, a2, b2, a3, b3, a4, b4)):
            lst.append(v)
    kv_p, kv_s, win_p, win_s, h_p, h_s, cv_p, cv_s = (jnp.stack(v) for v in outs)
    return (yp, ys[:nb].reshape(nb, 1, D), kv_p, kv_s, win_p, win_s, h_p, h_s, cv_p, cv_s)
```

```python
import functools
import math

import jax
import jax.numpy as jnp
from jax import lax
from jax.experimental import pallas as pl
from jax.experimental.pallas import tpu as pltpu

BF = jnp.bfloat16
F32 = jnp.float32

HEAD_DIM = 128
N_KV = 4
HPG = 4
N_HEADS = N_KV * HPG
CMP_BLOCK = 64
CMP_SHIFT = 6
N_SELECT = 16
WINDOW = 512
FORCED_SCORE = 1e4
RNN_BW = 128
CONV_W = 4
RG_C = 8.0
ROPE_THETA = 10000.0
EPS = 1e-6
NEG = -1e30
LANES = 128
VMEM_LIMIT = 56 * 1024 * 1024


def _cparams(sem):
    return pltpu.CompilerParams(dimension_semantics=sem, vmem_limit_bytes=VMEM_LIMIT)


def _dot(a, b):
    return jnp.dot(a, b, preferred_element_type=F32)


def _dot_nt(a, b):
    return lax.dot_general(a, b, (((1,), (1,)), ((), ())), preferred_element_type=F32)


def _rep_lanes(x, n):
    return x if n == LANES else jnp.concatenate([x] * (n // LANES), axis=1)


def _mm_kernel(*refs, nk, epi):
    if epi == "resid":
        a_ref, w_ref, x_ref, ga_ref, o_ref = refs[:5]
        scratch = refs[5:]
    elif epi == "silu_bias":
        a_ref, w_ref, bias_ref, o_ref = refs[:4]
        scratch = refs[4:]
    else:
        a_ref, w_ref, o_ref = refs[:3]
        scratch = refs[3:]

    def finish(acc):
        if epi == "relu2":
            r = jnp.maximum(acc, 0.0)
            acc = r * r
        elif epi == "resid":
            acc = x_ref[...] + ga_ref[...] * acc
        elif epi == "silu_bias":
            acc = acc + bias_ref[...]
        o_ref[...] = acc.astype(o_ref.dtype)

    a = a_ref[...]
    if epi == "silu_bias":
        a = (a * jax.nn.sigmoid(a)).astype(BF)
    part = _dot(a, w_ref[...].astype(BF))
    if nk == 1:
        finish(part)
    else:
        acc_ref = scratch[0]
        k = pl.program_id(2)

        @pl.when(k == 0)
        def _():
            acc_ref[...] = part

        @pl.when(k > 0)
        def _():
            acc_ref[...] += part

        @pl.when(k == nk - 1)
        def _():
            finish(acc_ref[...])


def _mm(a, w, *, tm, tn, tk=None, out_dtype=F32, epi=None, x_res=None, ga=None, ga_map=None, bias=None,
        layer=0):
    M, K = a.shape
    N = w.shape[2]
    tk = K if tk is None else tk
    nk = K // tk
    grid = (M // tm, N // tn, nk)
    in_specs = [pl.BlockSpec((tm, tk), lambda i, j, k: (i, k)),
                pl.BlockSpec((None, tk, tn), lambda i, j, k: (layer, k, j))]
    args = [a, w]
    if epi == "resid":
        in_specs.append(pl.BlockSpec((tm, tn), lambda i, j, k: (i, j)))
        in_specs.append(pl.BlockSpec((None, ga.shape[1], tn), ga_map))
        args += [x_res, ga]
    elif epi == "silu_bias":
        in_specs.append(pl.BlockSpec((1, tn), lambda i, j, k: (0, j)))
        args.append(bias)
    scratch =[pltpu.VMEM((tm, tn), F32)] if nk > 1 else []
    return pl.pallas_call(
        functools.partial(_mm_kernel, nk=nk, epi=epi),
        grid=grid, in_specs=in_specs,
        out_specs=pl.BlockSpec((tm, tn), lambda i, j, k: (i, j)),
        out_shape=jax.ShapeDtypeStruct((M, N), out_dtype),
        scratch_shapes=scratch,
        compiler_params=_cparams(("parallel", "parallel", "arbitrary")),
    )(*args)


def _cached_bf16(w_ref, wbf_ref, first):
    if wbf_ref is None:
        return w_ref[...]

    @pl.when(first)
    def _():
        wbf_ref[...] = w_ref[...].astype(BF)

    return wbf_ref[...]


def _mm2_kernel(*refs, epi, cast_w):
    n_in = 7 if epi == "resid" else 3
    a_ref, a2_ref, w_ref = refs[:3]
    o_ref, o2_ref = refs[n_in:n_in + 2]
    wbf_ref = refs[n_in + 2] if cast_w else None
    first = pl.program_id(1) == 0
    w = _cached_bf16(w_ref, wbf_ref, first)

    def finish(acc, x_ref, ga_ref, out_ref):
        if epi == "relu2":
            r = jnp.maximum(acc, 0.0)
            acc = r * r
        elif epi == "resid":
            acc = x_ref[...] + ga_ref[...] * acc
        out_ref[...] = acc.astype(out_ref.dtype)

    extra = refs[3:7] if epi == "resid" else (None,) * 4
    finish(_dot(a_ref[...], w), extra[0], extra[1], o_ref)

    @pl.when(first)
    def _():
        finish(_dot(a2_ref[...], w), extra[2], extra[3], o2_ref)


def _mm2(a, a2, w, *, layer, tm, tn, out_dtype=F32, epi=None, res=None):
    M, K = a.shape
    R2 = a2.shape[0]
    N = w.shape[2]
    cast_w = w.dtype != BF
    in_specs = [pl.BlockSpec((tm, K), lambda j, i: (i, 0)),
                pl.BlockSpec((R2, K), lambda j, i: (0, 0)),
                pl.BlockSpec((None, K, tn), lambda j, i: (layer, 0, j))]
    args = [a, a2, w]
    if epi == "resid":
        x, ga, ga_map, x2, ga2 = res
        in_specs += [pl.BlockSpec((tm, tn), lambda j, i: (i, j)),
                     pl.BlockSpec((None, ga.shape[1], tn), ga_map),
                     pl.BlockSpec((R2, tn), lambda j, i: (0, j)),
                     pl.BlockSpec((None, R2, tn), lambda j, i: (0, 0, j))]
        args += [x, ga, x2, ga2]
    return pl.pallas_call(
        functools.partial(_mm2_kernel, epi=epi, cast_w=cast_w),
        grid=(N // tn, M // tm), in_specs=in_specs,
        out_specs=[pl.BlockSpec((tm, tn), lambda j, i: (i, j)),
                   pl.BlockSpec((R2, tn), lambda j, i: (0, j))],
        out_shape=[jax.ShapeDtypeStruct((M, N), out_dtype), jax.ShapeDtypeStruct((R2, N), out_dtype)],
        scratch_shapes=[pltpu.VMEM((K, tn), BF)] if cast_w else [],
        compiler_params=_cparams(("arbitrary", "arbitrary")),
    )(*args)


def _merge2_kernel(an_ref, ar_ref, gn_ref, gr_ref, an2_ref, ar2_ref, gn2_ref, gr2_ref, wn_ref, wr_ref,
                   o_ref, o2_ref, wnb_ref, wrb_ref):
    first = pl.program_id(1) == 0
    wn = _cached_bf16(wn_ref, wnb_ref, first)
    wr = _cached_bf16(wr_ref, wrb_ref, first)

    def finish(a_n, a_r, g_n, g_r, out_ref):
        o = jax.nn.sigmoid(g_n[...]) * _dot(a_n[...], wn) + jax.nn.sigmoid(g_r[...]) * _dot(a_r[...], wr)
        out_ref[...] = o.astype(out_ref.dtype)

    finish(an_ref, ar_ref, gn_ref, gr_ref, o_ref)

    @pl.when(first)
    def _():
        finish(an2_ref, ar2_ref, gn2_ref, gr2_ref, o2_ref)


def _merge2(o_nsa, o_rnn, z, o_nsa2, o_rnn2, z2, w_nsa, w_rnn, *, layer, col0, tm, tn):
    M, K = o_nsa.shape
    R2 = o_nsa2.shape[0]
    D = w_nsa.shape[2]
    c0, c1 = col0 // tn, (col0 + D) // tn
    row = lambda j, i: (i, 0)
    one = lambda j, i: (0, 0)
    return pl.pallas_call(
        _merge2_kernel,
        grid=(D // tn, M // tm),
        in_specs=[pl.BlockSpec((tm, K), row), pl.BlockSpec((tm, K), row),
                  pl.BlockSpec((tm, tn), lambda j, i: (i, j + c0)),
                  pl.BlockSpec((tm, tn), lambda j, i: (i, j + c1)),
                  pl.BlockSpec((R2, K), one), pl.BlockSpec((R2, K), one),
                  pl.BlockSpec((R2, tn), lambda j, i: (0, j + c0)),
                  pl.BlockSpec((R2, tn), lambda j, i: (0, j + c1)),
                  pl.BlockSpec((None, K, tn), lambda j, i: (layer, 0, j)),
                  pl.BlockSpec((None, K, tn), lambda j, i: (layer, 0, j))],
        out_specs=[pl.BlockSpec((tm, tn), lambda j, i: (i, j)),
                   pl.BlockSpec((R2, tn), lambda j, i: (0, j))],
        out_shape=[jax.ShapeDtypeStruct((M, D), BF), jax.ShapeDtypeStruct((R2, D), BF)],
        scratch_shapes=[pltpu.VMEM((K, tn), BF), pltpu.VMEM((K, tn), BF)],
        compiler_params=_cparams(("arbitrary", "arbitrary")),
    )(o_nsa, o_rnn, z, z, o_nsa2, o_rnn2, z2, z2, w_nsa, w_rnn)


def _ln_kernel(x_ref, g_ref, sc_ref, sh_ref, o_ref):
    x = x_ref[...]
    ms = jnp.mean(x * x, axis=-1, keepdims=True)
    y = x * lax.rsqrt(ms + EPS) * g_ref[...]
    o_ref[...] = (y * (1.0 + sc_ref[...]) + sh_ref[...]).astype(o_ref.dtype)


def _ln_mod(x, g, sc, sh, *, tt):
    B, T, D = x.shape
    R = sc.shape[1]
    rr = tt if R == T else 1
    mod_map = (lambda b, i: (b, i, 0)) if R == T else (lambda b, i: (b, 0, 0))
    return pl.pallas_call(
        _ln_kernel,
        grid=(B, T // tt),
        in_specs=[pl.BlockSpec((None, tt, D), lambda b, i: (b, i, 0)),
                  pl.BlockSpec((1, D), lambda b, i: (0, 0)),
                  pl.BlockSpec((None, rr, D), mod_map),
                  pl.BlockSpec((None, rr, D), mod_map)],
        out_specs=pl.BlockSpec((None, tt, D), lambda b, i: (b, i, 0)),
        out_shape=jax.ShapeDtypeStruct((B, T, D), BF),
        compiler_params=_cparams(("parallel", "arbitrary")),
    )(x, g, sc, sh)


def _qk_kernel(z_ref, c_ref, s_ref, g_ref, q_ref, kv_ref, win_ref, kvb_ref, *, scale):
    cosf = c_ref[...]
    sinf = s_ref[...]

    def norm_rope(col, gi):
        x = z_ref[:, col:col + HEAD_DIM]
        ms = jnp.mean(x * x, axis=-1, keepdims=True)
        y = x * lax.rsqrt(ms + EPS) * g_ref[gi:gi + 1, :]
        return y * cosf + pltpu.roll(y, HEAD_DIM // 2, 1) * sinf

    nq = N_HEADS * HEAD_DIM
    grp = N_KV * HEAD_DIM
    for h in range(N_HEADS):
        q_ref[:, h * HEAD_DIM:(h + 1) * HEAD_DIM] = (norm_rope(h * HEAD_DIM, 0) * scale).astype(BF)
    for j in range(3):
        for gi in range(N_KV):
            kcol = nq + (2 * j) * grp + gi * HEAD_DIM
            vcol = nq + (2 * j + 1) * grp + gi * HEAD_DIM
            k = norm_rope(kcol, 1 + j)
            v = z_ref[:, vcol:vcol + HEAD_DIM]
            ko = (2 * j) * grp + gi * HEAD_DIM
            vo = (2 * j + 1) * grp + gi * HEAD_DIM
            kvb_ref[:, ko:ko + HEAD_DIM] = k.astype(BF)
            kvb_ref[:, vo:vo + HEAD_DIM] = v.astype(BF)
            tt = k.shape[0]
            if j < 2:
                planes = 4 * N_KV
                kv_ref[pl.ds(2 * j * N_KV + gi, tt, stride=planes), :] = k
                kv_ref[pl.ds((2 * j + 1) * N_KV + gi, tt, stride=planes), :] = v
            else:
                planes = 2 * N_KV
                win_ref[pl.ds(gi, tt, stride=planes), :] = k
                win_ref[pl.ds(N_KV + gi, tt, stride=planes), :] = v


def _qk_prep(z, cosf, sinf, qk_g, *, tt, nz):
    B, T, _ = z.shape
    nq = N_HEADS * HEAD_DIM
    grp = N_KV * HEAD_DIM
    row = lambda b, i: (b, i, 0)
    return pl.pallas_call(
        functools.partial(_qk_kernel, scale=HEAD_DIM ** -0.5),
        grid=(B, T // tt),
        in_specs=[pl.BlockSpec((None, tt, nz), row),
                  pl.BlockSpec((tt, HEAD_DIM), lambda b, i: (i, 0)),
                  pl.BlockSpec((tt, HEAD_DIM), lambda b, i: (i, 0)),
                  pl.BlockSpec((4, HEAD_DIM), lambda b, i: (0, 0))],
        out_specs=[pl.BlockSpec((None, tt, nq), row),
                   pl.BlockSpec((None, tt * 4 * N_KV, HEAD_DIM), row),
                   pl.BlockSpec((None, tt * 2 * N_KV, HEAD_DIM), row),
                   pl.BlockSpec((None, tt, 6 * grp), row)],
        out_shape=[jax.ShapeDtypeStruct((B, T, nq), BF),
                   jax.ShapeDtypeStruct((B, T * 4 * N_KV, HEAD_DIM), F32),
                   jax.ShapeDtypeStruct((B, T * 2 * N_KV, HEAD_DIM), F32),
                   jax.ShapeDtypeStruct((B, T, 6 * grp), BF)],
        compiler_params=_cparams(("parallel", "arbitrary")),
    )(z, cosf, sinf, qk_g)


def _rope_tables(pos0, T):
    half = HEAD_DIM // 2
    inv = ROPE_THETA ** (-jnp.arange(half, dtype=F32) / half)
    ang = (pos0 + jnp.arange(T, dtype=F32))[:, None] * inv[None, :]
    cos, sin = jnp.cos(ang), jnp.sin(ang)
    return jnp.concatenate([cos, cos], axis=1), jnp.concatenate([-sin, sin], axis=1)


def _compress_acc(load_pair, pe_ref, w_ref, rows):
    acc = jnp.zeros((rows, 2 * HEAD_DIM), F32)
    for lp in range(CMP_BLOCK // 2):
        xs = []
        for l in (2 * lp, 2 * lp + 1):
            x = load_pair(l) + pe_ref[l][None]
            xs.append(x.reshape(rows, HEAD_DIM).astype(BF))
        acc = acc + _dot(jnp.concatenate(xs, axis=1), w_ref[lp])
    is_k = (lax.broadcasted_iota(jnp.int32, (rows, HEAD_DIM), 0) % 8) < N_KV
    return jnp.where(is_k, acc[:, :HEAD_DIM], acc[:, HEAD_DIM:])


def _compress_kernel(x_ref, pe_ref, w_ref, o_ref, *, nblk):
    rows = nblk * 8
    out = _compress_acc(lambda l: x_ref[pl.ds(l, nblk, stride=CMP_BLOCK), :, :], pe_ref, w_ref, rows)
    o_ref[...] = out.reshape(nblk, 8, HEAD_DIM)


def _compress(kvrows, pe_t, w2):
    B = kvrows.shape[0]
    T = kvrows.shape[1] // (4 * N_KV)
    nblk = T // CMP_BLOCK
    x = kvrows.reshape(B, T, 4 * N_KV, HEAD_DIM)
    return pl.pallas_call(
        functools.partial(_compress_kernel, nblk=nblk),
        grid=(B,),
        in_specs=[pl.BlockSpec((None, T, 8, HEAD_DIM), lambda b: (b, 0, 0, 0)),
                  pl.BlockSpec((CMP_BLOCK, 8, HEAD_DIM), lambda b: (0, 0, 0)),
                  pl.BlockSpec((CMP_BLOCK // 2, 2 * HEAD_DIM, 2 * HEAD_DIM), lambda b: (0, 0, 0))],
        out_specs=pl.BlockSpec((None, nblk, 8, HEAD_DIM), lambda b: (b, 0, 0, 0)),
        out_shape=jax.ShapeDtypeStruct((B, nblk, 8, HEAD_DIM), F32),
        compiler_params=_cparams(("parallel",)),
    )(x, pe_t, w2)


def _compress_paged_kernel(pt_ref, *refs, pb):
    page_refs = refs[:pb]
    pe_ref, w_ref, o_ref = refs[pb:pb + 3]
    rows = pb * 2 * 8

    def load(l):
        tiles = []
        for p in range(pb):
            for h in range(2):
                tiles.append(page_refs[p][h * CMP_BLOCK + l])
        return jnp.stack(tiles, axis=0)

    out = _compress_acc(load, pe_ref, w_ref, rows)
    o_ref[...] = out.reshape(pb * 2, 8, HEAD_DIM)


def _compress_paged(cache5, page_table, layer, pe_t, w2, *, pb):
    n_phys, page = cache5.shape[1], cache5.shape[2]
    B, n_pages = page_table.shape
    assert page == 2 * CMP_BLOCK and n_pages % pb == 0

    def page_spec(p):
        return pl.BlockSpec((None, None, page, 8, HEAD_DIM),
                            lambda b, i, pt: (layer, pt[b, i * pb + p], 0, 0, 0))

    grid_spec = pltpu.PrefetchScalarGridSpec(
        num_scalar_prefetch=1,
        grid=(B, n_pages // pb),
        in_specs=[page_spec(p) for p in range(pb)] + [
            pl.BlockSpec((CMP_BLOCK, 8, HEAD_DIM), lambda b, i, pt: (0, 0, 0)),
            pl.BlockSpec((CMP_BLOCK // 2, 2 * HEAD_DIM, 2 * HEAD_DIM), lambda b, i, pt: (0, 0, 0))],
        out_specs=pl.BlockSpec((None, pb * 2, 8, HEAD_DIM), lambda b, i, pt: (b, i, 0, 0)),
    )
    return pl.pallas_call(
        functools.partial(_compress_paged_kernel, pb=pb),
        grid_spec=grid_spec,
        out_shape=jax.ShapeDtypeStruct((B, n_pages * 2, 8, HEAD_DIM), F32),
        compiler_params=_cparams(("parallel", "arbitrary")),
    )(page_table, *([cache5] * pb), pe_t, w2)


def _online_step(s2, v, m_ref, l_ref, acc_ref):
    tk = s2.shape[1]
    m_prev = m_ref[...]
    m_next = jnp.maximum(m_prev, jnp.max(s2, axis=1, keepdims=True))
    alpha = jnp.exp(m_prev - m_next)
    p = jnp.exp(s2 - _rep_lanes(m_next, tk))
    l_ref[...] = alpha * l_ref[...] + jnp.sum(p, axis=1, keepdims=True)
    m_ref[...] = m_next
    acc_ref[...] = alpha * acc_ref[...] + _dot(p.astype(BF), v)


def _nsa_kernel(q_ref, kc_ref, vc_ref, ks_ref, vs_ref, kw_ref, vw_ref, g_ref, o_ref,
                m_ref, l_ref, acc_ref, *, tq, nblk):
    qi = pl.program_id(2)
    q0 = qi * tq
    rows = HPG * tq
    q = q_ref[...]
    qcat = jnp.concatenate([q[:, h * HEAD_DIM:(h + 1) * HEAD_DIM] for h in range(HPG)], axis=0)

    kc = kc_ref[...].astype(BF)
    vc = vc_ref[...].astype(BF)
    t_b = q0 + lax.broadcasted_iota(jnp.int32, (nblk, tq), 1)
    blk = lax.broadcasted_iota(jnp.int32, (nblk, tq), 0)
    complete = jnp.concatenate([(blk + 1) * CMP_BLOCK - 1 <= t_b] * HPG, axis=1)
    s_t = jnp.where(complete, _dot_nt(kc, qcat), NEG)
    mx = jnp.max(s_t, axis=0, keepdims=True)
    e = jnp.where(complete, jnp.exp(s_t - mx), 0.0)
    d = jnp.sum(e, axis=0, keepdims=True)
    p_t = e / jnp.where(d > 0, d, 1.0)

    imp = p_t[:, 0:tq]
    for h in range(1, HPG):
        imp = imp + p_t[:, h * tq:(h + 1) * tq]
    cur = jnp.right_shift(t_b, CMP_SHIFT)
    forced = (blk == 0) | (blk == cur) | (blk == cur - 1)
    score = jnp.where(forced, FORCED_SCORE, imp)
    score = jnp.where(blk <= cur, score, -1.0)
    rank = jnp.zeros((nblk, tq), F32)
    for i in range(nblk):
        si = score[i:i + 1, :]
        beats = (si > score) | ((si == score) & (blk > i))
        rank = rank + jnp.where(beats, 1.0, 0.0)
    sel_t = jnp.where((rank < min(N_SELECT, nblk)) & (score > -0.5), 1.0, 0.0).astype(BF)

    eye = jnp.where(lax.broadcasted_iota(jnp.int32, (tq, tq), 0)
                    == lax.broadcasted_iota(jnp.int32, (tq, tq), 1), 1.0, 0.0).astype(BF)
    p_bf = p_t.astype(BF)
    p_rows = jnp.concatenate([_dot_nt(eye, p_bf[:, h * tq:(h + 1) * tq]) for h in range(HPG)], axis=0)
    o_cmp = _dot(p_rows.astype(BF), vc)
    sel = _dot_nt(eye, sel_t).astype(BF)

    t_k = q0 + lax.broadcasted_iota(jnp.int32, (tq, tq), 0)
    k_off = lax.broadcasted_iota(jnp.int32, (tq, tq), 1)
    e_blk = lax.broadcasted_iota(jnp.int32, (nblk, tq), 0)
    e_off = lax.broadcasted_iota(jnp.int32, (nblk, tq), 1)

    def reset():
        m_ref[...] = jnp.full((rows, LANES), NEG, F32)
        l_ref[...] = jnp.zeros((rows, LANES), F32)
        acc_ref[...] = jnp.zeros((rows, HEAD_DIM), F32)

    def attend(c, k_ref, v_ref, mask):
        start = pl.multiple_of(c * tq, tq)
        k = k_ref[pl.ds(start, tq), :]
        v = v_ref[pl.ds(start, tq), :]
        s = jnp.where(mask[None], _dot_nt(qcat, k).reshape(HPG, tq, tq), NEG)
        _online_step(s.reshape(rows, tq), v, m_ref, l_ref, acc_ref)

    def sel_body(c, carry):
        expand = jnp.where(jnp.right_shift(c * tq + e_off, CMP_SHIFT) == e_blk, 1.0, 0.0).astype(BF)
        chosen = _dot(sel, expand) > 0.5
        attend(c, ks_ref, vs_ref, chosen & (c * tq + k_off <= t_k))
        return carry

    reset()
    lax.fori_loop(0, qi + 1, sel_body, 0)
    o_sel = acc_ref[...] / l_ref[...]

    def win_body(c, carry):
        dist = t_k - (c * tq + k_off)
        attend(c, kw_ref, vw_ref, (dist >= 0) & (dist < WINDOW))
        return carry

    reset()
    lax.fori_loop(jnp.maximum(qi - WINDOW // tq, 0), qi + 1, win_body, 0)
    o_win = acc_ref[...] / l_ref[...]

    g = jax.nn.sigmoid(g_ref[...])
    for h in range(HPG):
        r = slice(h * tq, (h + 1) * tq)
        o = (g[:, 3 * h:3 * h + 1] * o_cmp[r] + g[:, 3 * h + 1:3 * h + 2] * o_sel[r]
             + g[:, 3 * h + 2:3 * h + 3] * o_win[r])
        o_ref[:, h * HEAD_DIM:(h + 1) * HEAD_DIM] = o.astype(BF)


def _nsa_prompt(q, kcv, kvb, zg, *, tq, gate_col):
    B, T, _ = q.shape
    nblk = kcv.shape[1]
    col = lambda c: pl.BlockSpec((None, T, HEAD_DIM), lambda b, g, i, c=c: (b, 0, c + g))
    rows = HPG * tq
    return pl.pallas_call(
        functools.partial(_nsa_kernel, tq=tq, nblk=nblk),
        grid=(B, N_KV, T // tq),
        in_specs=[pl.BlockSpec((None, tq, HPG * HEAD_DIM), lambda b, g, i: (b, i, g)),
                  pl.BlockSpec((None, nblk, HEAD_DIM), lambda b, g, i: (b, 0, g)),
                  pl.BlockSpec((None, nblk, HEAD_DIM), lambda b, g, i: (b, 0, N_KV + g)),
                  col(2 * N_KV), col(3 * N_KV), col(4 * N_KV), col(5 * N_KV),
                  pl.BlockSpec((None, tq, LANES), lambda b, g, i: (b, i, gate_col + g))],
        out_specs=pl.BlockSpec((None, tq, HPG * HEAD_DIM), lambda b, g, i: (b, i, g)),
        out_shape=jax.ShapeDtypeStruct((B, T, N_HEADS * HEAD_DIM), BF),
        scratch_shapes=[pltpu.VMEM((rows, LANES), F32), pltpu.VMEM((rows, LANES), F32),
                        pltpu.VMEM((rows, HEAD_DIM), F32)],
        compiler_params=_cparams(("parallel", "parallel", "arbitrary")),
    )(q, kcv, kcv, kvb, kvb, kvb, kvb, zg)


def _softplus(y):
    return jnp.maximum(y, 0.0) + jnp.log1p(jnp.exp(-jnp.abs(y)))


def _rg_gates(u, w_ref, brg_ref, lam_ref):
    ub = u.astype(BF)
    r = jax.nn.sigmoid(_dot(ub, w_ref[0].astype(BF)) + brg_ref[0:1, :])
    i = jax.nn.sigmoid(_dot(ub, w_ref[1].astype(BF)) + brg_ref[1:2, :])
    log_a = -RG_C * r * _softplus(-lam_ref[...])
    a = jnp.exp(log_a)
    th = jnp.tanh(log_a)
    b = jnp.sqrt(-2.0 * th / (1.0 - th)) * i * u
    return a, b


def _rglru_kernel(xr_ref, zg_ref, c0_ref, h0_ref, cw_ref, cb_ref, w_ref, brg_ref, lam_ref,
                  y_ref, hl_ref, cl_ref, xp_ref, a_ref, b_ref, h_ref, *, T):
    pad = 8
    xp_ref[0:pad, :] = jnp.zeros((pad, RNN_BW), F32)
    xp_ref[pad - (CONV_W - 1):pad, :] = c0_ref[...]
    xp_ref[pad:pad + T, :] = xr_ref[...]
    u = cb_ref[...] + xp_ref[pad:pad + T, :] * cw_ref[CONV_W - 1:CONV_W, :]
    for k in range(CONV_W - 1):
        off = pad - (CONV_W - 1) + k
        u = u + xp_ref[off:off + T, :] * cw_ref[k:k + 1, :]
    cl_ref[...] = xp_ref[pad + T - (CONV_W - 1):pad + T, :]

    a, b = _rg_gates(u, w_ref, brg_ref, lam_ref)
    a_ref[0:T, :] = a
    b_ref[0:T, :] = b

    levels = []
    n, off = T, 0
    while True:
        levels.append((n, off))
        if n <= 8:
            break
        off += n
        n //= 8
    for (n, off), (n2, off2) in zip(levels[:-1], levels[1:]):
        A = a_ref[pl.ds(off, n2, stride=8), :]
        Bv = b_ref[pl.ds(off, n2, stride=8), :]
        for r in range(1, 8):
            ar = a_ref[pl.ds(off + r, n2, stride=8), :]
            br = b_ref[pl.ds(off + r, n2, stride=8), :]
            Bv = ar * Bv + br
            A = ar * A
        a_ref[off2:off2 + n2, :] = A
        b_ref[off2:off2 + n2, :] = Bv
    n, off = levels[-1]
    h = h0_ref[...]
    for t in range(n):
        h = a_ref[off + t:off + t + 1, :] * h + b_ref[off + t:off + t + 1, :]
        h_ref[off + 8 + t:off + 9 + t, :] = h
    for (n, off), (n2, off2) in reversed(list(zip(levels[:-1], levels[1:]))):
        h_ref[off2 + 7:off2 + 8, :] = h0_ref[...]
        hp = h_ref[off2 + 7:off2 + 7 + n2, :]
        for r in range(8):
            ar = a_ref[pl.ds(off + r, n2, stride=8), :]
            br = b_ref[pl.ds(off + r, n2, stride=8), :]
            hp = ar * hp + br
            h_ref[pl.ds(off + 8 + r, n2, stride=8), :] = hp
    hfull = h_ref[8:8 + T, :]
    y_ref[...] = (jax.nn.gelu(zg_ref[...]) * hfull).astype(BF)
    hl_ref[...] = h_ref[8 + T - 1:8 + T, :]


def _rglru(zr, conv0, h0, conv_w, conv_b, w_rg, b_rg, lam, *, d_rnn, col0):
    B, T, _ = zr.shape
    nb = d_rnn // RNN_BW
    assert T % 8 == 0
    total = T + T // 4 + 64
    return pl.pallas_call(
        functools.partial(_rglru_kernel, T=T),
        grid=(B, nb),
        in_specs=[pl.BlockSpec((None, T, RNN_BW), lambda b, n: (b, 0, col0 + n)),
                  pl.BlockSpec((None, T, RNN_BW), lambda b, n: (b, 0, col0 + nb + n)),
                  pl.BlockSpec((None, CONV_W - 1, RNN_BW), lambda b, n: (b, 0, n)),
                  pl.BlockSpec((None, 1, RNN_BW), lambda b, n: (b, 0, n)),
                  pl.BlockSpec((CONV_W, RNN_BW), lambda b, n: (0, n)),
                  pl.BlockSpec((1, RNN_BW), lambda b, n: (0, n)),
                  pl.BlockSpec((2, None, RNN_BW, RNN_BW), lambda b, n: (0, n, 0, 0)),
                  pl.BlockSpec((2, RNN_BW), lambda b, n: (0, n)),
                  pl.BlockSpec((1, RNN_BW), lambda b, n: (0, n))],
        out_specs=[pl.BlockSpec((None, T, RNN_BW), lambda b, n: (b, 0, n)),
                   pl.BlockSpec((None, 1, RNN_BW), lambda b, n: (b, 0, n)),
                   pl.BlockSpec((None, CONV_W - 1, RNN_BW), lambda b, n: (b, 0, n))],
        out_shape=[jax.ShapeDtypeStruct((B, T, d_rnn), BF),
                   jax.ShapeDtypeStruct((B, 1, d_rnn), F32),
                   jax.ShapeDtypeStruct((B, CONV_W - 1, d_rnn), F32)],
        scratch_shapes=[pltpu.VMEM((T + 8, RNN_BW), F32), pltpu.VMEM((total, RNN_BW), F32),
                        pltpu.VMEM((total, RNN_BW), F32), pltpu.VMEM((total + 8, RNN_BW), F32)],
        compiler_params=_cparams(("parallel", "arbitrary")),
    )(zr, zr, conv0, h0, conv_w, conv_b[None], w_rg, b_rg, lam[None])


def _rglru_step_kernel(xr_ref, zg_ref, c0_ref, h0_ref, cw_ref, cb_ref, w_ref, brg_ref, lam_ref,
                       y_ref, hl_ref):
    u = cb_ref[...] + xr_ref[...] * cw_ref[CONV_W - 1:CONV_W, :]
    for k in range(CONV_W - 1):
        u = u + c0_ref[k] * cw_ref[k:k + 1, :]
    a, b = _rg_gates(u, w_ref, brg_ref, lam_ref)
    h = a * h0_ref[...] + b
    hl_ref[...] = h
    y_ref[...] = (jax.nn.gelu(zg_ref[...]) * h).astype(BF)


def _rglru_step(zr, conv0_t, h0, conv_w, conv_b, w_rg, b_rg, lam, *, d_rnn, col0):
    R = zr.shape[0]
    nb = d_rnn // RNN_BW
    return pl.pallas_call(
        _rglru_step_kernel,
        grid=(nb,),
        in_specs=[pl.BlockSpec((R, RNN_BW), lambda n: (0, col0 + n)),
                  pl.BlockSpec((R, RNN_BW), lambda n: (0, col0 + nb + n)),
                  pl.BlockSpec((CONV_W - 1, R, RNN_BW), lambda n: (0, 0, n)),
                  pl.BlockSpec((R, RNN_BW), lambda n: (0, n)),
                  pl.BlockSpec((CONV_W, RNN_BW), lambda n: (0, n)),
                  pl.BlockSpec((1, RNN_BW), lambda n: (0, n)),
                  pl.BlockSpec((2, None, RNN_BW, RNN_BW), lambda n: (0, n, 0, 0)),
                  pl.BlockSpec((2, RNN_BW), lambda n: (0, n)),
                  pl.BlockSpec((1, RNN_BW), lambda n: (0, n))],
        out_specs=[pl.BlockSpec((R, RNN_BW), lambda n: (0, n)),
                   pl.BlockSpec((R, RNN_BW), lambda n: (0, n))],
        out_shape=[jax.ShapeDtypeStruct((R, d_rnn), BF), jax.ShapeDtypeStruct((R, d_rnn), F32)],
        compiler_params=_cparams(("arbitrary",)),
    )(zr, zr, conv0_t, h0, conv_w, conv_b[None], w_rg, b_rg, lam[None])


def _dec_cmp_kernel(q_ref, kcv_ref, o_ref, idx_ref, *, n_past, t_pos):
    nlan = LANES * ((n_past + 1 + LANES - 1) // LANES)
    blk = lax.broadcasted_iota(jnp.int32, (1, n_past), 1)
    complete = (blk + 1) * CMP_BLOCK - 1 <= t_pos
    imps = []
    for g in range(N_KV):
        kc = kcv_ref[:, g * HEAD_DIM:(g + 1) * HEAD_DIM].astype(BF)
        vc = kcv_ref[:, (N_KV + g) * HEAD_DIM:(N_KV + g + 1) * HEAD_DIM].astype(BF)
        s = jnp.where(complete, _dot_nt(q_ref[g], kc), NEG)
        mx = jnp.max(s, axis=-1, keepdims=True)
        e = jnp.where(complete, jnp.exp(s - mx), 0.0)
        d = jnp.sum(e, axis=-1, keepdims=True)
        p = e / jnp.where(d > 0, d, 1.0)
        o_ref[g] = _dot(p.astype(BF), vc)
        head = lax.broadcasted_iota(jnp.int32, p.shape, 0) < HPG
        imps.append(jnp.sum(jnp.where(head, p, 0.0), axis=0, keepdims=True))
    imp = jnp.concatenate(imps, axis=0)

    cur = t_pos // CMP_BLOCK
    imp_all = jnp.concatenate([imp, jnp.zeros((N_KV, nlan - n_past), F32)], axis=1)
    blk_all = lax.broadcasted_iota(jnp.int32, (N_KV, nlan), 1)
    forced = (blk_all == 0) | (blk_all == cur) | (blk_all == cur - 1)
    score = jnp.where(forced, FORCED_SCORE, imp_all)
    score = jnp.where(blk_all <= cur, score, -1.0)
    score = jnp.where(blk_all <= n_past, score, -2.0)
    blk_f = blk_all.astype(F32)
    slot = lax.broadcasted_iota(jnp.int32, (N_KV, LANES), 1)
    res = jnp.full((N_KV, LANES), float(n_past), F32)
    for k in range(min(N_SELECT, n_past + 1)):
        top = jnp.max(score, axis=-1, keepdims=True)
        arg = jnp.min(jnp.where(score == top, blk_f, float(nlan)), axis=-1, keepdims=True)
        res = jnp.where(slot == k, jnp.where(top > -0.5, arg, float(n_past)), res)
        score = jnp.where(blk_f == arg, -3.0, score)
    idx_ref[...] = res.astype(jnp.int32)


def _dec_cmp(q16, kcv, *, t_pos):
    B = q16.shape[0]
    n_past = kcv.shape[1]
    return pl.pallas_call(
        functools.partial(_dec_cmp_kernel, n_past=n_past, t_pos=t_pos),
        grid=(B,),
        in_specs=[pl.BlockSpec((None, N_KV, 16, HEAD_DIM), lambda b: (b, 0, 0, 0)),
                  pl.BlockSpec((None, n_past, 2 * N_KV * HEAD_DIM), lambda b: (b, 0, 0))],
        out_specs=[pl.BlockSpec((None, N_KV, 16, HEAD_DIM), lambda b: (b, 0, 0, 0)),
                   pl.BlockSpec((None, N_KV, LANES), lambda b: (b, 0, 0))],
        out_shape=[jax.ShapeDtypeStruct((B, N_KV, 16, HEAD_DIM), F32),
                   jax.ShapeDtypeStruct((B, N_KV, LANES), jnp.int32)],
        compiler_params=_cparams(("parallel",)),
    )(q16, kcv)


def _dec_attn_kernel(pt_ref, ti_ref, q_ref, c0_ref, c1_ref, c2_ref, c3_ref, new_ref, win_ref, oc_ref, g_ref,
                     o_ref, m_ref, l_ref, acc_ref, *, n_past):
    b, s = pl.program_id(0), pl.program_id(1)
    cache_refs = (c0_ref, c1_ref, c2_ref, c3_ref)
    planes = 4 * N_KV

    @pl.when(s == 0)
    def _():
        m_ref[...] = jnp.full(m_ref.shape, NEG, F32)
        l_ref[...] = jnp.zeros(l_ref.shape, F32)
        acc_ref[...] = jnp.zeros(acc_ref.shape, F32)

    for g in range(N_KV):
        live = ti_ref[b * N_KV + g, s] < n_past
        k = cache_refs[g][pl.ds(2 * N_KV + g, CMP_BLOCK, stride=planes), :].astype(BF)
        v = cache_refs[g][pl.ds(3 * N_KV + g, CMP_BLOCK, stride=planes), :].astype(BF)
        sc = jnp.where(live, _dot_nt(q_ref[g], k), NEG)
        m_prev = m_ref[g]
        m_next = jnp.maximum(m_prev, jnp.max(sc, axis=1, keepdims=True))
        alpha = jnp.exp(m_prev - m_next)
        p = jnp.where(live, jnp.exp(sc - m_next[:, :CMP_BLOCK]), 0.0)
        l_ref[g] = alpha * l_ref[g] + jnp.sum(p, axis=1, keepdims=True)
        m_ref[g] = m_next
        acc_ref[g] = alpha * acc_ref[g] + _dot(p.astype(BF), v)

    @pl.when(s == pl.num_programs(1) - 1)
    def _():
        n_win = win_ref.shape[0] // (2 * N_KV)
        keep = lax.broadcasted_iota(jnp.int32, (1, n_win), 1) >= n_win + 1 - WINDOW
        for g in range(N_KV):
            q = q_ref[g]
            qf = q.astype(F32)

            def self_score(k_row):
                return jnp.sum(qf * k_row.astype(BF).astype(F32), axis=1, keepdims=True)

            s_self = self_score(new_ref[g, 2:3, :])
            v_self = new_ref[g, 3:4, :].astype(BF).astype(F32)
            m_prev = m_ref[g]
            m_fin = jnp.maximum(m_prev, s_self)
            alpha = jnp.exp(m_prev - m_fin)
            p_self = jnp.exp(s_self - m_fin)
            o_sel = (alpha * acc_ref[g] + p_self * v_self) / (alpha * l_ref[g] + p_self)

            kw = win_ref[pl.ds(g, n_win, stride=2 * N_KV), :].astype(BF)
            vw = win_ref[pl.ds(N_KV + g, n_win, stride=2 * N_KV), :].astype(BF)
            sw = jnp.where(keep, _dot_nt(q, kw), NEG)
            sw_self = self_score(new_ref[g, 4:5, :])
            mw = jnp.maximum(jnp.max(sw, axis=1, keepdims=True), sw_self)
            pw = jnp.where(keep, jnp.exp(sw - mw), 0.0)
            pw_self = jnp.exp(sw_self - mw)
            num = _dot(pw.astype(BF), vw) + pw_self * new_ref[g, 5:6, :].astype(BF).astype(F32)
            o_win = num / (jnp.sum(pw, axis=1, keepdims=True) + pw_self)

            gt = jax.nn.sigmoid(g_ref[g])
            o_ref[g] = gt[:, 0:1] * oc_ref[g] + gt[:, 1:2] * o_sel + gt[:, 2:3] * o_win


def _dec_attn(q16, cache4, page_table, top_i, newkv, win3, o_cmp, g16, *, layer, n_past):
    B = q16.shape[0]
    n_slots = top_i.shape[1]
    half_rows = CMP_BLOCK * 4 * N_KV

    def half_page(g):
        def index(b, s, pt, ti):
            blk = jnp.minimum(ti[b * N_KV + g, s], n_past - 1)
            return (layer, pt[b, blk // 2], blk % 2, 0)
        return pl.BlockSpec((None, None, half_rows, HEAD_DIM), index)

    per_seq = lambda b, s, pt, ti: (b, 0, 0, 0)
    grid_spec = pltpu.PrefetchScalarGridSpec(
        num_scalar_prefetch=2,
        grid=(B, n_slots),
        in_specs=[pl.BlockSpec((None, N_KV, 16, HEAD_DIM), per_seq)]
        + [half_page(g) for g in range(N_KV)]
        + [pl.BlockSpec((None, N_KV, 6, HEAD_DIM), per_seq),
           pl.BlockSpec((None, win3.shape[1], HEAD_DIM), lambda b, s, pt, ti: (layer * B + b, 0, 0)),
           pl.BlockSpec((None, N_KV, 16, HEAD_DIM), per_seq),
           pl.BlockSpec((None, N_KV, 16, LANES), per_seq)],
        out_specs=pl.BlockSpec((None, N_KV, 16, HEAD_DIM), per_seq),
        scratch_shapes=[pltpu.VMEM((N_KV, 16, LANES), F32), pltpu.VMEM((N_KV, 16, LANES), F32),
                        pltpu.VMEM((N_KV, 16, HEAD_DIM), F32)],
    )
    return pl.pallas_call(
        functools.partial(_dec_attn_kernel, n_past=n_past),
        grid_spec=grid_spec,
        out_shape=jax.ShapeDtypeStruct((B, N_KV, 16, HEAD_DIM), F32),
        compiler_params=_cparams(("parallel", "arbitrary")),
    )(page_table, top_i, q16, *([cache4] * N_KV), newkv, win3, o_cmp, g16)


def _regroup_w_in(w_in):
    L, D, _ = w_in.shape
    o_gn = N_HEADS * HEAD_DIM + 6 * N_KV * HEAD_DIM
    o_rx = o_gn + 3 * N_HEADS
    wg = w_in[:, :, o_gn:o_rx].reshape(L, D, N_KV, 3 * HPG)
    wg = jnp.pad(wg, ((0, 0), (0, 0), (0, 0), (0, LANES - 3 * HPG))).reshape(L, D, N_KV * LANES)
    return jnp.concatenate([w_in[:, :, :o_gn], wg, w_in[:, :, o_rx:]], axis=2).astype(BF)


def _row_tile(m):
    return 1024 if m % 1024 == 0 else 256


def _largest_divisor(n, cap):
    return max(d for d in range(1, cap + 1) if n % d == 0)


def _layer(l, xp, xs, mod_p, mod_s, P, rope_p, rope_s, cache_kv, page_table, state_win, state_rnn, state_conv):
    B, T, D = xp.shape
    M = B * T
    R = xs.shape[0]
    nb, n_pages = page_table.shape
    depth, n_phys, page = cache_kv.shape[:3]
    n_past = n_pages * page // CMP_BLOCK
    d_attn = N_HEADS * HEAD_DIM
    d_rnn = P["conv_w"].shape[2]
    o_gn = d_attn + 6 * N_KV * HEAD_DIM
    o_rx = o_gn + 3 * N_HEADS
    tm, tn = _row_tile(M), 512
    shp1, scp1, gap1, shp2, scp2, gap2 = mod_p
    shs1, scs1, gas1, shs2, scs2, gas2 = mod_s
    bmap = lambda j, i: (i // (T // tm), 0, j)
    g_ln = P["g_ln"][l]

    c_g = o_gn
    c_r = c_g + N_KV * LANES
    c_mg = c_r + 2 * d_rnn
    w2 = jnp.concatenate([P["w_phi"][l, 0], P["w_phi"][l, 1]], axis=-1)
    w2 = w2.reshape(CMP_BLOCK // 2, 2 * HEAD_DIM, 2 * HEAD_DIM).astype(BF)
    pe_t = jnp.concatenate([jnp.broadcast_to(P["pe_cmp"][l, j][:, None, :], (CMP_BLOCK, N_KV, HEAD_DIM))
                            for j in range(2)], axis=1)
    rnn = (P["conv_w"][l], P["conv_b"][l], P["w_rg"][l], P["b_rg"][l], P["lam"][l])

    hp = _ln_mod(xp, g_ln[0:1], scp1, shp1, tt=256).reshape(M, D)
    hs = _ln_mod(xs[None], g_ln[0:1], scs1, shs1, tt=R).reshape(R, D)
    zp, zs = _mm2(hp, hs, P["w_in_bf"], layer=l, tm=tm, tn=tn)
    zp = zp.reshape(B, T, -1)

    q, kvrows, winrows, kvb = _qk_prep(zp, rope_p[0], rope_p[1], P["qk_g"][l], tt=256, nz=o_gn)
    kcv = _compress(kvrows, pe_t, w2)
    o_nsa_p = _nsa_prompt(q, kcv.reshape(B, T // CMP_BLOCK, 8 * HEAD_DIM), kvb, zp, tq=256,
                          gate_col=c_g // LANES)
    conv0 = jnp.zeros((B, CONV_W - 1, d_rnn), F32)
    h0 = jnp.zeros((B, 1, d_rnn), F32)
    y_rnn_p, h_last, conv_last = _rglru(zp, conv0, h0, *rnn, d_rnn=d_rnn, col0=c_r // RNN_BW)

    qs, kvrows_s, winrows_s, _ = _qk_prep(zs[None], rope_s[0], rope_s[1], P["qk_g"][l], tt=R, nz=o_gn)
    q16 = jnp.pad(qs[0, :nb].reshape(nb, N_KV, HPG, HEAD_DIM), ((0, 0), (0, 0), (0, 16 - HPG), (0, 0)))
    g16 = jnp.pad(zs[:nb, c_g:c_r].reshape(nb, N_KV, LANES)[:, :, :3 * HPG].reshape(nb, N_KV, HPG, 3),
                  ((0, 0), (0, 0), (0, 16 - HPG), (0, LANES - 3)))
    kv_new = kvrows_s.reshape(R, 4, N_KV, HEAD_DIM)[:nb]
    win_new = winrows_s.reshape(R, 2, N_KV, HEAD_DIM)[:nb]
    new6 = jnp.concatenate([kv_new, win_new], axis=1).transpose(0, 2, 1, 3)
    cache5 = cache_kv.reshape(depth, n_phys, page, 4 * N_KV, HEAD_DIM)
    kcv_s = _compress_paged(cache5, page_table, l, pe_t, w2, pb=_largest_divisor(n_pages, 32))
    o_cmp, top = _dec_cmp(q16, kcv_s.reshape(nb, n_past, 8 * HEAD_DIM), t_pos=n_pages * page)
    top_i = top[:, :, :N_SELECT].reshape(nb * N_KV, N_SELECT)
    cache4 = cache_kv.reshape(depth, n_phys, page * 4 * N_KV, HEAD_DIM)
    n_win = state_win.shape[2]
    win3 = state_win.reshape(depth * nb, n_win * 2 * N_KV, HEAD_DIM)
    o16 = _dec_attn(q16, cache4, page_table, top_i, new6, win3, o_cmp, g16, layer=l, n_past=n_past)
    o_nsa_s = jnp.pad(o16[:, :, :HPG].reshape(nb, -1), ((0, R - nb), (0, 0))).astype(BF)
    pad_r = lambda a: jnp.pad(a, ((0, R - nb),) + ((0, 0),) * (a.ndim - 1))
    conv0_t = pad_r(state_conv[l]).transpose(1, 0, 2)
    y_rnn_s, h_new = _rglru_step(zs, conv0_t, pad_r(state_rnn[l]), *rnn, d_rnn=d_rnn, col0=c_r // RNN_BW)

    mg_p, mg_s = _merge2(o_nsa_p.reshape(M, -1), y_rnn_p.reshape(M, -1), zp.reshape(M, -1), o_nsa_s, y_rnn_s, zs,
                         P["w_branch_nsa"], P["w_branch_rnn"], layer=l, col0=c_mg, tm=tm, tn=tn)
    x1p, x1s = _mm2(mg_p, mg_s, P["w_out"], layer=l, tm=tm, tn=tn, epi="resid",
                    res=(xp.reshape(M, D), gap1, bmap, xs, gas1))
    h2p = _ln_mod(x1p.reshape(B, T, D), g_ln[1:2], scp2, shp2, tt=256).reshape(M, D)
    h2s = _ln_mod(x1s[None], g_ln[1:2], scs2, shs2, tt=R).reshape(R, D)
    up, us = _mm2(h2p, h2s, P["w_mlp1"], layer=l, tm=tm, tn=tn, out_dtype=BF, epi="relu2")
    x2p = _mm(up, P["w_mlp2_bf"], layer=l, tm=tm, tn=1024, tk=2048, epi="resid", x_res=x1p, ga=gap2,
              ga_map=lambda i, j, k: (i // (T // tm), 0, j))
    x2s = _mm(us, P["w_mlp2_bf"], layer=l, tm=R, tn=1024, tk=4096, epi="resid", x_res=x1s, ga=gas2,
              ga_map=lambda i, j, k: (0, 0, j))

    n_keep = min(WINDOW, T)
    outs_p = (kvrows.reshape(B, T, 4, N_KV, HEAD_DIM),
              winrows.reshape(B, T, 2, N_KV, HEAD_DIM)[:, T - n_keep:],
              h_last.reshape(B, -1), conv_last)
    outs_s = (kv_new[:, None],
              jnp.concatenate([state_win[l][:, 1:], win_new[:, None]], axis=1),
              h_new[:nb],
              jnp.concatenate([state_conv[l][:, 1:], zs[:nb, None, c_r:c_r + d_rnn]], axis=1))
    return x2p.reshape(B, T, D), x2s, outs_p, outs_s


def kernel(x_prompt, x_sample, cache_kv, state_win, state_rnn, state_conv, page_table, c_prompt, c_sample,
           w_ada, b_ada, g_ln, w_in, qk_g, w_phi, pe_cmp, conv_w, conv_b, w_rg, b_rg, lam,
           w_branch_nsa, w_branch_rnn, w_out, w_mlp1, w_mlp2):
    depth = w_ada.shape[0]
    B, T, D = x_prompt.shape
    nb = x_sample.shape[0]
    R = 16
    past_len = page_table.shape[1] * cache_kv.shape[2]

    c_all = jnp.zeros((2 * R, D), F32).at[:B].set(c_prompt).at[R:R + nb].set(c_sample)
    rope_p = _rope_tables(0.0, T)
    rope_s = tuple(jnp.broadcast_to(t, (R, HEAD_DIM)) for t in _rope_tables(float(past_len), 1))

    P = dict(g_ln=g_ln, w_in_bf=_regroup_w_in(w_in), qk_g=qk_g, w_phi=w_phi, pe_cmp=pe_cmp, conv_w=conv_w, conv_b=conv_b,
             w_rg=w_rg, b_rg=b_rg, lam=lam, w_branch_nsa=w_branch_nsa, w_branch_rnn=w_branch_rnn,
             w_out=w_out, w_mlp1=w_mlp1, w_mlp2_bf=w_mlp2.astype(BF))
    yp = x_prompt
    ys = jnp.pad(x_sample.reshape(nb, D), ((0, R - nb), (0, 0)))
    outs = [[] for _ in range(8)]
    for l in range(depth):
        mod = _mm(c_all, w_ada, layer=l, tm=2 * R, tn=512, epi="silu_bias", bias=b_ada[l][None])
        mod = mod.reshape(2 * R, 6, D)
        mod_p = [mod[:B, k][:, None, :] for k in range(6)]
        mod_s = [mod[R:, k][None] for k in range(6)]
        yp, ys, (a1, a2, a3, a4), (b1, b2, b3, b4) = _layer(
            l, yp, ys, mod_p, mod_s, P, rope_p, rope_s, cache_kv, page_table, state_win, state_rnn, state_conv)
        for lst, v in zip(outs, (a1, b1, a2, b2, a3, b3, a4, b4)):
            lst.append(v)
    kv_p, kv_s, win_p, win_s, h_p, h_s, cv_p, cv_s = (jnp.stack(v) for v in outs)
    return (yp, ys[:nb].reshape(nb, 1, D), kv_p, kv_s, win_p, win_s, h_p, h_s, cv_p, cv_s)
```

```python
import functools
import math

import jax
import jax.numpy as jnp
from jax import lax
from jax.experimental import pallas as pl
from jax.experimental.pallas import tpu as pltpu

BF = jnp.bfloat16
F32 = jnp.float32

HEAD_DIM = 128
N_KV = 4
HPG = 4
N_HEADS = N_KV * HPG
CMP_BLOCK = 64
CMP_SHIFT = 6
N_SELECT = 16
WINDOW = 512
FORCED_SCORE = 1e4
RNN_BW = 128
CONV_W = 4
RG_C = 8.0
ROPE_THETA = 10000.0
EPS = 1e-6
NEG = -1e30
LANES = 128
VMEM_LIMIT = 56 * 1024 * 1024


def _cparams(sem):
    return pltpu.CompilerParams(dimension_semantics=sem, vmem_limit_bytes=VMEM_LIMIT)


def _dot(a, b):
    return jnp.dot(a, b, preferred_element_type=F32)


def _dot_nt(a, b):
    return lax.dot_general(a, b, (((1,), (1,)), ((), ())), preferred_element_type=F32)


def _rep_lanes(x, n):
    return x if n == LANES else jnp.concatenate([x] * (n // LANES), axis=1)


def _mm_kernel(*refs, nk, epi):
    if epi == "resid":
        a_ref, w_ref, x_ref, ga_ref, o_ref = refs[:5]
        scratch = refs[5:]
    elif epi == "silu_bias":
        a_ref, w_ref, bias_ref, o_ref = refs[:4]
        scratch = refs[4:]
    else:
        a_ref, w_ref, o_ref = refs[:3]
        scratch = refs[3:]

    def finish(acc):
        if epi == "relu2":
            r = jnp.maximum(acc, 0.0)
            acc = r * r
        elif epi == "resid":
            acc = x_ref[...] + ga_ref[...] * acc
        elif epi == "silu_bias":
            acc = acc + bias_ref[...]
        o_ref[...] = acc.astype(o_ref.dtype)

    a = a_ref[...]
    if epi == "silu_bias":
        a = (a * jax.nn.sigmoid(a)).astype(BF)
    part = _dot(a, w_ref[...].astype(BF))
    if nk == 1:
        finish(part)
    else:
        acc_ref = scratch[0]
        k = pl.program_id(2)

        @pl.when(k == 0)
        def _():
            acc_ref[...] = part

        @pl.when(k > 0)
        def _():
            acc_ref[...] += part

        @pl.when(k == nk - 1)
        def _():
            finish(acc_ref[...])


def _mm(a, w, *, tm, tn, tk=None, out_dtype=F32, epi=None, x_res=None, ga=None, ga_map=None, bias=None,
        layer=0):
    M, K = a.shape
    N = w.shape[2]
    tk = K if tk is None else tk
    nk = K // tk
    grid = (M // tm, N // tn, nk)
    in_specs = [pl.BlockSpec((tm, tk), lambda i, j, k: (i, k)),
                pl.BlockSpec((None, tk, tn), lambda i, j, k: (layer, k, j))]
    args = [a, w]
    if epi == "resid":
        in_specs.append(pl.BlockSpec((tm, tn), lambda i, j, k: (i, j)))
        in_specs.append(pl.BlockSpec((None, ga.shape[1], tn), ga_map))
        args += [x_res, ga]
    elif epi == "silu_bias":
        in_specs.append(pl.BlockSpec((1, tn), lambda i, j, k: (0, j)))
        args.append(bias)
    scratch =[pltpu.VMEM((tm, tn), F32)] if nk > 1 else []
    return pl.pallas_call(
        functools.partial(_mm_kernel, nk=nk, epi=epi),
        grid=grid, in_specs=in_specs,
        out_specs=pl.BlockSpec((tm, tn), lambda i, j, k: (i, j)),
        out_shape=jax.ShapeDtypeStruct((M, N), out_dtype),
        scratch_shapes=scratch,
        compiler_params=_cparams(("parallel", "parallel", "arbitrary")),
    )(*args)


def _cached_bf16(w_ref, wbf_ref, first):
    if wbf_ref is None:
        return w_ref[...]

    @pl.when(first)
    def _():
        wbf_ref[...] = w_ref[...].astype(BF)

    return wbf_ref[...]


def _mm2_kernel(*refs, epi, cast_w, n_side):
    n_res = 4 if epi == "resid" else 0
    n_in = 3 + n_res + n_side
    a_ref, a2_ref, w_ref = refs[:3]
    side_src = refs[3 + n_res:n_in]
    o_ref, o2_ref = refs[n_in:n_in + 2]
    side_dst = refs[n_in + 2:n_in + 2 + n_side]
    wbf_ref = refs[n_in + 2 + n_side] if cast_w else None
    first = pl.program_id(1) == 0
    w = _cached_bf16(w_ref, wbf_ref, first)
    for src, dst in zip(side_src, side_dst):
        dst[...] = src[...].astype(BF)

    def finish(acc, x_ref, ga_ref, out_ref):
        if epi == "relu2":
            r = jnp.maximum(acc, 0.0)
            acc = r * r
        elif epi == "resid":
            acc = x_ref[...] + ga_ref[...] * acc
        out_ref[...] = acc.astype(out_ref.dtype)

    extra = refs[3:7] if epi == "resid" else (None,) * 4
    finish(_dot(a_ref[...], w), extra[0], extra[1], o_ref)

    @pl.when(first)
    def _():
        finish(_dot(a2_ref[...], w), extra[2], extra[3], o2_ref)


def _mm2(a, a2, w, *, layer, tm, tn, out_dtype=F32, epi=None, res=None, side_cast=()):
    M, K = a.shape
    R2 = a2.shape[0]
    N = w.shape[2]
    cast_w = w.dtype != BF
    n_i = M // tm
    n_steps = (N // tn) * n_i
    in_specs = [pl.BlockSpec((tm, K), lambda j, i: (i, 0)),
                pl.BlockSpec((R2, K), lambda j, i: (0, 0)),
                pl.BlockSpec((None, K, tn), lambda j, i: (layer, 0, j))]
    args = [a, a2, w]
    if epi == "resid":
        x, ga, ga_map, x2, ga2 = res
        in_specs += [pl.BlockSpec((tm, tn), lambda j, i: (i, j)),
                     pl.BlockSpec((None, ga.shape[1], tn), ga_map),
                     pl.BlockSpec((R2, tn), lambda j, i: (0, j)),
                     pl.BlockSpec((None, R2, tn), lambda j, i: (0, 0, j))]
        args += [x, ga, x2, ga2]
    out_specs = [pl.BlockSpec((tm, tn), lambda j, i: (i, j)),
                 pl.BlockSpec((R2, tn), lambda j, i: (0, j))]
    out_shape = [jax.ShapeDtypeStruct((M, N), out_dtype), jax.ShapeDtypeStruct((R2, N), out_dtype)]
    for p in side_cast:
        _, rows, cols = p.shape
        rb = next(r for r in range(16, rows + 1, 16) if rows % r == 0 and rows // r <= n_steps)
        last = rows // rb - 1
        step_block = lambda j, i, last=last: (0, jnp.minimum(j * n_i + i, last), 0)
        in_specs.append(pl.BlockSpec((None, rb, cols),
                                     lambda j, i, last=last: (layer, jnp.minimum(j * n_i + i, last), 0)))
        args.append(p)
        out_specs.append(pl.BlockSpec((None, rb, cols), step_block))
        out_shape.append(jax.ShapeDtypeStruct((1, rows, cols), BF))
    return pl.pallas_call(
        functools.partial(_mm2_kernel, epi=epi, cast_w=cast_w, n_side=len(side_cast)),
        grid=(N // tn, n_i), in_specs=in_specs, out_specs=out_specs, out_shape=out_shape,
        scratch_shapes=[pltpu.VMEM((K, tn), BF)] if cast_w else [],
        compiler_params=_cparams(("arbitrary", "arbitrary")),
    )(*args)


def _merge2_kernel(an_ref, ar_ref, gn_ref, gr_ref, an2_ref, ar2_ref, gn2_ref, gr2_ref, wn_ref, wr_ref,
                   o_ref, o2_ref, wnb_ref, wrb_ref):
    first = pl.program_id(1) == 0
    wn = _cached_bf16(wn_ref, wnb_ref, first)
    wr = _cached_bf16(wr_ref, wrb_ref, first)

    def finish(a_n, a_r, g_n, g_r, out_ref):
        o = jax.nn.sigmoid(g_n[...]) * _dot(a_n[...], wn) + jax.nn.sigmoid(g_r[...]) * _dot(a_r[...], wr)
        out_ref[...] = o.astype(out_ref.dtype)

    finish(an_ref, ar_ref, gn_ref, gr_ref, o_ref)

    @pl.when(first)
    def _():
        finish(an2_ref, ar2_ref, gn2_ref, gr2_ref, o2_ref)


def _merge2(o_nsa, o_rnn, z, o_nsa2, o_rnn2, z2, w_nsa, w_rnn, *, layer, col0, tm, tn):
    M, K = o_nsa.shape
    R2 = o_nsa2.shape[0]
    D = w_nsa.shape[2]
    c0, c1 = col0 // tn, (col0 + D) // tn
    row = lambda j, i: (i, 0)
    one = lambda j, i: (0, 0)
    return pl.pallas_call(
        _merge2_kernel,
        grid=(D // tn, M // tm),
        in_specs=[pl.BlockSpec((tm, K), row), pl.BlockSpec((tm, K), row),
                  pl.BlockSpec((tm, tn), lambda j, i: (i, j + c0)),
                  pl.BlockSpec((tm, tn), lambda j, i: (i, j + c1)),
                  pl.BlockSpec((R2, K), one), pl.BlockSpec((R2, K), one),
                  pl.BlockSpec((R2, tn), lambda j, i: (0, j + c0)),
                  pl.BlockSpec((R2, tn), lambda j, i: (0, j + c1)),
                  pl.BlockSpec((None, K, tn), lambda j, i: (layer, 0, j)),
                  pl.BlockSpec((None, K, tn), lambda j, i: (layer, 0, j))],
        out_specs=[pl.BlockSpec((tm, tn), lambda j, i: (i, j)),
                   pl.BlockSpec((R2, tn), lambda j, i: (0, j))],
        out_shape=[jax.ShapeDtypeStruct((M, D), BF), jax.ShapeDtypeStruct((R2, D), BF)],
        scratch_shapes=[pltpu.VMEM((K, tn), BF), pltpu.VMEM((K, tn), BF)],
        compiler_params=_cparams(("arbitrary", "arbitrary")),
    )(o_nsa, o_rnn, z, z, o_nsa2, o_rnn2, z2, z2, w_nsa, w_rnn)


def _ln_kernel(x_ref, g_ref, sc_ref, sh_ref, o_ref):
    x = x_ref[...]
    ms = jnp.mean(x * x, axis=-1, keepdims=True)
    y = x * lax.rsqrt(ms + EPS) * g_ref[...]
    o_ref[...] = (y * (1.0 + sc_ref[...]) + sh_ref[...]).astype(o_ref.dtype)


def _ln_mod(x, g, sc, sh, *, tt):
    B, T, D = x.shape
    R = sc.shape[1]
    rr = tt if R == T else 1
    mod_map = (lambda b, i: (b, i, 0)) if R == T else (lambda b, i: (b, 0, 0))
    return pl.pallas_call(
        _ln_kernel,
        grid=(B, T // tt),
        in_specs=[pl.BlockSpec((None, tt, D), lambda b, i: (b, i, 0)),
                  pl.BlockSpec((1, D), lambda b, i: (0, 0)),
                  pl.BlockSpec((None, rr, D), mod_map),
                  pl.BlockSpec((None, rr, D), mod_map)],
        out_specs=pl.BlockSpec((None, tt, D), lambda b, i: (b, i, 0)),
        out_shape=jax.ShapeDtypeStruct((B, T, D), BF),
        compiler_params=_cparams(("parallel", "arbitrary")),
    )(x, g, sc, sh)


def _qk_kernel(z_ref, c_ref, s_ref, g_ref, q_ref, kv_ref, win_ref, kvb_ref, *, scale):
    cosf = c_ref[...]
    sinf = s_ref[...]

    def norm_rope(col, gi):
        x = z_ref[:, col:col + HEAD_DIM]
        ms = jnp.mean(x * x, axis=-1, keepdims=True)
        y = x * lax.rsqrt(ms + EPS) * g_ref[gi:gi + 1, :]
        return y * cosf + pltpu.roll(y, HEAD_DIM // 2, 1) * sinf

    nq = N_HEADS * HEAD_DIM
    grp = N_KV * HEAD_DIM
    for h in range(N_HEADS):
        q_ref[:, h * HEAD_DIM:(h + 1) * HEAD_DIM] = (norm_rope(h * HEAD_DIM, 0) * scale).astype(BF)
    for j in range(3):
        for gi in range(N_KV):
            kcol = nq + (2 * j) * grp + gi * HEAD_DIM
            vcol = nq + (2 * j + 1) * grp + gi * HEAD_DIM
            k = norm_rope(kcol, 1 + j)
            v = z_ref[:, vcol:vcol + HEAD_DIM]
            ko = (2 * j) * grp + gi * HEAD_DIM
            vo = (2 * j + 1) * grp + gi * HEAD_DIM
            kvb_ref[:, ko:ko + HEAD_DIM] = k.astype(BF)
            kvb_ref[:, vo:vo + HEAD_DIM] = v.astype(BF)
            tt = k.shape[0]
            if j < 2:
                planes = 4 * N_KV
                kv_ref[pl.ds(2 * j * N_KV + gi, tt, stride=planes), :] = k
                kv_ref[pl.ds((2 * j + 1) * N_KV + gi, tt, stride=planes), :] = v
            else:
                planes = 2 * N_KV
                win_ref[pl.ds(gi, tt, stride=planes), :] = k
                win_ref[pl.ds(N_KV + gi, tt, stride=planes), :] = v


def _qk_prep(z, cosf, sinf, qk_g, *, tt):
    B, T, _ = z.shape
    nq = N_HEADS * HEAD_DIM
    grp = N_KV * HEAD_DIM
    row = lambda b, i: (b, i, 0)
    return pl.pallas_call(
        functools.partial(_qk_kernel, scale=HEAD_DIM ** -0.5),
        grid=(B, T // tt),
        in_specs=[pl.BlockSpec((None, tt, z.shape[2]), row),
                  pl.BlockSpec((tt, HEAD_DIM), lambda b, i: (i, 0)),
                  pl.BlockSpec((tt, HEAD_DIM), lambda b, i: (i, 0)),
                  pl.BlockSpec((4, HEAD_DIM), lambda b, i: (0, 0))],
        out_specs=[pl.BlockSpec((None, tt, nq), row),
                   pl.BlockSpec((None, tt * 4 * N_KV, HEAD_DIM), row),
                   pl.BlockSpec((None, tt * 2 * N_KV, HEAD_DIM), row),
                   pl.BlockSpec((None, tt, 6 * grp), row)],
        out_shape=[jax.ShapeDtypeStruct((B, T, nq), BF),
                   jax.ShapeDtypeStruct((B, T * 4 * N_KV, HEAD_DIM), F32),
                   jax.ShapeDtypeStruct((B, T * 2 * N_KV, HEAD_DIM), F32),
                   jax.ShapeDtypeStruct((B, T, 6 * grp), BF)],
        compiler_params=_cparams(("parallel", "arbitrary")),
    )(z, cosf, sinf, qk_g)


def _rope_tables(pos0, T):
    half = HEAD_DIM // 2
    inv = ROPE_THETA ** (-jnp.arange(half, dtype=F32) / half)
    ang = (pos0 + jnp.arange(T, dtype=F32))[:, None] * inv[None, :]
    cos, sin = jnp.cos(ang), jnp.sin(ang)
    return jnp.concatenate([cos, cos], axis=1), jnp.concatenate([-sin, sin], axis=1)


def _compress_acc(load_pair, pe_ref, w_ref, rows):
    acc = jnp.zeros((rows, 2 * HEAD_DIM), F32)
    for lp in range(CMP_BLOCK // 2):
        xs = []
        for l in (2 * lp, 2 * lp + 1):
            x = load_pair(l) + pe_ref[l][None]
            xs.append(x.reshape(rows, HEAD_DIM).astype(BF))
        acc = acc + _dot(jnp.concatenate(xs, axis=1), w_ref[lp])
    is_k = (lax.broadcasted_iota(jnp.int32, (rows, HEAD_DIM), 0) % 8) < N_KV
    return jnp.where(is_k, acc[:, :HEAD_DIM], acc[:, HEAD_DIM:])


def _compress_kernel(x_ref, pe_ref, w_ref, o_ref, *, nblk):
    rows = nblk * 8
    out = _compress_acc(lambda l: x_ref[pl.ds(l, nblk, stride=CMP_BLOCK), :, :], pe_ref, w_ref, rows)
    o_ref[...] = out.reshape(nblk, 8, HEAD_DIM)


def _compress(kvrows, pe_t, w2):
    B = kvrows.shape[0]
    T = kvrows.shape[1] // (4 * N_KV)
    nblk = T // CMP_BLOCK
    x = kvrows.reshape(B, T, 4 * N_KV, HEAD_DIM)
    return pl.pallas_call(
        functools.partial(_compress_kernel, nblk=nblk),
        grid=(B,),
        in_specs=[pl.BlockSpec((None, T, 8, HEAD_DIM), lambda b: (b, 0, 0, 0)),
                  pl.BlockSpec((CMP_BLOCK, 8, HEAD_DIM), lambda b: (0, 0, 0)),
                  pl.BlockSpec((CMP_BLOCK // 2, 2 * HEAD_DIM, 2 * HEAD_DIM), lambda b: (0, 0, 0))],
        out_specs=pl.BlockSpec((None, nblk, 8, HEAD_DIM), lambda b: (b, 0, 0, 0)),
        out_shape=jax.ShapeDtypeStruct((B, nblk, 8, HEAD_DIM), F32),
        compiler_params=_cparams(("parallel",)),
    )(x, pe_t, w2)


def _compress_paged_kernel(pt_ref, *refs, pb):
    page_refs = refs[:pb]
    pe_ref, w_ref, o_ref = refs[pb:pb + 3]
    rows = pb * 2 * 8

    def load(l):
        tiles = []
        for p in range(pb):
            for h in range(2):
                tiles.append(page_refs[p][h * CMP_BLOCK + l])
        return jnp.stack(tiles, axis=0)

    out = _compress_acc(load, pe_ref, w_ref, rows)
    o_ref[...] = out.reshape(pb * 2, 8, HEAD_DIM)


def _compress_paged(cache5, page_table, layer, pe_t, w2, *, pb):
    n_phys, page = cache5.shape[1], cache5.shape[2]
    B, n_pages = page_table.shape
    assert page == 2 * CMP_BLOCK and n_pages % pb == 0

    def page_spec(p):
        return pl.BlockSpec((None, None, page, 8, HEAD_DIM),
                            lambda b, i, pt: (layer, pt[b, i * pb + p], 0, 0, 0))

    grid_spec = pltpu.PrefetchScalarGridSpec(
        num_scalar_prefetch=1,
        grid=(B, n_pages // pb),
        in_specs=[page_spec(p) for p in range(pb)] + [
            pl.BlockSpec((CMP_BLOCK, 8, HEAD_DIM), lambda b, i, pt: (0, 0, 0)),
            pl.BlockSpec((CMP_BLOCK // 2, 2 * HEAD_DIM, 2 * HEAD_DIM), lambda b, i, pt: (0, 0, 0))],
        out_specs=pl.BlockSpec((None, pb * 2, 8, HEAD_DIM), lambda b, i, pt: (b, i, 0, 0)),
    )
    return pl.pallas_call(
        functools.partial(_compress_paged_kernel, pb=pb),
        grid_spec=grid_spec,
        out_shape=jax.ShapeDtypeStruct((B, n_pages * 2, 8, HEAD_DIM), F32),
        compiler_params=_cparams(("parallel", "arbitrary")),
    )(page_table, *([cache5] * pb), pe_t, w2)


def _online_step(s2, v, m_ref, l_ref, acc_ref):
    tk = s2.shape[1]
    m_prev = m_ref[...]
    m_next = jnp.maximum(m_prev, jnp.max(s2, axis=1, keepdims=True))
    alpha = jnp.exp(m_prev - m_next)
    p = jnp.exp(s2 - _rep_lanes(m_next, tk))
    l_ref[...] = alpha * l_ref[...] + jnp.sum(p, axis=1, keepdims=True)
    m_ref[...] = m_next
    acc_ref[...] = alpha * acc_ref[...] + _dot(p.astype(BF), v)


def _nsa_kernel(q_ref, kc_ref, vc_ref, ks_ref, vs_ref, kw_ref, vw_ref, g_ref, o_ref,
                m_ref, l_ref, acc_ref, *, tq, nblk):
    qi = pl.program_id(2)
    q0 = qi * tq
    rows = HPG * tq
    q = q_ref[...]
    qcat = jnp.concatenate([q[:, h * HEAD_DIM:(h + 1) * HEAD_DIM] for h in range(HPG)], axis=0)

    kc = kc_ref[...].astype(BF)
    vc = vc_ref[...].astype(BF)
    t_b = q0 + lax.broadcasted_iota(jnp.int32, (nblk, tq), 1)
    blk = lax.broadcasted_iota(jnp.int32, (nblk, tq), 0)
    complete = jnp.concatenate([(blk + 1) * CMP_BLOCK - 1 <= t_b] * HPG, axis=1)
    s_t = jnp.where(complete, _dot_nt(kc, qcat), NEG)
    mx = jnp.max(s_t, axis=0, keepdims=True)
    e = jnp.where(complete, jnp.exp(s_t - mx), 0.0)
    d = jnp.sum(e, axis=0, keepdims=True)
    p_t = e / jnp.where(d > 0, d, 1.0)

    imp = p_t[:, 0:tq]
    for h in range(1, HPG):
        imp = imp + p_t[:, h * tq:(h + 1) * tq]
    cur = jnp.right_shift(t_b, CMP_SHIFT)
    forced = (blk == 0) | (blk == cur) | (blk == cur - 1)
    score = jnp.where(forced, FORCED_SCORE, imp)
    score = jnp.where(blk <= cur, score, -1.0)
    rank = jnp.zeros((nblk, tq), F32)
    for i in range(nblk):
        si = score[i:i + 1, :]
        beats = (si > score) | ((si == score) & (blk > i))
        rank = rank + jnp.where(beats, 1.0, 0.0)
    sel_t = jnp.where((rank < min(N_SELECT, nblk)) & (score > -0.5), 1.0, 0.0).astype(BF)

    eye = jnp.where(lax.broadcasted_iota(jnp.int32, (tq, tq), 0)
                    == lax.broadcasted_iota(jnp.int32, (tq, tq), 1), 1.0, 0.0).astype(BF)
    p_bf = p_t.astype(BF)
    p_rows = jnp.concatenate([_dot_nt(eye, p_bf[:, h * tq:(h + 1) * tq]) for h in range(HPG)], axis=0)
    o_cmp = _dot(p_rows.astype(BF), vc)
    sel = _dot_nt(eye, sel_t).astype(BF)

    t_k = q0 + lax.broadcasted_iota(jnp.int32, (tq, tq), 0)
    k_off = lax.broadcasted_iota(jnp.int32, (tq, tq), 1)
    e_blk = lax.broadcasted_iota(jnp.int32, (nblk, tq), 0)
    e_off = lax.broadcasted_iota(jnp.int32, (nblk, tq), 1)

    def reset():
        m_ref[...] = jnp.full((rows, LANES), NEG, F32)
        l_ref[...] = jnp.zeros((rows, LANES), F32)
        acc_ref[...] = jnp.zeros((rows, HEAD_DIM), F32)

    def attend(c, k_ref, v_ref, mask):
        start = pl.multiple_of(c * tq, tq)
        k = k_ref[pl.ds(start, tq), :]
        v = v_ref[pl.ds(start, tq), :]
        s = jnp.where(mask[None], _dot_nt(qcat, k).reshape(HPG, tq, tq), NEG)
        _online_step(s.reshape(rows, tq), v, m_ref, l_ref, acc_ref)

    def sel_body(c, carry):
        expand = jnp.where(jnp.right_shift(c * tq + e_off, CMP_SHIFT) == e_blk, 1.0, 0.0).astype(BF)
        chosen = _dot(sel, expand) > 0.5
        attend(c, ks_ref, vs_ref, chosen & (c * tq + k_off <= t_k))
        return carry

    reset()
    lax.fori_loop(0, qi + 1, sel_body, 0)
    o_sel = acc_ref[...] / l_ref[...]

    def win_body(c, carry):
        dist = t_k - (c * tq + k_off)
        attend(c, kw_ref, vw_ref, (dist >= 0) & (dist < WINDOW))
        return carry

    reset()
    lax.fori_loop(jnp.maximum(qi - WINDOW // tq, 0), qi + 1, win_body, 0)
    o_win = acc_ref[...] / l_ref[...]

    g = jax.nn.sigmoid(g_ref[...])
    for h in range(HPG):
        r = slice(h * tq, (h + 1) * tq)
        o = (g[:, 3 * h:3 * h + 1] * o_cmp[r] + g[:, 3 * h + 1:3 * h + 2] * o_sel[r]
             + g[:, 3 * h + 2:3 * h + 3] * o_win[r])
        o_ref[:, h * HEAD_DIM:(h + 1) * HEAD_DIM] = o.astype(BF)


def _nsa_prompt(q, kcv, kvb, zg, *, tq):
    B, T, _ = q.shape
    nblk = kcv.shape[1]
    col = lambda c: pl.BlockSpec((None, T, HEAD_DIM), lambda b, g, i, c=c: (b, 0, c + g))
    rows = HPG * tq
    return pl.pallas_call(
        functools.partial(_nsa_kernel, tq=tq, nblk=nblk),
        grid=(B, N_KV, T // tq),
        in_specs=[pl.BlockSpec((None, tq, HPG * HEAD_DIM), lambda b, g, i: (b, i, g)),
                  pl.BlockSpec((None, nblk, HEAD_DIM), lambda b, g, i: (b, 0, g)),
                  pl.BlockSpec((None, nblk, HEAD_DIM), lambda b, g, i: (b, 0, N_KV + g)),
                  col(2 * N_KV), col(3 * N_KV), col(4 * N_KV), col(5 * N_KV),
                  pl.BlockSpec((None, tq, LANES), lambda b, g, i: (b, i, g))],
        out_specs=pl.BlockSpec((None, tq, HPG * HEAD_DIM), lambda b, g, i: (b, i, g)),
        out_shape=jax.ShapeDtypeStruct((B, T, N_HEADS * HEAD_DIM), BF),
        scratch_shapes=[pltpu.VMEM((rows, LANES), F32), pltpu.VMEM((rows, LANES), F32),
                        pltpu.VMEM((rows, HEAD_DIM), F32)],
        compiler_params=_cparams(("parallel", "parallel", "arbitrary")),
    )(q, kcv, kcv, kvb, kvb, kvb, kvb, zg)


def _softplus(y):
    return jnp.maximum(y, 0.0) + jnp.log1p(jnp.exp(-jnp.abs(y)))


def _rg_gates(u, w_ref, brg_ref, lam_ref):
    ub = u.astype(BF)
    r = jax.nn.sigmoid(_dot(ub, w_ref[0].astype(BF)) + brg_ref[0:1, :])
    i = jax.nn.sigmoid(_dot(ub, w_ref[1].astype(BF)) + brg_ref[1:2, :])
    log_a = -RG_C * r * _softplus(-lam_ref[...])
    a = jnp.exp(log_a)
    th = jnp.tanh(log_a)
    b = jnp.sqrt(-2.0 * th / (1.0 - th)) * i * u
    return a, b


def _rglru_kernel(xr_ref, zg_ref, c0_ref, h0_ref, cw_ref, cb_ref, w_ref, brg_ref, lam_ref,
                  y_ref, hl_ref, cl_ref, xp_ref, a_ref, b_ref, h_ref, *, T):
    pad = 8
    xp_ref[0:pad, :] = jnp.zeros((pad, RNN_BW), F32)
    xp_ref[pad - (CONV_W - 1):pad, :] = c0_ref[...]
    xp_ref[pad:pad + T, :] = xr_ref[...]
    u = cb_ref[...] + xp_ref[pad:pad + T, :] * cw_ref[CONV_W - 1:CONV_W, :]
    for k in range(CONV_W - 1):
        off = pad - (CONV_W - 1) + k
        u = u + xp_ref[off:off + T, :] * cw_ref[k:k + 1, :]
    cl_ref[...] = xp_ref[pad + T - (CONV_W - 1):pad + T, :]

    a, b = _rg_gates(u, w_ref, brg_ref, lam_ref)
    a_ref[0:T, :] = a
    b_ref[0:T, :] = b

    levels = []
    n, off = T, 0
    while True:
        levels.append((n, off))
        if n <= 8:
            break
        off += n
        n //= 8
    for (n, off), (n2, off2) in zip(levels[:-1], levels[1:]):
        A = a_ref[pl.ds(off, n2, stride=8), :]
        Bv = b_ref[pl.ds(off, n2, stride=8), :]
        for r in range(1, 8):
            ar = a_ref[pl.ds(off + r, n2, stride=8), :]
            br = b_ref[pl.ds(off + r, n2, stride=8), :]
            Bv = ar * Bv + br
            A = ar * A
        a_ref[off2:off2 + n2, :] = A
        b_ref[off2:off2 + n2, :] = Bv
    n, off = levels[-1]
    h = h0_ref[...]
    for t in range(n):
        h = a_ref[off + t:off + t + 1, :] * h + b_ref[off + t:off + t + 1, :]
        h_ref[off + 8 + t:off + 9 + t, :] = h
    for (n, off), (n2, off2) in reversed(list(zip(levels[:-1], levels[1:]))):
        h_ref[off2 + 7:off2 + 8, :] = h0_ref[...]
        hp = h_ref[off2 + 7:off2 + 7 + n2, :]
        for r in range(8):
            ar = a_ref[pl.ds(off + r, n2, stride=8), :]
            br = b_ref[pl.ds(off + r, n2, stride=8), :]
            hp = ar * hp + br
            h_ref[pl.ds(off + 8 + r, n2, stride=8), :] = hp
    hfull = h_ref[8:8 + T, :]
    y_ref[...] = (jax.nn.gelu(zg_ref[...]) * hfull).astype(BF)
    hl_ref[...] = h_ref[8 + T - 1:8 + T, :]


def _rglru(zr, conv0, h0, conv_w, conv_b, w_rg, b_rg, lam, *, d_rnn):
    B, T, _ = zr.shape
    nb = d_rnn // RNN_BW
    assert T % 8 == 0
    total = T + T // 4 + 64
    return pl.pallas_call(
        functools.partial(_rglru_kernel, T=T),
        grid=(B, nb),
        in_specs=[pl.BlockSpec((None, T, RNN_BW), lambda b, n: (b, 0, n)),
                  pl.BlockSpec((None, T, RNN_BW), lambda b, n: (b, 0, nb + n)),
                  pl.BlockSpec((None, CONV_W - 1, RNN_BW), lambda b, n: (b, 0, n)),
                  pl.BlockSpec((None, 1, RNN_BW), lambda b, n: (b, 0, n)),
                  pl.BlockSpec((CONV_W, RNN_BW), lambda b, n: (0, n)),
                  pl.BlockSpec((1, RNN_BW), lambda b, n: (0, n)),
                  pl.BlockSpec((2, None, RNN_BW, RNN_BW), lambda b, n: (0, n, 0, 0)),
                  pl.BlockSpec((2, RNN_BW), lambda b, n: (0, n)),
                  pl.BlockSpec((1, RNN_BW), lambda b, n: (0, n))],
        out_specs=[pl.BlockSpec((None, T, RNN_BW), lambda b, n: (b, 0, n)),
                   pl.BlockSpec((None, 1, RNN_BW), lambda b, n: (b, 0, n)),
                   pl.BlockSpec((None, CONV_W - 1, RNN_BW), lambda b, n: (b, 0, n))],
        out_shape=[jax.ShapeDtypeStruct((B, T, d_rnn), BF),
                   jax.ShapeDtypeStruct((B, 1, d_rnn), F32),
                   jax.ShapeDtypeStruct((B, CONV_W - 1, d_rnn), F32)],
        scratch_shapes=[pltpu.VMEM((T + 8, RNN_BW), F32), pltpu.VMEM((total, RNN_BW), F32),
                        pltpu.VMEM((total, RNN_BW), F32), pltpu.VMEM((total + 8, RNN_BW), F32)],
        compiler_params=_cparams(("parallel", "arbitrary")),
    )(zr, zr, conv0, h0, conv_w, conv_b[None], w_rg, b_rg, lam[None])


def _rglru_step_kernel(xr_ref, zg_ref, c0_ref, h0_ref, cw_ref, cb_ref, w_ref, brg_ref, lam_ref,
                       y_ref, hl_ref):
    u = cb_ref[...] + xr_ref[...] * cw_ref[CONV_W - 1:CONV_W, :]
    for k in range(CONV_W - 1):
        u = u + c0_ref[k] * cw_ref[k:k + 1, :]
    a, b = _rg_gates(u, w_ref, brg_ref, lam_ref)
    h = a * h0_ref[...] + b
    hl_ref[...] = h
    y_ref[...] = (jax.nn.gelu(zg_ref[...]) * h).astype(BF)


def _rglru_step(zr, conv0_t, h0, conv_w, conv_b, w_rg, b_rg, lam, *, d_rnn):
    R = zr.shape[0]
    nb = d_rnn // RNN_BW
    return pl.pallas_call(
        _rglru_step_kernel,
        grid=(nb,),
        in_specs=[pl.BlockSpec((R, RNN_BW), lambda n: (0, n)),
                  pl.BlockSpec((R, RNN_BW), lambda n: (0, nb + n)),
                  pl.BlockSpec((CONV_W - 1, R, RNN_BW), lambda n: (0, 0, n)),
                  pl.BlockSpec((R, RNN_BW), lambda n: (0, n)),
                  pl.BlockSpec((CONV_W, RNN_BW), lambda n: (0, n)),
                  pl.BlockSpec((1, RNN_BW), lambda n: (0, n)),
                  pl.BlockSpec((2, None, RNN_BW, RNN_BW), lambda n: (0, n, 0, 0)),
                  pl.BlockSpec((2, RNN_BW), lambda n: (0, n)),
                  pl.BlockSpec((1, RNN_BW), lambda n: (0, n))],
        out_specs=[pl.BlockSpec((R, RNN_BW), lambda n: (0, n)),
                   pl.BlockSpec((R, RNN_BW), lambda n: (0, n))],
        out_shape=[jax.ShapeDtypeStruct((R, d_rnn), BF), jax.ShapeDtypeStruct((R, d_rnn), F32)],
        compiler_params=_cparams(("arbitrary",)),
    )(zr, zr, conv0_t, h0, conv_w, conv_b[None], w_rg, b_rg, lam[None])


def _dec_cmp_kernel(q_ref, kcv_ref, o_ref, idx_ref, *, n_past, t_pos):
    nlan = LANES * ((n_past + 1 + LANES - 1) // LANES)
    blk = lax.broadcasted_iota(jnp.int32, (1, n_past), 1)
    complete = (blk + 1) * CMP_BLOCK - 1 <= t_pos
    imps = []
    for g in range(N_KV):
        kc = kcv_ref[:, g * HEAD_DIM:(g + 1) * HEAD_DIM].astype(BF)
        vc = kcv_ref[:, (N_KV + g) * HEAD_DIM:(N_KV + g + 1) * HEAD_DIM].astype(BF)
        s = jnp.where(complete, _dot_nt(q_ref[g], kc), NEG)
        mx = jnp.max(s, axis=-1, keepdims=True)
        e = jnp.where(complete, jnp.exp(s - mx), 0.0)
        d = jnp.sum(e, axis=-1, keepdims=True)
        p = e / jnp.where(d > 0, d, 1.0)
        o_ref[g] = _dot(p.astype(BF), vc)
        head = lax.broadcasted_iota(jnp.int32, p.shape, 0) < HPG
        imps.append(jnp.sum(jnp.where(head, p, 0.0), axis=0, keepdims=True))
    imp = jnp.concatenate(imps, axis=0)

    cur = t_pos // CMP_BLOCK
    imp_all = jnp.concatenate([imp, jnp.zeros((N_KV, nlan - n_past), F32)], axis=1)
    blk_all = lax.broadcasted_iota(jnp.int32, (N_KV, nlan), 1)
    forced = (blk_all == 0) | (blk_all == cur) | (blk_all == cur - 1)
    score = jnp.where(forced, FORCED_SCORE, imp_all)
    score = jnp.where(blk_all <= cur, score, -1.0)
    score = jnp.where(blk_all <= n_past, score, -2.0)
    blk_f = blk_all.astype(F32)
    slot = lax.broadcasted_iota(jnp.int32, (N_KV, LANES), 1)
    res = jnp.full((N_KV, LANES), float(n_past), F32)
    for k in range(min(N_SELECT, n_past + 1)):
        top = jnp.max(score, axis=-1, keepdims=True)
        arg = jnp.min(jnp.where(score == top, blk_f, float(nlan)), axis=-1, keepdims=True)
        res = jnp.where(slot == k, jnp.where(top > -0.5, arg, float(n_past)), res)
        score = jnp.where(blk_f == arg, -3.0, score)
    idx_ref[...] = res.astype(jnp.int32)


def _dec_cmp(q16, kcv, *, t_pos):
    B = q16.shape[0]
    n_past = kcv.shape[1]
    return pl.pallas_call(
        functools.partial(_dec_cmp_kernel, n_past=n_past, t_pos=t_pos),
        grid=(B,),
        in_specs=[pl.BlockSpec((None, N_KV, 16, HEAD_DIM), lambda b: (b, 0, 0, 0)),
                  pl.BlockSpec((None, n_past, 2 * N_KV * HEAD_DIM), lambda b: (b, 0, 0))],
        out_specs=[pl.BlockSpec((None, N_KV, 16, HEAD_DIM), lambda b: (b, 0, 0, 0)),
                   pl.BlockSpec((None, N_KV, LANES), lambda b: (b, 0, 0))],
        out_shape=[jax.ShapeDtypeStruct((B, N_KV, 16, HEAD_DIM), F32),
                   jax.ShapeDtypeStruct((B, N_KV, LANES), jnp.int32)],
        compiler_params=_cparams(("parallel",)),
    )(q16, kcv)


def _dec_attn_kernel(pt_ref, ti_ref, q_ref, c0_ref, c1_ref, c2_ref, c3_ref, new_ref, win_ref, oc_ref, g_ref,
                     o_ref, m_ref, l_ref, acc_ref, *, n_past):
    b, s = pl.program_id(0), pl.program_id(1)
    cache_refs = (c0_ref, c1_ref, c2_ref, c3_ref)
    planes = 4 * N_KV

    @pl.when(s == 0)
    def _():
        m_ref[...] = jnp.full(m_ref.shape, NEG, F32)
        l_ref[...] = jnp.zeros(l_ref.shape, F32)
        acc_ref[...] = jnp.zeros(acc_ref.shape, F32)

    for g in range(N_KV):
        live = ti_ref[b * N_KV + g, s] < n_past
        k = cache_refs[g][pl.ds(2 * N_KV + g, CMP_BLOCK, stride=planes), :].astype(BF)
        v = cache_refs[g][pl.ds(3 * N_KV + g, CMP_BLOCK, stride=planes), :].astype(BF)
        sc = jnp.where(live, _dot_nt(q_ref[g], k), NEG)
        m_prev = m_ref[g]
        m_next = jnp.maximum(m_prev, jnp.max(sc, axis=1, keepdims=True))
        alpha = jnp.exp(m_prev - m_next)
        p = jnp.where(live, jnp.exp(sc - m_next[:, :CMP_BLOCK]), 0.0)
        l_ref[g] = alpha * l_ref[g] + jnp.sum(p, axis=1, keepdims=True)
        m_ref[g] = m_next
        acc_ref[g] = alpha * acc_ref[g] + _dot(p.astype(BF), v)

    @pl.when(s == pl.num_programs(1) - 1)
    def _():
        n_win = win_ref.shape[0] // (2 * N_KV)
        keep = lax.broadcasted_iota(jnp.int32, (1, n_win), 1) >= n_win + 1 - WINDOW
        for g in range(N_KV):
            q = q_ref[g]
            qf = q.astype(F32)

            def self_score(k_row):
                return jnp.sum(qf * k_row.astype(BF).astype(F32), axis=1, keepdims=True)

            s_self = self_score(new_ref[g, 2:3, :])
            v_self = new_ref[g, 3:4, :].astype(BF).astype(F32)
            m_prev = m_ref[g]
            m_fin = jnp.maximum(m_prev, s_self)
            alpha = jnp.exp(m_prev - m_fin)
            p_self = jnp.exp(s_self - m_fin)
            o_sel = (alpha * acc_ref[g] + p_self * v_self) / (alpha * l_ref[g] + p_self)

            kw = win_ref[pl.ds(g, n_win, stride=2 * N_KV), :].astype(BF)
            vw = win_ref[pl.ds(N_KV + g, n_win, stride=2 * N_KV), :].astype(BF)
            sw = jnp.where(keep, _dot_nt(q, kw), NEG)
            sw_self = self_score(new_ref[g, 4:5, :])
            mw = jnp.maximum(jnp.max(sw, axis=1, keepdims=True), sw_self)
            pw = jnp.where(keep, jnp.exp(sw - mw), 0.0)
            pw_self = jnp.exp(sw_self - mw)
            num = _dot(pw.astype(BF), vw) + pw_self * new_ref[g, 5:6, :].astype(BF).astype(F32)
            o_win = num / (jnp.sum(pw, axis=1, keepdims=True) + pw_self)

            gt = jax.nn.sigmoid(g_ref[g])
            o_ref[g] = gt[:, 0:1] * oc_ref[g] + gt[:, 1:2] * o_sel + gt[:, 2:3] * o_win


def _dec_attn(q16, cache4, page_table, top_i, newkv, win3, o_cmp, g16, *, layer, n_past):
    B = q16.shape[0]
    n_slots = top_i.shape[1]
    half_rows = CMP_BLOCK * 4 * N_KV

    def half_page(g):
        def index(b, s, pt, ti):
            blk = jnp.minimum(ti[b * N_KV + g, s], n_past - 1)
            return (layer, pt[b, blk // 2], blk % 2, 0)
        return pl.BlockSpec((None, None, half_rows, HEAD_DIM), index)

    per_seq = lambda b, s, pt, ti: (b, 0, 0, 0)
    grid_spec = pltpu.PrefetchScalarGridSpec(
        num_scalar_prefetch=2,
        grid=(B, n_slots),
        in_specs=[pl.BlockSpec((None, N_KV, 16, HEAD_DIM), per_seq)]
        + [half_page(g) for g in range(N_KV)]
        + [pl.BlockSpec((None, N_KV, 6, HEAD_DIM), per_seq),
           pl.BlockSpec((None, win3.shape[1], HEAD_DIM), lambda b, s, pt, ti: (layer * B + b, 0, 0)),
           pl.BlockSpec((None, N_KV, 16, HEAD_DIM), per_seq),
           pl.BlockSpec((None, N_KV, 16, LANES), per_seq)],
        out_specs=pl.BlockSpec((None, N_KV, 16, HEAD_DIM), per_seq),
        scratch_shapes=[pltpu.VMEM((N_KV, 16, LANES), F32), pltpu.VMEM((N_KV, 16, LANES), F32),
                        pltpu.VMEM((N_KV, 16, HEAD_DIM), F32)],
    )
    return pl.pallas_call(
        functools.partial(_dec_attn_kernel, n_past=n_past),
        grid_spec=grid_spec,
        out_shape=jax.ShapeDtypeStruct((B, N_KV, 16, HEAD_DIM), F32),
        compiler_params=_cparams(("parallel", "arbitrary")),
    )(page_table, top_i, q16, *([cache4] * N_KV), newkv, win3, o_cmp, g16)


def _regroup_w_in(w_in):
    L, D, _ = w_in.shape
    o_gn = N_HEADS * HEAD_DIM + 6 * N_KV * HEAD_DIM
    o_rx = o_gn + 3 * N_HEADS
    wg = w_in[:, :, o_gn:o_rx].reshape(L, D, N_KV, 3 * HPG)
    wg = jnp.pad(wg, ((0, 0), (0, 0), (0, 0), (0, LANES - 3 * HPG))).reshape(L, D, N_KV * LANES)
    return w_in[:, :, :o_gn].astype(BF), wg.astype(BF), w_in[:, :, o_rx:].astype(BF)


def _row_tile(m):
    return 1024 if m % 1024 == 0 else 256


def _largest_divisor(n, cap):
    return max(d for d in range(1, cap + 1) if n % d == 0)


def _layer(l, xp, xs, mod_p, mod_s, P, rope_p, rope_s, cache_kv, page_table, state_win, state_rnn, state_conv):
    B, T, D = xp.shape
    M = B * T
    R = xs.shape[0]
    nb, n_pages = page_table.shape
    depth, n_phys, page = cache_kv.shape[:3]
    n_past = n_pages * page // CMP_BLOCK
    d_attn = N_HEADS * HEAD_DIM
    d_rnn = P["conv_w"].shape[2]
    o_gn = d_attn + 6 * N_KV * HEAD_DIM
    o_rx = o_gn + 3 * N_HEADS
    tm, tn = _row_tile(M), 512
    shp1, scp1, gap1, shp2, scp2, gap2 = mod_p
    shs1, scs1, gas1, shs2, scs2, gas2 = mod_s
    bmap = lambda j, i: (i // (T // tm), 0, j)
    g_ln = P["g_ln"][l]

    w2 = jnp.concatenate([P["w_phi"][l, 0], P["w_phi"][l, 1]], axis=-1)
    w2 = w2.reshape(CMP_BLOCK // 2, 2 * HEAD_DIM, 2 * HEAD_DIM).astype(BF)
    pe_t = jnp.concatenate([jnp.broadcast_to(P["pe_cmp"][l, j][:, None, :], (CMP_BLOCK, N_KV, HEAD_DIM))
                            for j in range(2)], axis=1)
    rnn = (P["conv_w"][l], P["conv_b"][l], P["w_rg"][l], P["b_rg"][l], P["lam"][l])

    hp = _ln_mod(xp, g_ln[0:1], scp1, shp1, tt=256).reshape(M, D)
    hs = _ln_mod(xs[None], g_ln[0:1], scs1, shs1, tt=R).reshape(R, D)
    w_qkv, w_gate, w_rmg = P["w_in_bf"]
    zq_p, zq_s = _mm2(hp, hs, w_qkv, layer=l, tm=tm, tn=tn)
    zg_p, zg_s = _mm2(hp, hs, w_gate, layer=l, tm=tm, tn=tn)
    zr_p, zr_s, w_out_bf, w_mlp1_bf, w_mlp2_bf = _mm2(
        hp, hs, w_rmg, layer=l, tm=tm, tn=tn, side_cast=(P["w_out"], P["w_mlp1"], P["w_mlp2"]))

    q, kvrows, winrows, kvb = _qk_prep(zq_p.reshape(B, T, -1), rope_p[0], rope_p[1], P["qk_g"][l], tt=256)
    kcv = _compress(kvrows, pe_t, w2)
    o_nsa_p = _nsa_prompt(q, kcv.reshape(B, T // CMP_BLOCK, 8 * HEAD_DIM), kvb, zg_p.reshape(B, T, -1), tq=256)
    conv0 = jnp.zeros((B, CONV_W - 1, d_rnn), F32)
    h0 = jnp.zeros((B, 1, d_rnn), F32)
    y_rnn_p, h_last, conv_last = _rglru(zr_p.reshape(B, T, -1), conv0, h0, *rnn, d_rnn=d_rnn)

    qs, kvrows_s, winrows_s, _ = _qk_prep(zq_s[None], rope_s[0], rope_s[1], P["qk_g"][l], tt=R)
    q16 = jnp.pad(qs[0, :nb].reshape(nb, N_KV, HPG, HEAD_DIM), ((0, 0), (0, 0), (0, 16 - HPG), (0, 0)))
    g16 = jnp.pad(zg_s[:nb].reshape(nb, N_KV, LANES)[:, :, :3 * HPG].reshape(nb, N_KV, HPG, 3),
                  ((0, 0), (0, 0), (0, 16 - HPG), (0, LANES - 3)))
    kv_new = kvrows_s.reshape(R, 4, N_KV, HEAD_DIM)[:nb]
    win_new = winrows_s.reshape(R, 2, N_KV, HEAD_DIM)[:nb]
    new6 = jnp.concatenate([kv_new, win_new], axis=1).transpose(0, 2, 1, 3)
    cache5 = cache_kv.reshape(depth, n_phys, page, 4 * N_KV, HEAD_DIM)
    kcv_s = _compress_paged(cache5, page_table, l, pe_t, w2, pb=_largest_divisor(n_pages, 32))
    o_cmp, top = _dec_cmp(q16, kcv_s.reshape(nb, n_past, 8 * HEAD_DIM), t_pos=n_pages * page)
    top_i = top[:, :, :N_SELECT].reshape(nb * N_KV, N_SELECT)
    cache4 = cache_kv.reshape(depth, n_phys, page * 4 * N_KV, HEAD_DIM)
    n_win = state_win.shape[2]
    win3 = state_win.reshape(depth * nb, n_win * 2 * N_KV, HEAD_DIM)
    o16 = _dec_attn(q16, cache4, page_table, top_i, new6, win3, o_cmp, g16, layer=l, n_past=n_past)
    o_nsa_s = jnp.pad(o16[:, :, :HPG].reshape(nb, -1), ((0, R - nb), (0, 0))).astype(BF)
    pad_r = lambda a: jnp.pad(a, ((0, R - nb),) + ((0, 0),) * (a.ndim - 1))
    conv0_t = pad_r(state_conv[l]).transpose(1, 0, 2)
    y_rnn_s, h_new = _rglru_step(zr_s, conv0_t, pad_r(state_rnn[l]), *rnn, d_rnn=d_rnn)

    mg_p, mg_s = _merge2(o_nsa_p.reshape(M, -1), y_rnn_p.reshape(M, -1), zr_p, o_nsa_s, y_rnn_s, zr_s,
                         P["w_branch_nsa"], P["w_branch_rnn"], layer=l, col0=2 * d_rnn, tm=tm, tn=tn)
    x1p, x1s = _mm2(mg_p, mg_s, w_out_bf, layer=0, tm=tm, tn=tn, epi="resid",
                    res=(xp.reshape(M, D), gap1, bmap, xs, gas1))
    h2p = _ln_mod(x1p.reshape(B, T, D), g_ln[1:2], scp2, shp2, tt=256).reshape(M, D)
    h2s = _ln_mod(x1s[None], g_ln[1:2], scs2, shs2, tt=R).reshape(R, D)
    up, us = _mm2(h2p, h2s, w_mlp1_bf, layer=0, tm=tm, tn=1024, out_dtype=BF, epi="relu2")
    x2p = _mm(up, w_mlp2_bf, layer=0, tm=tm, tn=tn, tk=4096, epi="resid", x_res=x1p, ga=gap2,
              ga_map=lambda i, j, k: (i // (T // tm), 0, j))
    x2s = _mm(us, w_mlp2_bf, layer=0, tm=R, tn=1024, tk=4096, epi="resid", x_res=x1s, ga=gas2,
              ga_map=lambda i, j, k: (0, 0, j))

    n_keep = min(WINDOW, T)
    outs_p = (kvrows.reshape(B, T, 4, N_KV, HEAD_DIM),
              winrows.reshape(B, T, 2, N_KV, HEAD_DIM)[:, T - n_keep:],
              h_last.reshape(B, -1), conv_last)
    outs_s = (kv_new[:, None],
              jnp.concatenate([state_win[l][:, 1:], win_new[:, None]], axis=1),
              h_new[:nb],
              jnp.concatenate([state_conv[l][:, 1:], zr_s[:nb, None, :d_rnn]], axis=1))
    return x2p.reshape(B, T, D), x2s, outs_p, outs_s


def kernel(x_prompt, x_sample, cache_kv, state_win, state_rnn, state_conv, page_table, c_prompt, c_sample,
           w_ada, b_ada, g_ln, w_in, qk_g, w_phi, pe_cmp, conv_w, conv_b, w_rg, b_rg, lam,
           w_branch_nsa, w_branch_rnn, w_out, w_mlp1, w_mlp2):
    depth = w_ada.shape[0]
    B, T, D = x_prompt.shape
    nb = x_sample.shape[0]
    R = 16
    past_len = page_table.shape[1] * cache_kv.shape[2]

    c_all = jnp.zeros((2 * R, D), F32).at[:B].set(c_prompt).at[R:R + nb].set(c_sample)
    rope_p = _rope_tables(0.0, T)
    rope_s = tuple(jnp.broadcast_to(t, (R, HEAD_DIM)) for t in _rope_tables(float(past_len), 1))

    P = dict(g_ln=g_ln, w_in_bf=_regroup_w_in(w_in), qk_g=qk_g, w_phi=w_phi, pe_cmp=pe_cmp, conv_w=conv_w, conv_b=conv_b,
             w_rg=w_rg, b_rg=b_rg, lam=lam, w_branch_nsa=w_branch_nsa, w_branch_rnn=w_branch_rnn,
             w_out=w_out, w_mlp1=w_mlp1, w_mlp2=w_mlp2)
    yp = x_prompt
    ys = jnp.pad(x_sample.reshape(nb, D), ((0, R - nb), (0, 0)))
    outs = [[] for _ in range(8)]
    for l in range(depth):
        mod = _mm(c_all, w_ada, layer=l, tm=2 * R, tn=512, epi="silu_bias", bias=b_ada[l][None])
        mod = mod.reshape(2 * R, 6, D)
        mod_p = [mod[:B, k][:, None, :] for k in range(6)]
        mod_s = [mod[R:, k][None] for k in range(6)]
        yp, ys, (a1, a2, a3, a4), (b1, b2, b3, b4) = _layer(
            l, yp, ys, mod_p, mod_s, P, rope_p, rope_s, cache_kv, page_table, state_win, state_rnn, state_conv)
        for lst, v in zip(outs, (a1, b1, a2, b2, a3, b3, a4, b4)):
            lst.append(v)
    kv_p, kv_s, win_p, win_s, h_p, h_s, cv_p, cv_s = (jnp.stack(v) for v in outs)
    return (yp, ys[:nb].reshape(nb, 1, D), kv_p, kv_s, win_p, win_s, h_p, h_s, cv_p, cv_s)
```

```python
import functools
import math

import jax
import jax.numpy as jnp
from jax import lax
from jax.experimental import pallas as pl
from jax.experimental.pallas import tpu as pltpu

BF = jnp.bfloat16
F32 = jnp.float32

HEAD_DIM = 128
N_KV = 4
HPG = 4
N_HEADS = N_KV * HPG
CMP_BLOCK = 64
CMP_SHIFT = 6
N_SELECT = 16
WINDOW = 512
FORCED_SCORE = 1e4
RNN_BW = 128
CONV_W = 4
RG_C = 8.0
ROPE_THETA = 10000.0
EPS = 1e-6
NEG = -1e30
LANES = 128
VMEM_LIMIT = 56 * 1024 * 1024


def _cparams(sem):
    return pltpu.CompilerParams(dimension_semantics=sem, vmem_limit_bytes=VMEM_LIMIT)


def _dot(a, b):
    return jnp.dot(a, b, preferred_element_type=F32)


def _dot_nt(a, b):
    return lax.dot_general(a, b, (((1,), (1,)), ((), ())), preferred_element_type=F32)


def _rep_lanes(x, n):
    return x if n == LANES else jnp.concatenate([x] * (n // LANES), axis=1)


def _mm_kernel(*refs, nk, epi):
    if epi == "resid":
        a_ref, w_ref, x_ref, ga_ref, o_ref = refs[:5]
        scratch = refs[5:]
    elif epi == "silu_bias":
        a_ref, w_ref, bias_ref, o_ref = refs[:4]
        scratch = refs[4:]
    else:
        a_ref, w_ref, o_ref = refs[:3]
        scratch = refs[3:]

    def finish(acc):
        if epi == "relu2":
            r = jnp.maximum(acc, 0.0)
            acc = r * r
        elif epi == "resid":
            acc = x_ref[...] + ga_ref[...] * acc
        elif epi == "silu_bias":
            acc = acc + bias_ref[...]
        o_ref[...] = acc.astype(o_ref.dtype)

    a = a_ref[...]
    if epi == "silu_bias":
        a = (a * jax.nn.sigmoid(a)).astype(BF)
    part = _dot(a, w_ref[...].astype(BF))
    if nk == 1:
        finish(part)
    else:
        acc_ref = scratch[0]
        k = pl.program_id(2)

        @pl.when(k == 0)
        def _():
            acc_ref[...] = part

        @pl.when(k > 0)
        def _():
            acc_ref[...] += part

        @pl.when(k == nk - 1)
        def _():
            finish(acc_ref[...])


def _mm(a, w, *, tm, tn, tk=None, out_dtype=F32, epi=None, x_res=None, ga=None, ga_map=None, bias=None,
        layer=0):
    M, K = a.shape
    N = w.shape[2]
    tk = K if tk is None else tk
    nk = K // tk
    grid = (M // tm, N // tn, nk)
    in_specs = [pl.BlockSpec((tm, tk), lambda i, j, k: (i, k)),
                pl.BlockSpec((None, tk, tn), lambda i, j, k: (layer, k, j))]
    args = [a, w]
    if epi == "resid":
        in_specs.append(pl.BlockSpec((tm, tn), lambda i, j, k: (i, j)))
        in_specs.append(pl.BlockSpec((None, ga.shape[1], tn), ga_map))
        args += [x_res, ga]
    elif epi == "silu_bias":
        in_specs.append(pl.BlockSpec((1, tn), lambda i, j, k: (0, j)))
        args.append(bias)
    scratch =[pltpu.VMEM((tm, tn), F32)] if nk > 1 else []
    return pl.pallas_call(
        functools.partial(_mm_kernel, nk=nk, epi=epi),
        grid=grid, in_specs=in_specs,
        out_specs=pl.BlockSpec((tm, tn), lambda i, j, k: (i, j)),
        out_shape=jax.ShapeDtypeStruct((M, N), out_dtype),
        scratch_shapes=scratch,
        compiler_params=_cparams(("parallel", "parallel", "arbitrary")),
    )(*args)


def _cached_bf16(w_ref, wbf_ref, first):
    if wbf_ref is None:
        return w_ref[...]

    @pl.when(first)
    def _():
        wbf_ref[...] = w_ref[...].astype(BF)

    return wbf_ref[...]


def _mm2_kernel(*refs, epi, cast_w, w_transposed):
    n_in = 7 if epi == "resid" else 3
    a_ref, a2_ref, w_ref = refs[:3]
    o_ref, o2_ref = refs[n_in:n_in + 2]
    wbf_ref = refs[n_in + 2] if cast_w else None
    first = pl.program_id(1) == 0
    w = _cached_bf16(w_ref, wbf_ref, first)
    mul = _dot_nt if w_transposed else _dot

    def finish(acc, x_ref, ga_ref, out_ref):
        if epi == "relu2":
            r = jnp.maximum(acc, 0.0)
            acc = r * r
        elif epi == "resid":
            acc = x_ref[...] + ga_ref[...] * acc
        out_ref[...] = acc.astype(out_ref.dtype)

    extra = refs[3:7] if epi == "resid" else (None,) * 4
    finish(mul(a_ref[...], w), extra[0], extra[1], o_ref)

    @pl.when(first)
    def _():
        finish(mul(a2_ref[...], w), extra[2], extra[3], o2_ref)


def _mm2(a, a2, w, *, layer, tm, tn, out_dtype=F32, epi=None, res=None, w_rows=None):
    M, K = a.shape
    R2 = a2.shape[0]
    cast_w = w.dtype != BF
    if w_rows is None:
        N = w.shape[2]
        w_spec = pl.BlockSpec((None, K, tn), lambda j, i: (layer, 0, j))
        wbf_shape = (K, tn)
    else:
        row0, N = w_rows
        assert row0 % 8 == 0 and tn % 8 == 0
        w_spec = pl.BlockSpec((None, pl.Element(tn), pl.Element(K)),
                              lambda j, i: (layer, pl.multiple_of(row0 + j * tn, 8), 0))
        wbf_shape = (tn, K)
    in_specs = [pl.BlockSpec((tm, K), lambda j, i: (i, 0)),
                pl.BlockSpec((R2, K), lambda j, i: (0, 0)),
                w_spec]
    args = [a, a2, w]
    if epi == "resid":
        x, ga, ga_map, x2, ga2 = res
        in_specs += [pl.BlockSpec((tm, tn), lambda j, i: (i, j)),
                     pl.BlockSpec((None, ga.shape[1], tn), ga_map),
                     pl.BlockSpec((R2, tn), lambda j, i: (0, j)),
                     pl.BlockSpec((None, R2, tn), lambda j, i: (0, 0, j))]
        args += [x, ga, x2, ga2]
    return pl.pallas_call(
        functools.partial(_mm2_kernel, epi=epi, cast_w=cast_w, w_transposed=w_rows is not None),
        grid=(N // tn, M // tm), in_specs=in_specs,
        out_specs=[pl.BlockSpec((tm, tn), lambda j, i: (i, j)),
                   pl.BlockSpec((R2, tn), lambda j, i: (0, j))],
        out_shape=[jax.ShapeDtypeStruct((M, N), out_dtype), jax.ShapeDtypeStruct((R2, N), out_dtype)],
        scratch_shapes=[pltpu.VMEM(wbf_shape, BF)] if cast_w else [],
        compiler_params=_cparams(("arbitrary", "arbitrary")),
    )(*args)


def _merge2_kernel(an_ref, ar_ref, gn_ref, gr_ref, an2_ref, ar2_ref, gn2_ref, gr2_ref, wn_ref, wr_ref,
                   o_ref, o2_ref, wnb_ref, wrb_ref):
    first = pl.program_id(1) == 0
    wn = _cached_bf16(wn_ref, wnb_ref, first)
    wr = _cached_bf16(wr_ref, wrb_ref, first)

    def finish(a_n, a_r, g_n, g_r, out_ref):
        o = jax.nn.sigmoid(g_n[...]) * _dot(a_n[...], wn) + jax.nn.sigmoid(g_r[...]) * _dot(a_r[...], wr)
        out_ref[...] = o.astype(out_ref.dtype)

    finish(an_ref, ar_ref, gn_ref, gr_ref, o_ref)

    @pl.when(first)
    def _():
        finish(an2_ref, ar2_ref, gn2_ref, gr2_ref, o2_ref)


def _merge2(o_nsa, o_rnn, z, o_nsa2, o_rnn2, z2, w_nsa, w_rnn, *, layer, col0, tm, tn):
    M, K = o_nsa.shape
    R2 = o_nsa2.shape[0]
    D = w_nsa.shape[2]
    c0, c1 = col0 // tn, (col0 + D) // tn
    row = lambda j, i: (i, 0)
    one = lambda j, i: (0, 0)
    return pl.pallas_call(
        _merge2_kernel,
        grid=(D // tn, M // tm),
        in_specs=[pl.BlockSpec((tm, K), row), pl.BlockSpec((tm, K), row),
                  pl.BlockSpec((tm, tn), lambda j, i: (i, j + c0)),
                  pl.BlockSpec((tm, tn), lambda j, i: (i, j + c1)),
                  pl.BlockSpec((R2, K), one), pl.BlockSpec((R2, K), one),
                  pl.BlockSpec((R2, tn), lambda j, i: (0, j + c0)),
                  pl.BlockSpec((R2, tn), lambda j, i: (0, j + c1)),
                  pl.BlockSpec((None, K, tn), lambda j, i: (layer, 0, j)),
                  pl.BlockSpec((None, K, tn), lambda j, i: (layer, 0, j))],
        out_specs=[pl.BlockSpec((tm, tn), lambda j, i: (i, j)),
                   pl.BlockSpec((R2, tn), lambda j, i: (0, j))],
        out_shape=[jax.ShapeDtypeStruct((M, D), BF), jax.ShapeDtypeStruct((R2, D), BF)],
        scratch_shapes=[pltpu.VMEM((K, tn), BF), pltpu.VMEM((K, tn), BF)],
        compiler_params=_cparams(("arbitrary", "arbitrary")),
    )(o_nsa, o_rnn, z, z, o_nsa2, o_rnn2, z2, z2, w_nsa, w_rnn)


def _ln_kernel(x_ref, g_ref, sc_ref, sh_ref, o_ref):
    x = x_ref[...]
    ms = jnp.mean(x * x, axis=-1, keepdims=True)
    y = x * lax.rsqrt(ms + EPS) * g_ref[...]
    o_ref[...] = (y * (1.0 + sc_ref[...]) + sh_ref[...]).astype(o_ref.dtype)


def _ln_mod(x, g, sc, sh, *, tt):
    B, T, D = x.shape
    R = sc.shape[1]
    rr = tt if R == T else 1
    mod_map = (lambda b, i: (b, i, 0)) if R == T else (lambda b, i: (b, 0, 0))
    return pl.pallas_call(
        _ln_kernel,
        grid=(B, T // tt),
        in_specs=[pl.BlockSpec((None, tt, D), lambda b, i: (b, i, 0)),
                  pl.BlockSpec((1, D), lambda b, i: (0, 0)),
                  pl.BlockSpec((None, rr, D), mod_map),
                  pl.BlockSpec((None, rr, D), mod_map)],
        out_specs=pl.BlockSpec((None, tt, D), lambda b, i: (b, i, 0)),
        out_shape=jax.ShapeDtypeStruct((B, T, D), BF),
        compiler_params=_cparams(("parallel", "arbitrary")),
    )(x, g, sc, sh)


def _qk_kernel(z_ref, c_ref, s_ref, g_ref, q_ref, kv_ref, win_ref, kvb_ref, *, scale):
    cosf = c_ref[...]
    sinf = s_ref[...]

    def norm_rope(col, gi):
        x = z_ref[:, col:col + HEAD_DIM]
        ms = jnp.mean(x * x, axis=-1, keepdims=True)
        y = x * lax.rsqrt(ms + EPS) * g_ref[gi:gi + 1, :]
        return y * cosf + pltpu.roll(y, HEAD_DIM // 2, 1) * sinf

    nq = N_HEADS * HEAD_DIM
    grp = N_KV * HEAD_DIM
    for h in range(N_HEADS):
        q_ref[:, h * HEAD_DIM:(h + 1) * HEAD_DIM] = (norm_rope(h * HEAD_DIM, 0) * scale).astype(BF)
    for j in range(3):
        for gi in range(N_KV):
            kcol = nq + (2 * j) * grp + gi * HEAD_DIM
            vcol = nq + (2 * j + 1) * grp + gi * HEAD_DIM
            k = norm_rope(kcol, 1 + j)
            v = z_ref[:, vcol:vcol + HEAD_DIM]
            ko = (2 * j) * grp + gi * HEAD_DIM
            vo = (2 * j + 1) * grp + gi * HEAD_DIM
            kvb_ref[:, ko:ko + HEAD_DIM] = k.astype(BF)
            kvb_ref[:, vo:vo + HEAD_DIM] = v.astype(BF)
            tt = k.shape[0]
            if j < 2:
                planes = 4 * N_KV
                kv_ref[pl.ds(2 * j * N_KV + gi, tt, stride=planes), :] = k
                kv_ref[pl.ds((2 * j + 1) * N_KV + gi, tt, stride=planes), :] = v
            else:
                planes = 2 * N_KV
                win_ref[pl.ds(gi, tt, stride=planes), :] = k
                win_ref[pl.ds(N_KV + gi, tt, stride=planes), :] = v


def _qk_prep(z, cosf, sinf, qk_g, *, tt):
    B, T, _ = z.shape
    nq = N_HEADS * HEAD_DIM
    grp = N_KV * HEAD_DIM
    row = lambda b, i: (b, i, 0)
    return pl.pallas_call(
        functools.partial(_qk_kernel, scale=HEAD_DIM ** -0.5),
        grid=(B, T // tt),
        in_specs=[pl.BlockSpec((None, tt, z.shape[2]), row),
                  pl.BlockSpec((tt, HEAD_DIM), lambda b, i: (i, 0)),
                  pl.BlockSpec((tt, HEAD_DIM), lambda b, i: (i, 0)),
                  pl.BlockSpec((4, HEAD_DIM), lambda b, i: (0, 0))],
        out_specs=[pl.BlockSpec((None, tt, nq), row),
                   pl.BlockSpec((None, tt * 4 * N_KV, HEAD_DIM), row),
                   pl.BlockSpec((None, tt * 2 * N_KV, HEAD_DIM), row),
                   pl.BlockSpec((None, tt, 6 * grp), row)],
        out_shape=[jax.ShapeDtypeStruct((B, T, nq), BF),
                   jax.ShapeDtypeStruct((B, T * 4 * N_KV, HEAD_DIM), F32),
                   jax.ShapeDtypeStruct((B, T * 2 * N_KV, HEAD_DIM), F32),
                   jax.ShapeDtypeStruct((B, T, 6 * grp), BF)],
        compiler_params=_cparams(("parallel", "arbitrary")),
    )(z, cosf, sinf, qk_g)


def _rope_tables(pos0, T):
    half = HEAD_DIM // 2
    inv = ROPE_THETA ** (-jnp.arange(half, dtype=F32) / half)
    ang = (pos0 + jnp.arange(T, dtype=F32))[:, None] * inv[None, :]
    cos, sin = jnp.cos(ang), jnp.sin(ang)
    return jnp.concatenate([cos, cos], axis=1), jnp.concatenate([-sin, sin], axis=1)


def _compress_acc(load_pair, pe_ref, w_ref, rows):
    acc = jnp.zeros((rows, 2 * HEAD_DIM), F32)
    for lp in range(CMP_BLOCK // 2):
        xs = []
        for l in (2 * lp, 2 * lp + 1):
            x = load_pair(l) + pe_ref[l][None]
            xs.append(x.reshape(rows, HEAD_DIM).astype(BF))
        acc = acc + _dot(jnp.concatenate(xs, axis=1), w_ref[lp])
    is_k = (lax.broadcasted_iota(jnp.int32, (rows, HEAD_DIM), 0) % 8) < N_KV
    return jnp.where(is_k, acc[:, :HEAD_DIM], acc[:, HEAD_DIM:])


def _compress_kernel(x_ref, pe_ref, w_ref, o_ref, *, nblk):
    rows = nblk * 8
    out = _compress_acc(lambda l: x_ref[pl.ds(l, nblk, stride=CMP_BLOCK), :, :], pe_ref, w_ref, rows)
    o_ref[...] = out.reshape(nblk, 8, HEAD_DIM)


def _compress(kvrows, pe_t, w2):
    B = kvrows.shape[0]
    T = kvrows.shape[1] // (4 * N_KV)
    nblk = T // CMP_BLOCK
    x = kvrows.reshape(B, T, 4 * N_KV, HEAD_DIM)
    return pl.pallas_call(
        functools.partial(_compress_kernel, nblk=nblk),
        grid=(B,),
        in_specs=[pl.BlockSpec((None, T, 8, HEAD_DIM), lambda b: (b, 0, 0, 0)),
                  pl.BlockSpec((CMP_BLOCK, 8, HEAD_DIM), lambda b: (0, 0, 0)),
                  pl.BlockSpec((CMP_BLOCK // 2, 2 * HEAD_DIM, 2 * HEAD_DIM), lambda b: (0, 0, 0))],
        out_specs=pl.BlockSpec((None, nblk, 8, HEAD_DIM), lambda b: (b, 0, 0, 0)),
        out_shape=jax.ShapeDtypeStruct((B, nblk, 8, HEAD_DIM), F32),
        compiler_params=_cparams(("parallel",)),
    )(x, pe_t, w2)


def _compress_paged_kernel(pt_ref, *refs, pb):
    page_refs = refs[:pb]
    pe_ref, w_ref, o_ref = refs[pb:pb + 3]
    rows = pb * 2 * 8

    def load(l):
        tiles = []
        for p in range(pb):
            for h in range(2):
                tiles.append(page_refs[p][h * CMP_BLOCK + l])
        return jnp.stack(tiles, axis=0)

    out = _compress_acc(load, pe_ref, w_ref, rows)
    o_ref[...] = out.reshape(pb * 2, 8, HEAD_DIM)


def _compress_paged(cache5, page_table, layer, pe_t, w2, *, pb):
    n_phys, page = cache5.shape[1], cache5.shape[2]
    B, n_pages = page_table.shape
    assert page == 2 * CMP_BLOCK and n_pages % pb == 0

    def page_spec(p):
        return pl.BlockSpec((None, None, page, 8, HEAD_DIM),
                            lambda b, i, pt: (layer, pt[b, i * pb + p], 0, 0, 0))

    grid_spec = pltpu.PrefetchScalarGridSpec(
        num_scalar_prefetch=1,
        grid=(B, n_pages // pb),
        in_specs=[page_spec(p) for p in range(pb)] + [
            pl.BlockSpec((CMP_BLOCK, 8, HEAD_DIM), lambda b, i, pt: (0, 0, 0)),
            pl.BlockSpec((CMP_BLOCK // 2, 2 * HEAD_DIM, 2 * HEAD_DIM), lambda b, i, pt: (0, 0, 0))],
        out_specs=pl.BlockSpec((None, pb * 2, 8, HEAD_DIM), lambda b, i, pt: (b, i, 0, 0)),
    )
    return pl.pallas_call(
        functools.partial(_compress_paged_kernel, pb=pb),
        grid_spec=grid_spec,
        out_shape=jax.ShapeDtypeStruct((B, n_pages * 2, 8, HEAD_DIM), F32),
        compiler_params=_cparams(("parallel", "arbitrary")),
    )(page_table, *([cache5] * pb), pe_t, w2)


def _online_step(s2, v, m_ref, l_ref, acc_ref):
    tk = s2.shape[1]
    m_prev = m_ref[...]
    m_next = jnp.maximum(m_prev, jnp.max(s2, axis=1, keepdims=True))
    alpha = jnp.exp(m_prev - m_next)
    p = jnp.exp(s2 - _rep_lanes(m_next, tk))
    l_ref[...] = alpha * l_ref[...] + jnp.sum(p, axis=1, keepdims=True)
    m_ref[...] = m_next
    acc_ref[...] = alpha * acc_ref[...] + _dot(p.astype(BF), v)


def _nsa_kernel(*refs, tq, nblk, n_side):
    q_ref, kc_ref, vc_ref, ks_ref, vs_ref, kw_ref, vw_ref, g_ref = refs[:8]
    o_ref = refs[8 + n_side]
    m_ref, l_ref, acc_ref = refs[9 + 2 * n_side:]
    for src, dst in zip(refs[8:8 + n_side], refs[9 + n_side:9 + 2 * n_side]):
        dst[...] = src[...].astype(BF)

    qi = pl.program_id(2)
    q0 = qi * tq
    rows = HPG * tq
    q = q_ref[...]
    qcat = jnp.concatenate([q[:, h * HEAD_DIM:(h + 1) * HEAD_DIM] for h in range(HPG)], axis=0)

    kc = kc_ref[...].astype(BF)
    vc = vc_ref[...].astype(BF)
    t_b = q0 + lax.broadcasted_iota(jnp.int32, (nblk, tq), 1)
    blk = lax.broadcasted_iota(jnp.int32, (nblk, tq), 0)
    complete = jnp.concatenate([(blk + 1) * CMP_BLOCK - 1 <= t_b] * HPG, axis=1)
    s_t = jnp.where(complete, _dot_nt(kc, qcat), NEG)
    mx = jnp.max(s_t, axis=0, keepdims=True)
    e = jnp.where(complete, jnp.exp(s_t - mx), 0.0)
    d = jnp.sum(e, axis=0, keepdims=True)
    p_t = e / jnp.where(d > 0, d, 1.0)

    imp = p_t[:, 0:tq]
    for h in range(1, HPG):
        imp = imp + p_t[:, h * tq:(h + 1) * tq]
    cur = jnp.right_shift(t_b, CMP_SHIFT)
    forced = (blk == 0) | (blk == cur) | (blk == cur - 1)
    score = jnp.where(forced, FORCED_SCORE, imp)
    score = jnp.where(blk <= cur, score, -1.0)
    rank = jnp.zeros((nblk, tq), F32)
    for i in range(nblk):
        si = score[i:i + 1, :]
        beats = (si > score) | ((si == score) & (blk > i))
        rank = rank + jnp.where(beats, 1.0, 0.0)
    sel_t = jnp.where((rank < min(N_SELECT, nblk)) & (score > -0.5), 1.0, 0.0).astype(BF)

    eye = jnp.where(lax.broadcasted_iota(jnp.int32, (tq, tq), 0)
                    == lax.broadcasted_iota(jnp.int32, (tq, tq), 1), 1.0, 0.0).astype(BF)
    p_bf = p_t.astype(BF)
    p_rows = jnp.concatenate([_dot_nt(eye, p_bf[:, h * tq:(h + 1) * tq]) for h in range(HPG)], axis=0)
    o_cmp = _dot(p_rows.astype(BF), vc)
    sel = _dot_nt(eye, sel_t).astype(BF)

    t_k = q0 + lax.broadcasted_iota(jnp.int32, (tq, tq), 0)
    k_off = lax.broadcasted_iota(jnp.int32, (tq, tq), 1)
    e_blk = lax.broadcasted_iota(jnp.int32, (nblk, tq), 0)
    e_off = lax.broadcasted_iota(jnp.int32, (nblk, tq), 1)

    def reset():
        m_ref[...] = jnp.full((rows, LANES), NEG, F32)
        l_ref[...] = jnp.zeros((rows, LANES), F32)
        acc_ref[...] = jnp.zeros((rows, HEAD_DIM), F32)

    def attend(c, k_ref, v_ref, mask):
        start = pl.multiple_of(c * tq, tq)
        k = k_ref[pl.ds(start, tq), :]
        v = v_ref[pl.ds(start, tq), :]
        s = jnp.where(mask[None], _dot_nt(qcat, k).reshape(HPG, tq, tq), NEG)
        _online_step(s.reshape(rows, tq), v, m_ref, l_ref, acc_ref)

    def sel_body(c, carry):
        expand = jnp.where(jnp.right_shift(c * tq + e_off, CMP_SHIFT) == e_blk, 1.0, 0.0).astype(BF)
        chosen = _dot(sel, expand) > 0.5
        attend(c, ks_ref, vs_ref, chosen & (c * tq + k_off <= t_k))
        return carry

    reset()
    lax.fori_loop(0, qi + 1, sel_body, 0)
    o_sel = acc_ref[...] / l_ref[...]

    def win_body(c, carry):
        dist = t_k - (c * tq + k_off)
        attend(c, kw_ref, vw_ref, (dist >= 0) & (dist < WINDOW))
        return carry

    reset()
    lax.fori_loop(jnp.maximum(qi - WINDOW // tq, 0), qi + 1, win_body, 0)
    o_win = acc_ref[...] / l_ref[...]

    g = jax.nn.sigmoid(g_ref[...])
    for h in range(HPG):
        r = slice(h * tq, (h + 1) * tq)
        o = (g[:, 3 * h:3 * h + 1] * o_cmp[r] + g[:, 3 * h + 1:3 * h + 2] * o_sel[r]
             + g[:, 3 * h + 2:3 * h + 3] * o_win[r])
        o_ref[:, h * HEAD_DIM:(h + 1) * HEAD_DIM] = o.astype(BF)


def _nsa_prompt(q, kcv, kvb, zg, *, tq, layer, side_cast):
    B, T, _ = q.shape
    nblk = kcv.shape[1]
    n_q = T // tq
    n_steps = B * N_KV * n_q
    col = lambda c: pl.BlockSpec((None, T, HEAD_DIM), lambda b, g, i, c=c: (b, 0, c + g))
    rows = HPG * tq
    in_specs = [pl.BlockSpec((None, tq, HPG * HEAD_DIM), lambda b, g, i: (b, i, g)),
                pl.BlockSpec((None, nblk, HEAD_DIM), lambda b, g, i: (b, 0, g)),
                pl.BlockSpec((None, nblk, HEAD_DIM), lambda b, g, i: (b, 0, N_KV + g)),
                col(2 * N_KV), col(3 * N_KV), col(4 * N_KV), col(5 * N_KV),
                pl.BlockSpec((None, tq, LANES), lambda b, g, i: (b, i, g))]
    out_specs = [pl.BlockSpec((None, tq, HPG * HEAD_DIM), lambda b, g, i: (b, i, g))]
    out_shape = [jax.ShapeDtypeStruct((B, T, N_HEADS * HEAD_DIM), BF)]
    for p in side_cast:
        _, n_rows, n_cols = p.shape
        rb = next(r for r in range(16, n_rows + 1, 16) if n_rows % r == 0 and n_rows // r <= n_steps)
        last = n_rows // rb - 1
        step = lambda b, g, i, last=last: jnp.minimum((b * N_KV + g) * n_q + i, last)
        in_specs.append(pl.BlockSpec((None, rb, n_cols), lambda b, g, i, step=step: (layer, step(b, g, i), 0)))
        out_specs.append(pl.BlockSpec((None, rb, n_cols), lambda b, g, i, step=step: (0, step(b, g, i), 0)))
        out_shape.append(jax.ShapeDtypeStruct((1, n_rows, n_cols), BF))
    return pl.pallas_call(
        functools.partial(_nsa_kernel, tq=tq, nblk=nblk, n_side=len(side_cast)),
        grid=(B, N_KV, n_q), in_specs=in_specs, out_specs=out_specs, out_shape=out_shape,
        scratch_shapes=[pltpu.VMEM((rows, LANES), F32), pltpu.VMEM((rows, LANES), F32),
                        pltpu.VMEM((rows, HEAD_DIM), F32)],
        compiler_params=_cparams(("arbitrary", "arbitrary", "arbitrary")),
    )(q, kcv, kcv, kvb, kvb, kvb, kvb, zg, *side_cast)


def _softplus(y):
    return jnp.maximum(y, 0.0) + jnp.log1p(jnp.exp(-jnp.abs(y)))


def _rg_gates(u, w_ref, brg_ref, lam_ref):
    ub = u.astype(BF)
    r = jax.nn.sigmoid(_dot(ub, w_ref[0].astype(BF)) + brg_ref[0:1, :])
    i = jax.nn.sigmoid(_dot(ub, w_ref[1].astype(BF)) + brg_ref[1:2, :])
    log_a = -RG_C * r * _softplus(-lam_ref[...])
    a = jnp.exp(log_a)
    th = jnp.tanh(log_a)
    b = jnp.sqrt(-2.0 * th / (1.0 - th)) * i * u
    return a, b


def _rglru_kernel(xr_ref, zg_ref, c0_ref, h0_ref, cw_ref, cb_ref, w_ref, brg_ref, lam_ref,
                  y_ref, hl_ref, cl_ref, xp_ref, a_ref, b_ref, h_ref, *, T):
    pad = 8
    xp_ref[0:pad, :] = jnp.zeros((pad, RNN_BW), F32)
    xp_ref[pad - (CONV_W - 1):pad, :] = c0_ref[...]
    xp_ref[pad:pad + T, :] = xr_ref[...]
    u = cb_ref[...] + xp_ref[pad:pad + T, :] * cw_ref[CONV_W - 1:CONV_W, :]
    for k in range(CONV_W - 1):
        off = pad - (CONV_W - 1) + k
        u = u + xp_ref[off:off + T, :] * cw_ref[k:k + 1, :]
    cl_ref[...] = xp_ref[pad + T - (CONV_W - 1):pad + T, :]

    a, b = _rg_gates(u, w_ref, brg_ref, lam_ref)
    a_ref[0:T, :] = a
    b_ref[0:T, :] = b

    levels = []
    n, off = T, 0
    while True:
        levels.append((n, off))
        if n <= 8:
            break
        off += n
        n //= 8
    for (n, off), (n2, off2) in zip(levels[:-1], levels[1:]):
        A = a_ref[pl.ds(off, n2, stride=8), :]
        Bv = b_ref[pl.ds(off, n2, stride=8), :]
        for r in range(1, 8):
            ar = a_ref[pl.ds(off + r, n2, stride=8), :]
            br = b_ref[pl.ds(off + r, n2, stride=8), :]
            Bv = ar * Bv + br
            A = ar * A
        a_ref[off2:off2 + n2, :] = A
        b_ref[off2:off2 + n2, :] = Bv
    n, off = levels[-1]
    h = h0_ref[...]
    for t in range(n):
        h = a_ref[off + t:off + t + 1, :] * h + b_ref[off + t:off + t + 1, :]
        h_ref[off + 8 + t:off + 9 + t, :] = h
    for (n, off), (n2, off2) in reversed(list(zip(levels[:-1], levels[1:]))):
        h_ref[off2 + 7:off2 + 8, :] = h0_ref[...]
        hp = h_ref[off2 + 7:off2 + 7 + n2, :]
        for r in range(8):
            ar = a_ref[pl.ds(off + r, n2, stride=8), :]
            br = b_ref[pl.ds(off + r, n2, stride=8), :]
            hp = ar * hp + br
            h_ref[pl.ds(off + 8 + r, n2, stride=8), :] = hp
    hfull = h_ref[8:8 + T, :]
    y_ref[...] = (jax.nn.gelu(zg_ref[...]) * hfull).astype(BF)
    hl_ref[...] = h_ref[8 + T - 1:8 + T, :]


def _rglru(zr, conv0, h0, conv_w, conv_b, w_rg, b_rg, lam, *, d_rnn):
    B, T, _ = zr.shape
    nb = d_rnn // RNN_BW
    assert T % 8 == 0
    total = T + T // 4 + 64
    return pl.pallas_call(
        functools.partial(_rglru_kernel, T=T),
        grid=(B, nb),
        in_specs=[pl.BlockSpec((None, T, RNN_BW), lambda b, n: (b, 0, n)),
                  pl.BlockSpec((None, T, RNN_BW), lambda b, n: (b, 0, nb + n)),
                  pl.BlockSpec((None, CONV_W - 1, RNN_BW), lambda b, n: (b, 0, n)),
                  pl.BlockSpec((None, 1, RNN_BW), lambda b, n: (b, 0, n)),
                  pl.BlockSpec((CONV_W, RNN_BW), lambda b, n: (0, n)),
                  pl.BlockSpec((1, RNN_BW), lambda b, n: (0, n)),
                  pl.BlockSpec((2, None, RNN_BW, RNN_BW), lambda b, n: (0, n, 0, 0)),
                  pl.BlockSpec((2, RNN_BW), lambda b, n: (0, n)),
                  pl.BlockSpec((1, RNN_BW), lambda b, n: (0, n))],
        out_specs=[pl.BlockSpec((None, T, RNN_BW), lambda b, n: (b, 0, n)),
                   pl.BlockSpec((None, 1, RNN_BW), lambda b, n: (b, 0, n)),
                   pl.BlockSpec((None, CONV_W - 1, RNN_BW), lambda b, n: (b, 0, n))],
        out_shape=[jax.ShapeDtypeStruct((B, T, d_rnn), BF),
                   jax.ShapeDtypeStruct((B, 1, d_rnn), F32),
                   jax.ShapeDtypeStruct((B, CONV_W - 1, d_rnn), F32)],
        scratch_shapes=[pltpu.VMEM((T + 8, RNN_BW), F32), pltpu.VMEM((total, RNN_BW), F32),
                        pltpu.VMEM((total, RNN_BW), F32), pltpu.VMEM((total + 8, RNN_BW), F32)],
        compiler_params=_cparams(("parallel", "arbitrary")),
    )(zr, zr, conv0, h0, conv_w, conv_b[None], w_rg, b_rg, lam[None])


def _rglru_step_kernel(xr_ref, zg_ref, c0_ref, h0_ref, cw_ref, cb_ref, w_ref, brg_ref, lam_ref,
                       y_ref, hl_ref):
    u = cb_ref[...] + xr_ref[...] * cw_ref[CONV_W - 1:CONV_W, :]
    for k in range(CONV_W - 1):
        u = u + c0_ref[k] * cw_ref[k:k + 1, :]
    a, b = _rg_gates(u, w_ref, brg_ref, lam_ref)
    h = a * h0_ref[...] + b
    hl_ref[...] = h
    y_ref[...] = (jax.nn.gelu(zg_ref[...]) * h).astype(BF)


def _rglru_step(zr, conv0_t, h0, conv_w, conv_b, w_rg, b_rg, lam, *, d_rnn):
    R = zr.shape[0]
    nb = d_rnn // RNN_BW
    return pl.pallas_call(
        _rglru_step_kernel,
        grid=(nb,),
        in_specs=[pl.BlockSpec((R, RNN_BW), lambda n: (0, n)),
                  pl.BlockSpec((R, RNN_BW), lambda n: (0, nb + n)),
                  pl.BlockSpec((CONV_W - 1, R, RNN_BW), lambda n: (0, 0, n)),
                  pl.BlockSpec((R, RNN_BW), lambda n: (0, n)),
                  pl.BlockSpec((CONV_W, RNN_BW), lambda n: (0, n)),
                  pl.BlockSpec((1, RNN_BW), lambda n: (0, n)),
                  pl.BlockSpec((2, None, RNN_BW, RNN_BW), lambda n: (0, n, 0, 0)),
                  pl.BlockSpec((2, RNN_BW), lambda n: (0, n)),
                  pl.BlockSpec((1, RNN_BW), lambda n: (0, n))],
        out_specs=[pl.BlockSpec((R, RNN_BW), lambda n: (0, n)),
                   pl.BlockSpec((R, RNN_BW), lambda n: (0, n))],
        out_shape=[jax.ShapeDtypeStruct((R, d_rnn), BF), jax.ShapeDtypeStruct((R, d_rnn), F32)],
        compiler_params=_cparams(("arbitrary",)),
    )(zr, zr, conv0_t, h0, conv_w, conv_b[None], w_rg, b_rg, lam[None])


def _dec_cmp_kernel(q_ref, kcv_ref, o_ref, idx_ref, *, n_past, t_pos):
    nlan = LANES * ((n_past + 1 + LANES - 1) // LANES)
    blk = lax.broadcasted_iota(jnp.int32, (1, n_past), 1)
    complete = (blk + 1) * CMP_BLOCK - 1 <= t_pos
    imps = []
    for g in range(N_KV):
        kc = kcv_ref[:, g * HEAD_DIM:(g + 1) * HEAD_DIM].astype(BF)
        vc = kcv_ref[:, (N_KV + g) * HEAD_DIM:(N_KV + g + 1) * HEAD_DIM].astype(BF)
        s = jnp.where(complete, _dot_nt(q_ref[g], kc), NEG)
        mx = jnp.max(s, axis=-1, keepdims=True)
        e = jnp.where(complete, jnp.exp(s - mx), 0.0)
        d = jnp.sum(e, axis=-1, keepdims=True)
        p = e / jnp.where(d > 0, d, 1.0)
        o_ref[g] = _dot(p.astype(BF), vc)
        head = lax.broadcasted_iota(jnp.int32, p.shape, 0) < HPG
        imps.append(jnp.sum(jnp.where(head, p, 0.0), axis=0, keepdims=True))
    imp = jnp.concatenate(imps, axis=0)

    cur = t_pos // CMP_BLOCK
    imp_all = jnp.concatenate([imp, jnp.zeros((N_KV, nlan - n_past), F32)], axis=1)
    blk_all = lax.broadcasted_iota(jnp.int32, (N_KV, nlan), 1)
    forced = (blk_all == 0) | (blk_all == cur) | (blk_all == cur - 1)
    score = jnp.where(forced, FORCED_SCORE, imp_all)
    score = jnp.where(blk_all <= cur, score, -1.0)
    score = jnp.where(blk_all <= n_past, score, -2.0)
    blk_f = blk_all.astype(F32)
    slot = lax.broadcasted_iota(jnp.int32, (N_KV, LANES), 1)
    res = jnp.full((N_KV, LANES), float(n_past), F32)
    for k in range(min(N_SELECT, n_past + 1)):
        top = jnp.max(score, axis=-1, keepdims=True)
        arg = jnp.min(jnp.where(score == top, blk_f, float(nlan)), axis=-1, keepdims=True)
        res = jnp.where(slot == k, jnp.where(top > -0.5, arg, float(n_past)), res)
        score = jnp.where(blk_f == arg, -3.0, score)
    idx_ref[...] = res.astype(jnp.int32)


def _dec_cmp(q16, kcv, *, t_pos):
    B = q16.shape[0]
    n_past = kcv.shape[1]
    return pl.pallas_call(
        functools.partial(_dec_cmp_kernel, n_past=n_past, t_pos=t_pos),
        grid=(B,),
        in_specs=[pl.BlockSpec((None, N_KV, 16, HEAD_DIM), lambda b: (b, 0, 0, 0)),
                  pl.BlockSpec((None, n_past, 2 * N_KV * HEAD_DIM), lambda b: (b, 0, 0))],
        out_specs=[pl.BlockSpec((None, N_KV, 16, HEAD_DIM), lambda b: (b, 0, 0, 0)),
                   pl.BlockSpec((None, N_KV, LANES), lambda b: (b, 0, 0))],
        out_shape=[jax.ShapeDtypeStruct((B, N_KV, 16, HEAD_DIM), F32),
                   jax.ShapeDtypeStruct((B, N_KV, LANES), jnp.int32)],
        compiler_params=_cparams(("parallel",)),
    )(q16, kcv)


def _dec_attn_kernel(pt_ref, ti_ref, q_ref, c0_ref, c1_ref, c2_ref, c3_ref, new_ref, win_ref, oc_ref, g_ref,
                     o_ref, m_ref, l_ref, acc_ref, *, n_past):
    b, s = pl.program_id(0), pl.program_id(1)
    cache_refs = (c0_ref, c1_ref, c2_ref, c3_ref)
    planes = 4 * N_KV

    @pl.when(s == 0)
    def _():
        m_ref[...] = jnp.full(m_ref.shape, NEG, F32)
        l_ref[...] = jnp.zeros(l_ref.shape, F32)
        acc_ref[...] = jnp.zeros(acc_ref.shape, F32)

    for g in range(N_KV):
        live = ti_ref[b * N_KV + g, s] < n_past
        k = cache_refs[g][pl.ds(2 * N_KV + g, CMP_BLOCK, stride=planes), :].astype(BF)
        v = cache_refs[g][pl.ds(3 * N_KV + g, CMP_BLOCK, stride=planes), :].astype(BF)
        sc = jnp.where(live, _dot_nt(q_ref[g], k), NEG)
        m_prev = m_ref[g]
        m_next = jnp.maximum(m_prev, jnp.max(sc, axis=1, keepdims=True))
        alpha = jnp.exp(m_prev - m_next)
        p = jnp.where(live, jnp.exp(sc - m_next[:, :CMP_BLOCK]), 0.0)
        l_ref[g] = alpha * l_ref[g] + jnp.sum(p, axis=1, keepdims=True)
        m_ref[g] = m_next
        acc_ref[g] = alpha * acc_ref[g] + _dot(p.astype(BF), v)

    @pl.when(s == pl.num_programs(1) - 1)
    def _():
        n_win = win_ref.shape[0] // (2 * N_KV)
        keep = lax.broadcasted_iota(jnp.int32, (1, n_win), 1) >= n_win + 1 - WINDOW
        for g in range(N_KV):
            q = q_ref[g]
            qf = q.astype(F32)

            def self_score(k_row):
                return jnp.sum(qf * k_row.astype(BF).astype(F32), axis=1, keepdims=True)

            s_self = self_score(new_ref[g, 2:3, :])
            v_self = new_ref[g, 3:4, :].astype(BF).astype(F32)
            m_prev = m_ref[g]
            m_fin = jnp.maximum(m_prev, s_self)
            alpha = jnp.exp(m_prev - m_fin)
            p_self = jnp.exp(s_self - m_fin)
            o_sel = (alpha * acc_ref[g] + p_self * v_self) / (alpha * l_ref[g] + p_self)

            kw = win_ref[pl.ds(g, n_win, stride=2 * N_KV), :].astype(BF)
            vw = win_ref[pl.ds(N_KV + g, n_win, stride=2 * N_KV), :].astype(BF)
            sw = jnp.where(keep, _dot_nt(q, kw), NEG)
            sw_self = self_score(new_ref[g, 4:5, :])
            mw = jnp.maximum(jnp.max(sw, axis=1, keepdims=True), sw_self)
            pw = jnp.where(keep, jnp.exp(sw - mw), 0.0)
            pw_self = jnp.exp(sw_self - mw)
            num = _dot(pw.astype(BF), vw) + pw_self * new_ref[g, 5:6, :].astype(BF).astype(F32)
            o_win = num / (jnp.sum(pw, axis=1, keepdims=True) + pw_self)

            gt = jax.nn.sigmoid(g_ref[g])
            o_ref[g] = gt[:, 0:1] * oc_ref[g] + gt[:, 1:2] * o_sel + gt[:, 2:3] * o_win


def _dec_attn(q16, cache4, page_table, top_i, newkv, win3, o_cmp, g16, *, layer, n_past):
    B = q16.shape[0]
    n_slots = top_i.shape[1]
    half_rows = CMP_BLOCK * 4 * N_KV

    def half_page(g):
        def index(b, s, pt, ti):
            blk = jnp.minimum(ti[b * N_KV + g, s], n_past - 1)
            return (layer, pt[b, blk // 2], blk % 2, 0)
        return pl.BlockSpec((None, None, half_rows, HEAD_DIM), index)

    per_seq = lambda b, s, pt, ti: (b, 0, 0, 0)
    grid_spec = pltpu.PrefetchScalarGridSpec(
        num_scalar_prefetch=2,
        grid=(B, n_slots),
        in_specs=[pl.BlockSpec((None, N_KV, 16, HEAD_DIM), per_seq)]
        + [half_page(g) for g in range(N_KV)]
        + [pl.BlockSpec((None, N_KV, 6, HEAD_DIM), per_seq),
           pl.BlockSpec((None, win3.shape[1], HEAD_DIM), lambda b, s, pt, ti: (layer * B + b, 0, 0)),
           pl.BlockSpec((None, N_KV, 16, HEAD_DIM), per_seq),
           pl.BlockSpec((None, N_KV, 16, LANES), per_seq)],
        out_specs=pl.BlockSpec((None, N_KV, 16, HEAD_DIM), per_seq),
        scratch_shapes=[pltpu.VMEM((N_KV, 16, LANES), F32), pltpu.VMEM((N_KV, 16, LANES), F32),
                        pltpu.VMEM((N_KV, 16, HEAD_DIM), F32)],
    )
    return pl.pallas_call(
        functools.partial(_dec_attn_kernel, n_past=n_past),
        grid_spec=grid_spec,
        out_shape=jax.ShapeDtypeStruct((B, N_KV, 16, HEAD_DIM), F32),
        compiler_params=_cparams(("parallel", "arbitrary")),
    )(page_table, top_i, q16, *([cache4] * N_KV), newkv, win3, o_cmp, g16)


def _regroup_w_in(w_in):
    L, D, _ = w_in.shape
    o_gn = N_HEADS * HEAD_DIM + 6 * N_KV * HEAD_DIM
    o_rx = o_gn + 3 * N_HEADS
    wg = w_in[:, :, o_gn:o_rx].reshape(L, D, N_KV, 3 * HPG)
    wg = jnp.pad(wg, ((0, 0), (0, 0), (0, 0), (0, LANES - 3 * HPG))).reshape(L, D, N_KV * LANES)
    return jnp.swapaxes(w_in, 1, 2), wg.astype(BF)


def _row_tile(m):
    return 1024 if m % 1024 == 0 else 256


def _largest_divisor(n, cap):
    return max(d for d in range(1, cap + 1) if n % d == 0)


def _layer(l, xp, xs, mod_p, mod_s, P, rope_p, rope_s, cache_kv, page_table, state_win, state_rnn, state_conv):
    B, T, D = xp.shape
    M = B * T
    R = xs.shape[0]
    nb, n_pages = page_table.shape
    depth, n_phys, page = cache_kv.shape[:3]
    n_past = n_pages * page // CMP_BLOCK
    d_attn = N_HEADS * HEAD_DIM
    d_rnn = P["conv_w"].shape[2]
    o_gn = d_attn + 6 * N_KV * HEAD_DIM
    o_rx = o_gn + 3 * N_HEADS
    tm, tn = _row_tile(M), 512
    shp1, scp1, gap1, shp2, scp2, gap2 = mod_p
    shs1, scs1, gas1, shs2, scs2, gas2 = mod_s
    bmap = lambda j, i: (i // (T // tm), 0, j)
    g_ln = P["g_ln"][l]

    w2 = jnp.concatenate([P["w_phi"][l, 0], P["w_phi"][l, 1]], axis=-1)
    w2 = w2.reshape(CMP_BLOCK // 2, 2 * HEAD_DIM, 2 * HEAD_DIM).astype(BF)
    pe_t = jnp.concatenate([jnp.broadcast_to(P["pe_cmp"][l, j][:, None, :], (CMP_BLOCK, N_KV, HEAD_DIM))
                            for j in range(2)], axis=1)
    rnn = (P["conv_w"][l], P["conv_b"][l], P["w_rg"][l], P["b_rg"][l], P["lam"][l])

    hp = _ln_mod(xp, g_ln[0:1], scp1, shp1, tt=256).reshape(M, D)
    hs = _ln_mod(xs[None], g_ln[0:1], scs1, shs1, tt=R).reshape(R, D)
    w_in_t, w_gate = P["w_in_t"], P["w_gate"]
    zq_p, zq_s = _mm2(hp, hs, w_in_t, layer=l, tm=tm, tn=tn, w_rows=(0, o_gn))
    zg_p, zg_s = _mm2(hp, hs, w_gate, layer=l, tm=tm, tn=tn)
    zr_p, zr_s = _mm2(hp, hs, w_in_t, layer=l, tm=tm, tn=tn, w_rows=(o_rx, w_in_t.shape[1] - o_rx))

    q, kvrows, winrows, kvb = _qk_prep(zq_p.reshape(B, T, -1), rope_p[0], rope_p[1], P["qk_g"][l], tt=256)
    kcv = _compress(kvrows, pe_t, w2)
    o_nsa_p, w_out_bf, w_mlp1_bf, w_mlp2_bf = _nsa_prompt(
        q, kcv.reshape(B, T // CMP_BLOCK, 8 * HEAD_DIM), kvb, zg_p.reshape(B, T, -1), tq=256, layer=l,
        side_cast=(P["w_out"], P["w_mlp1"], P["w_mlp2"]))
    conv0 = jnp.zeros((B, CONV_W - 1, d_rnn), F32)
    h0 = jnp.zeros((B, 1, d_rnn), F32)
    y_rnn_p, h_last, conv_last = _rglru(zr_p.reshape(B, T, -1), conv0, h0, *rnn, d_rnn=d_rnn)

    qs, kvrows_s, winrows_s, _ = _qk_prep(zq_s[None], rope_s[0], rope_s[1], P["qk_g"][l], tt=R)
    q16 = jnp.pad(qs[0, :nb].reshape(nb, N_KV, HPG, HEAD_DIM), ((0, 0), (0, 0), (0, 16 - HPG), (0, 0)))
    g16 = jnp.pad(zg_s[:nb].reshape(nb, N_KV, LANES)[:, :, :3 * HPG].reshape(nb, N_KV, HPG, 3),
                  ((0, 0), (0, 0), (0, 16 - HPG), (0, LANES - 3)))
    kv_new = kvrows_s.reshape(R, 4, N_KV, HEAD_DIM)[:nb]
    win_new = winrows_s.reshape(R, 2, N_KV, HEAD_DIM)[:nb]
    new6 = jnp.concatenate([kv_new, win_new], axis=1).transpose(0, 2, 1, 3)
    cache5 = cache_kv.reshape(depth, n_phys, page, 4 * N_KV, HEAD_DIM)
    kcv_s = _compress_paged(cache5, page_table, l, pe_t, w2, pb=_largest_divisor(n_pages, 32))
    o_cmp, top = _dec_cmp(q16, kcv_s.reshape(nb, n_past, 8 * HEAD_DIM), t_pos=n_pages * page)
    top_i = top[:, :, :N_SELECT].reshape(nb * N_KV, N_SELECT)
    cache4 = cache_kv.reshape(depth, n_phys, page * 4 * N_KV, HEAD_DIM)
    n_win = state_win.shape[2]
    win3 = state_win.reshape(depth * nb, n_win * 2 * N_KV, HEAD_DIM)
    o16 = _dec_attn(q16, cache4, page_table, top_i, new6, win3, o_cmp, g16, layer=l, n_past=n_past)
    o_nsa_s = jnp.pad(o16[:, :, :HPG].reshape(nb, -1), ((0, R - nb), (0, 0))).astype(BF)
    pad_r = lambda a: jnp.pad(a, ((0, R - nb),) + ((0, 0),) * (a.ndim - 1))
    conv0_t = pad_r(state_conv[l]).transpose(1, 0, 2)
    y_rnn_s, h_new = _rglru_step(zr_s, conv0_t, pad_r(state_rnn[l]), *rnn, d_rnn=d_rnn)

    mg_p, mg_s = _merge2(o_nsa_p.reshape(M, -1), y_rnn_p.reshape(M, -1), zr_p, o_nsa_s, y_rnn_s, zr_s,
                         P["w_branch_nsa"], P["w_branch_rnn"], layer=l, col0=2 * d_rnn, tm=tm, tn=tn)
    x1p, x1s = _mm2(mg_p, mg_s, w_out_bf, layer=0, tm=tm, tn=tn, epi="resid",
                    res=(xp.reshape(M, D), gap1, bmap, xs, gas1))
    h2p = _ln_mod(x1p.reshape(B, T, D), g_ln[1:2], scp2, shp2, tt=256).reshape(M, D)
    h2s = _ln_mod(x1s[None], g_ln[1:2], scs2, shs2, tt=R).reshape(R, D)
    up, us = _mm2(h2p, h2s, w_mlp1_bf, layer=0, tm=tm, tn=1024, out_dtype=BF, epi="relu2")
    x2p = _mm(up, w_mlp2_bf, layer=0, tm=tm, tn=tn, tk=4096, epi="resid", x_res=x1p, ga=gap2,
              ga_map=lambda i, j, k: (i // (T // tm), 0, j))
    x2s = _mm(us, w_mlp2_bf, layer=0, tm=R, tn=1024, tk=4096, epi="resid", x_res=x1s, ga=gas2,
              ga_map=lambda i, j, k: (0, 0, j))

    n_keep = min(WINDOW, T)
    outs_p = (kvrows.reshape(B, T, 4, N_KV, HEAD_DIM),
              winrows.reshape(B, T, 2, N_KV, HEAD_DIM)[:, T - n_keep:],
              h_last.reshape(B, -1), conv_last)
    outs_s = (kv_new[:, None],
              jnp.concatenate([state_win[l][:, 1:], win_new[:, None]], axis=1),
              h_new[:nb],
              jnp.concatenate([state_conv[l][:, 1:], zr_s[:nb, None, :d_rnn]], axis=1))
    return x2p.reshape(B, T, D), x2s, outs_p, outs_s


def kernel(x_prompt, x_sample, cache_kv, state_win, state_rnn, state_conv, page_table, c_prompt, c_sample,
           w_ada, b_ada, g_ln, w_in, qk_g, w_phi, pe_cmp, conv_w, conv_b, w_rg, b_rg, lam,
           w_branch_nsa, w_branch_rnn, w_out, w_mlp1, w_mlp2):
    depth = w_ada.shape[0]
    B, T, D = x_prompt.shape
    nb = x_sample.shape[0]
    R = 16
    past_len = page_table.shape[1] * cache_kv.shape[2]

    c_all = jnp.zeros((2 * R, D), F32).at[:B].set(c_prompt).at[R:R + nb].set(c_sample)
    rope_p = _rope_tables(0.0, T)
    rope_s = tuple(jnp.broadcast_to(t, (R, HEAD_DIM)) for t in _rope_tables(float(past_len), 1))

    w_in_t, w_gate = _regroup_w_in(w_in)
    P = dict(g_ln=g_ln, w_in_t=w_in_t, w_gate=w_gate, qk_g=qk_g, w_phi=w_phi, pe_cmp=pe_cmp, conv_w=conv_w, conv_b=conv_b,
             w_rg=w_rg, b_rg=b_rg, lam=lam, w_branch_nsa=w_branch_nsa, w_branch_rnn=w_branch_rnn,
             w_out=w_out, w_mlp1=w_mlp1, w_mlp2=w_mlp2)
    yp = x_prompt
    ys = jnp.pad(x_sample.reshape(nb, D), ((0, R - nb), (0, 0)))
    outs = [[] for _ in range(8)]
    for l in range(depth):
        mod = _mm(c_all, w_ada, layer=l, tm=2 * R, tn=512, epi="silu_bias", bias=b_ada[l][None])
        mod = mod.reshape(2 * R, 6, D)
        mod_p = [mod[:B, k][:, None, :] for k in range(6)]
        mod_s = [mod[R:, k][None] for k in range(6)]
        yp, ys, (a1, a2, a3, a4), (b1, b2, b3, b4) = _layer(
            l, yp, ys, mod_p, mod_s, P, rope_p, rope_s, cache_kv, page_table, state_win, state_rnn, state_conv)
        for lst, v in zip(outs, (a1, b1, a2, b2, a3, b3, a4, b4)):
            lst.append(v)
    kv_p, kv_s, win_p, win_s, h_p, h_s, cv_p, cv_s = (jnp.stack(v) for v in outs)
    return (yp, ys[:nb].reshape(nb, 1, D), kv_p, kv_s, win_p, win_s, h_p, h_s, cv_p, cv_s)
```

```python
import functools
import math

import jax
import jax.numpy as jnp
from jax import lax
from jax.experimental import pallas as pl
from jax.experimental.pallas import tpu as pltpu

BF = jnp.bfloat16
F32 = jnp.float32

HEAD_DIM = 128
N_KV = 4
HPG = 4
N_HEADS = N_KV * HPG
CMP_BLOCK = 64
CMP_SHIFT = 6
N_SELECT = 16
WINDOW = 512
FORCED_SCORE = 1e4
RNN_BW = 128
CONV_W = 4
RG_C = 8.0
ROPE_THETA = 10000.0
EPS = 1e-6
NEG = -1e30
LANES = 128
VMEM_LIMIT = 56 * 1024 * 1024


def _cparams(sem):
    return pltpu.CompilerParams(dimension_semantics=sem, vmem_limit_bytes=VMEM_LIMIT)


def _dot(a, b):
    return jnp.dot(a, b, preferred_element_type=F32)


def _dot_nt(a, b):
    return lax.dot_general(a, b, (((1,), (1,)), ((), ())), preferred_element_type=F32)


def _rep_lanes(x, n):
    return x if n == LANES else jnp.concatenate([x] * (n // LANES), axis=1)


def _mm_kernel(*refs, nk, epi):
    if epi == "resid":
        a_ref, w_ref, x_ref, ga_ref, o_ref = refs[:5]
        scratch = refs[5:]
    elif epi == "silu_bias":
        a_ref, w_ref, bias_ref, o_ref = refs[:4]
        scratch = refs[4:]
    else:
        a_ref, w_ref, o_ref = refs[:3]
        scratch = refs[3:]

    def finish(acc):
        if epi == "relu2":
            r = jnp.maximum(acc, 0.0)
            acc = r * r
        elif epi == "resid":
            acc = x_ref[...] + ga_ref[...] * acc
        elif epi == "silu_bias":
            acc = acc + bias_ref[...]
        o_ref[...] = acc.astype(o_ref.dtype)

    a = a_ref[...]
    if epi == "silu_bias":
        a = (a * jax.nn.sigmoid(a)).astype(BF)
    part = _dot(a, w_ref[...].astype(BF))
    if nk == 1:
        finish(part)
    else:
        acc_ref = scratch[0]
        k = pl.program_id(2)

        @pl.when(k == 0)
        def _():
            acc_ref[...] = part

        @pl.when(k > 0)
        def _():
            acc_ref[...] += part

        @pl.when(k == nk - 1)
        def _():
            finish(acc_ref[...])


def _mm(a, w, *, tm, tn, tk=None, out_dtype=F32, epi=None, x_res=None, ga=None, ga_map=None, bias=None,
        layer=0):
    M, K = a.shape
    N = w.shape[2]
    tk = K if tk is None else tk
    nk = K // tk
    grid = (M // tm, N // tn, nk)
    in_specs = [pl.BlockSpec((tm, tk), lambda i, j, k: (i, k)),
                pl.BlockSpec((None, tk, tn), lambda i, j, k: (layer, k, j))]
    args = [a, w]
    if epi == "resid":
        in_specs.append(pl.BlockSpec((tm, tn), lambda i, j, k: (i, j)))
        in_specs.append(pl.BlockSpec((None, ga.shape[1], tn), ga_map))
        args += [x_res, ga]
    elif epi == "silu_bias":
        in_specs.append(pl.BlockSpec((1, tn), lambda i, j, k: (0, j)))
        args.append(bias)
    scratch =[pltpu.VMEM((tm, tn), F32)] if nk > 1 else []
    return pl.pallas_call(
        functools.partial(_mm_kernel, nk=nk, epi=epi),
        grid=grid, in_specs=in_specs,
        out_specs=pl.BlockSpec((tm, tn), lambda i, j, k: (i, j)),
        out_shape=jax.ShapeDtypeStruct((M, N), out_dtype),
        scratch_shapes=scratch,
        compiler_params=_cparams(("parallel", "parallel", "arbitrary")),
    )(*args)


def _mm_kres_kernel(a_ref, w_ref, x_ref, ga_ref, o_ref, acc_ref, *, nk):
    k, j = pl.program_id(1), pl.program_id(2)
    part = _dot(a_ref[...], w_ref[...])

    @pl.when(k == 0)
    def _():
        acc_ref[j] = part

    @pl.when((k > 0) & (k < nk - 1))
    def _():
        acc_ref[j] += part

    @pl.when(k == nk - 1)
    def _():
        o_ref[...] = x_ref[...] + ga_ref[...] * (acc_ref[j] + part)


def _mm_kres(a, w, x_res, ga, *, layer, rows_per_batch, tm, tn, tk):
    M, K = a.shape
    N = w.shape[2]
    nk, nj = K // tk, N // tn
    assert nk >= 2
    last = lambda k, j: jnp.where(k == nk - 1, j, 0)
    return pl.pallas_call(
        functools.partial(_mm_kres_kernel, nk=nk),
        grid=(M // tm, nk, nj),
        in_specs=[pl.BlockSpec((tm, tk), lambda i, k, j: (i, k)),
                  pl.BlockSpec((None, tk, tn), lambda i, k, j: (layer, k, j)),
                  pl.BlockSpec((tm, tn), lambda i, k, j: (i, last(k, j))),
                  pl.BlockSpec((None, 1, tn), lambda i, k, j: (i * tm // rows_per_batch, 0, last(k, j)))],
        out_specs=pl.BlockSpec((tm, tn), lambda i, k, j: (i, last(k, j))),
        out_shape=jax.ShapeDtypeStruct((M, N), F32),
        scratch_shapes=[pltpu.VMEM((nj, tm, tn), F32)],
        compiler_params=_cparams(("parallel", "arbitrary", "arbitrary")),
    )(a, w, x_res, ga)


def _cached_bf16(w_ref, wbf_ref, first):
    if wbf_ref is None:
        return w_ref[...]

    @pl.when(first)
    def _():
        wbf_ref[...] = w_ref[...].astype(BF)

    return wbf_ref[...]


def _mm2_kernel(*refs, epi, cast_w, w_transposed):
    n_in = 7 if epi == "resid" else 3
    a_ref, a2_ref, w_ref = refs[:3]
    o_ref, o2_ref = refs[n_in:n_in + 2]
    wbf_ref = refs[n_in + 2] if cast_w else None
    first = pl.program_id(1) == 0
    w = _cached_bf16(w_ref, wbf_ref, first)
    mul = _dot_nt if w_transposed else _dot

    def finish(acc, x_ref, ga_ref, out_ref):
        if epi == "relu2":
            r = jnp.maximum(acc, 0.0)
            acc = r * r
        elif epi == "resid":
            acc = x_ref[...] + ga_ref[...] * acc
        out_ref[...] = acc.astype(out_ref.dtype)

    extra = refs[3:7] if epi == "resid" else (None,) * 4
    finish(mul(a_ref[...], w), extra[0], extra[1], o_ref)

    @pl.when(first)
    def _():
        finish(mul(a2_ref[...], w), extra[2], extra[3], o2_ref)


def _mm2(a, a2, w, *, layer, tm, tn, out_dtype=F32, epi=None, res=None, w_rows=None):
    M, K = a.shape
    R2 = a2.shape[0]
    cast_w = w.dtype != BF
    if w_rows is None:
        N = w.shape[2]
        w_spec = pl.BlockSpec((None, K, tn), lambda j, i: (layer, 0, j))
        wbf_shape = (K, tn)
    else:
        row0, N = w_rows
        assert row0 % 8 == 0 and tn % 8 == 0
        w_spec = pl.BlockSpec((None, pl.Element(tn), pl.Element(K)),
                              lambda j, i: (layer, pl.multiple_of(row0 + j * tn, 8), 0))
        wbf_shape = (tn, K)
    in_specs = [pl.BlockSpec((tm, K), lambda j, i: (i, 0)),
                pl.BlockSpec((R2, K), lambda j, i: (0, 0)),
                w_spec]
    args = [a, a2, w]
    if epi == "resid":
        x, ga, ga_map, x2, ga2 = res
        in_specs += [pl.BlockSpec((tm, tn), lambda j, i: (i, j)),
                     pl.BlockSpec((None, ga.shape[1], tn), ga_map),
                     pl.BlockSpec((R2, tn), lambda j, i: (0, j)),
                     pl.BlockSpec((None, R2, tn), lambda j, i: (0, 0, j))]
        args += [x, ga, x2, ga2]
    return pl.pallas_call(
        functools.partial(_mm2_kernel, epi=epi, cast_w=cast_w, w_transposed=w_rows is not None),
        grid=(N // tn, M // tm), in_specs=in_specs,
        out_specs=[pl.BlockSpec((tm, tn), lambda j, i: (i, j)),
                   pl.BlockSpec((R2, tn), lambda j, i: (0, j))],
        out_shape=[jax.ShapeDtypeStruct((M, N), out_dtype), jax.ShapeDtypeStruct((R2, N), out_dtype)],
        scratch_shapes=[pltpu.VMEM(wbf_shape, BF)] if cast_w else [],
        compiler_params=_cparams(("arbitrary", "arbitrary")),
    )(*args)


def _merge2_kernel(an_ref, ar_ref, gn_ref, gr_ref, an2_ref, ar2_ref, gn2_ref, gr2_ref, wn_ref, wr_ref,
                   o_ref, o2_ref, wnb_ref, wrb_ref):
    first = pl.program_id(1) == 0
    wn = _cached_bf16(wn_ref, wnb_ref, first)
    wr = _cached_bf16(wr_ref, wrb_ref, first)

    def finish(a_n, a_r, g_n, g_r, out_ref):
        o = jax.nn.sigmoid(g_n[...]) * _dot(a_n[...], wn) + jax.nn.sigmoid(g_r[...]) * _dot(a_r[...], wr)
        out_ref[...] = o.astype(out_ref.dtype)

    finish(an_ref, ar_ref, gn_ref, gr_ref, o_ref)

    @pl.when(first)
    def _():
        finish(an2_ref, ar2_ref, gn2_ref, gr2_ref, o2_ref)


def _merge2(o_nsa, o_rnn, z, o_nsa2, o_rnn2, z2, w_nsa, w_rnn, *, layer, col0, tm, tn):
    M, K = o_nsa.shape
    R2 = o_nsa2.shape[0]
    D = w_nsa.shape[2]
    c0, c1 = col0 // tn, (col0 + D) // tn
    row = lambda j, i: (i, 0)
    one = lambda j, i: (0, 0)
    return pl.pallas_call(
        _merge2_kernel,
        grid=(D // tn, M // tm),
        in_specs=[pl.BlockSpec((tm, K), row), pl.BlockSpec((tm, K), row),
                  pl.BlockSpec((tm, tn), lambda j, i: (i, j + c0)),
                  pl.BlockSpec((tm, tn), lambda j, i: (i, j + c1)),
                  pl.BlockSpec((R2, K), one), pl.BlockSpec((R2, K), one),
                  pl.BlockSpec((R2, tn), lambda j, i: (0, j + c0)),
                  pl.BlockSpec((R2, tn), lambda j, i: (0, j + c1)),
                  pl.BlockSpec((None, K, tn), lambda j, i: (layer, 0, j)),
                  pl.BlockSpec((None, K, tn), lambda j, i: (layer, 0, j))],
        out_specs=[pl.BlockSpec((tm, tn), lambda j, i: (i, j)),
                   pl.BlockSpec((R2, tn), lambda j, i: (0, j))],
        out_shape=[jax.ShapeDtypeStruct((M, D), BF), jax.ShapeDtypeStruct((R2, D), BF)],
        scratch_shapes=[pltpu.VMEM((K, tn), BF), pltpu.VMEM((K, tn), BF)],
        compiler_params=_cparams(("arbitrary", "arbitrary")),
    )(o_nsa, o_rnn, z, z, o_nsa2, o_rnn2, z2, z2, w_nsa, w_rnn)


def _ln_kernel(x_ref, g_ref, sc_ref, sh_ref, o_ref):
    x = x_ref[...]
    ms = jnp.mean(x * x, axis=-1, keepdims=True)
    y = x * lax.rsqrt(ms + EPS) * g_ref[...]
    o_ref[...] = (y * (1.0 + sc_ref[...]) + sh_ref[...]).astype(o_ref.dtype)


def _ln_mod(x, g, sc, sh, *, tt):
    B, T, D = x.shape
    R = sc.shape[1]
    rr = tt if R == T else 1
    mod_map = (lambda b, i: (b, i, 0)) if R == T else (lambda b, i: (b, 0, 0))
    return pl.pallas_call(
        _ln_kernel,
        grid=(B, T // tt),
        in_specs=[pl.BlockSpec((None, tt, D), lambda b, i: (b, i, 0)),
                  pl.BlockSpec((1, D), lambda b, i: (0, 0)),
                  pl.BlockSpec((None, rr, D), mod_map),
                  pl.BlockSpec((None, rr, D), mod_map)],
        out_specs=pl.BlockSpec((None, tt, D), lambda b, i: (b, i, 0)),
        out_shape=jax.ShapeDtypeStruct((B, T, D), BF),
        compiler_params=_cparams(("parallel", "arbitrary")),
    )(x, g, sc, sh)


def _qk_kernel(z_ref, c_ref, s_ref, g_ref, q_ref, kv_ref, win_ref, kvb_ref, *, scale):
    cosf = c_ref[...]
    sinf = s_ref[...]

    def norm_rope(col, gi):
        x = z_ref[:, col:col + HEAD_DIM]
        ms = jnp.mean(x * x, axis=-1, keepdims=True)
        y = x * lax.rsqrt(ms + EPS) * g_ref[gi:gi + 1, :]
        return y * cosf + pltpu.roll(y, HEAD_DIM // 2, 1) * sinf

    nq = N_HEADS * HEAD_DIM
    grp = N_KV * HEAD_DIM
    for h in range(N_HEADS):
        q_ref[:, h * HEAD_DIM:(h + 1) * HEAD_DIM] = (norm_rope(h * HEAD_DIM, 0) * scale).astype(BF)
    for j in range(3):
        for gi in range(N_KV):
            kcol = nq + (2 * j) * grp + gi * HEAD_DIM
            vcol = nq + (2 * j + 1) * grp + gi * HEAD_DIM
            k = norm_rope(kcol, 1 + j)
            v = z_ref[:, vcol:vcol + HEAD_DIM]
            ko = (2 * j) * grp + gi * HEAD_DIM
            vo = (2 * j + 1) * grp + gi * HEAD_DIM
            kvb_ref[:, ko:ko + HEAD_DIM] = k.astype(BF)
            kvb_ref[:, vo:vo + HEAD_DIM] = v.astype(BF)
            tt = k.shape[0]
            if j < 2:
                planes = 4 * N_KV
                kv_ref[pl.ds(2 * j * N_KV + gi, tt, stride=planes), :] = k
                kv_ref[pl.ds((2 * j + 1) * N_KV + gi, tt, stride=planes), :] = v
            else:
                planes = 2 * N_KV
                win_ref[pl.ds(gi, tt, stride=planes), :] = k
                win_ref[pl.ds(N_KV + gi, tt, stride=planes), :] = v


def _qk_prep(z, cosf, sinf, qk_g, *, tt):
    B, T, _ = z.shape
    nq = N_HEADS * HEAD_DIM
    grp = N_KV * HEAD_DIM
    row = lambda b, i: (b, i, 0)
    return pl.pallas_call(
        functools.partial(_qk_kernel, scale=HEAD_DIM ** -0.5),
        grid=(B, T // tt),
        in_specs=[pl.BlockSpec((None, tt, z.shape[2]), row),
                  pl.BlockSpec((tt, HEAD_DIM), lambda b, i: (i, 0)),
                  pl.BlockSpec((tt, HEAD_DIM), lambda b, i: (i, 0)),
                  pl.BlockSpec((4, HEAD_DIM), lambda b, i: (0, 0))],
        out_specs=[pl.BlockSpec((None, tt, nq), row),
                   pl.BlockSpec((None, tt * 4 * N_KV, HEAD_DIM), row),
                   pl.BlockSpec((None, tt * 2 * N_KV, HEAD_DIM), row),
                   pl.BlockSpec((None, tt, 6 * grp), row)],
        out_shape=[jax.ShapeDtypeStruct((B, T, nq), BF),
                   jax.ShapeDtypeStruct((B, T * 4 * N_KV, HEAD_DIM), F32),
                   jax.ShapeDtypeStruct((B, T * 2 * N_KV, HEAD_DIM), F32),
                   jax.ShapeDtypeStruct((B, T, 6 * grp), BF)],
        compiler_params=_cparams(("parallel", "arbitrary")),
    )(z, cosf, sinf, qk_g)


def _rope_tables(pos0, T):
    half = HEAD_DIM // 2
    inv = ROPE_THETA ** (-jnp.arange(half, dtype=F32) / half)
    ang = (pos0 + jnp.arange(T, dtype=F32))[:, None] * inv[None, :]
    cos, sin = jnp.cos(ang), jnp.sin(ang)
    return jnp.concatenate([cos, cos], axis=1), jnp.concatenate([-sin, sin], axis=1)


def _compress_acc(load_pair, pe_ref, w_ref, rows):
    acc = jnp.zeros((rows, 2 * HEAD_DIM), F32)
    for lp in range(CMP_BLOCK // 2):
        xs = []
        for l in (2 * lp, 2 * lp + 1):
            x = load_pair(l) + pe_ref[l][None]
            xs.append(x.reshape(rows, HEAD_DIM).astype(BF))
        acc = acc + _dot(jnp.concatenate(xs, axis=1), w_ref[lp])
    is_k = (lax.broadcasted_iota(jnp.int32, (rows, HEAD_DIM), 0) % 8) < N_KV
    return jnp.where(is_k, acc[:, :HEAD_DIM], acc[:, HEAD_DIM:])


def _compress_kernel(x_ref, pe_ref, w_ref, o_ref, *, nblk):
    rows = nblk * 8
    out = _compress_acc(lambda l: x_ref[pl.ds(l, nblk, stride=CMP_BLOCK), :, :], pe_ref, w_ref, rows)
    o_ref[...] = out.reshape(nblk, 8, HEAD_DIM)


def _compress(kvrows, pe_t, w2):
    B = kvrows.shape[0]
    T = kvrows.shape[1] // (4 * N_KV)
    nblk = T // CMP_BLOCK
    x = kvrows.reshape(B, T, 4 * N_KV, HEAD_DIM)
    return pl.pallas_call(
        functools.partial(_compress_kernel, nblk=nblk),
        grid=(B,),
        in_specs=[pl.BlockSpec((None, T, 8, HEAD_DIM), lambda b: (b, 0, 0, 0)),
                  pl.BlockSpec((CMP_BLOCK, 8, HEAD_DIM), lambda b: (0, 0, 0)),
                  pl.BlockSpec((CMP_BLOCK // 2, 2 * HEAD_DIM, 2 * HEAD_DIM), lambda b: (0, 0, 0))],
        out_specs=pl.BlockSpec((None, nblk, 8, HEAD_DIM), lambda b: (b, 0, 0, 0)),
        out_shape=jax.ShapeDtypeStruct((B, nblk, 8, HEAD_DIM), F32),
        compiler_params=_cparams(("parallel",)),
    )(x, pe_t, w2)


def _compress_paged_kernel(pt_ref, *refs, pb):
    page_refs = refs[:pb]
    pe_ref, w_ref, o_ref = refs[pb:pb + 3]
    rows = pb * 2 * 8

    def load(l):
        tiles = []
        for p in range(pb):
            for h in range(2):
                tiles.append(page_refs[p][h * CMP_BLOCK + l])
        return jnp.stack(tiles, axis=0)

    out = _compress_acc(load, pe_ref, w_ref, rows)
    o_ref[...] = out.reshape(pb * 2, 8, HEAD_DIM)


def _compress_paged(cache5, page_table, layer, pe_t, w2, *, pb):
    n_phys, page = cache5.shape[1], cache5.shape[2]
    B, n_pages = page_table.shape
    assert page == 2 * CMP_BLOCK and n_pages % pb == 0

    def page_spec(p):
        return pl.BlockSpec((None, None, page, 8, HEAD_DIM),
                            lambda b, i, pt: (layer, pt[b, i * pb + p], 0, 0, 0))

    grid_spec = pltpu.PrefetchScalarGridSpec(
        num_scalar_prefetch=1,
        grid=(B, n_pages // pb),
        in_specs=[page_spec(p) for p in range(pb)] + [
            pl.BlockSpec((CMP_BLOCK, 8, HEAD_DIM), lambda b, i, pt: (0, 0, 0)),
            pl.BlockSpec((CMP_BLOCK // 2, 2 * HEAD_DIM, 2 * HEAD_DIM), lambda b, i, pt: (0, 0, 0))],
        out_specs=pl.BlockSpec((None, pb * 2, 8, HEAD_DIM), lambda b, i, pt: (b, i, 0, 0)),
    )
    return pl.pallas_call(
        functools.partial(_compress_paged_kernel, pb=pb),
        grid_spec=grid_spec,
        out_shape=jax.ShapeDtypeStruct((B, n_pages * 2, 8, HEAD_DIM), F32),
        compiler_params=_cparams(("parallel", "arbitrary")),
    )(page_table, *([cache5] * pb), pe_t, w2)


def _online_step(s2, v, m_ref, l_ref, acc_ref):
    tk = s2.shape[1]
    m_prev = m_ref[...]
    m_next = jnp.maximum(m_prev, jnp.max(s2, axis=1, keepdims=True))
    alpha = jnp.exp(m_prev - m_next)
    p = jnp.exp(s2 - _rep_lanes(m_next, tk))
    l_ref[...] = alpha * l_ref[...] + jnp.sum(p, axis=1, keepdims=True)
    m_ref[...] = m_next
    acc_ref[...] = alpha * acc_ref[...] + _dot(p.astype(BF), v)


def _nsa_kernel(*refs, tq, nblk, n_side):
    q_ref, kc_ref, vc_ref, ks_ref, vs_ref, kw_ref, vw_ref, g_ref = refs[:8]
    o_ref = refs[8 + n_side]
    m_ref, l_ref, acc_ref = refs[9 + 2 * n_side:]
    for src, dst in zip(refs[8:8 + n_side], refs[9 + n_side:9 + 2 * n_side]):
        dst[...] = src[...].astype(BF)

    qi = pl.program_id(2)
    q0 = qi * tq
    rows = HPG * tq
    q = q_ref[...]
    qcat = jnp.concatenate([q[:, h * HEAD_DIM:(h + 1) * HEAD_DIM] for h in range(HPG)], axis=0)

    kc = kc_ref[...].astype(BF)
    vc = vc_ref[...].astype(BF)
    t_b = q0 + lax.broadcasted_iota(jnp.int32, (nblk, tq), 1)
    blk = lax.broadcasted_iota(jnp.int32, (nblk, tq), 0)
    complete = jnp.concatenate([(blk + 1) * CMP_BLOCK - 1 <= t_b] * HPG, axis=1)
    s_t = jnp.where(complete, _dot_nt(kc, qcat), NEG)
    mx = jnp.max(s_t, axis=0, keepdims=True)
    e = jnp.where(complete, jnp.exp(s_t - mx), 0.0)
    d = jnp.sum(e, axis=0, keepdims=True)
    p_t = e / jnp.where(d > 0, d, 1.0)

    imp = p_t[:, 0:tq]
    for h in range(1, HPG):
        imp = imp + p_t[:, h * tq:(h + 1) * tq]
    cur = jnp.right_shift(t_b, CMP_SHIFT)
    forced = (blk == 0) | (blk == cur) | (blk == cur - 1)
    score = jnp.where(forced, FORCED_SCORE, imp)
    score = jnp.where(blk <= cur, score, -1.0)
    rank = jnp.zeros((nblk, tq), F32)
    for i in range(nblk):
        si = score[i:i + 1, :]
        beats = (si > score) | ((si == score) & (blk > i))
        rank = rank + jnp.where(beats, 1.0, 0.0)
    sel_t = jnp.where((rank < min(N_SELECT, nblk)) & (score > -0.5), 1.0, 0.0).astype(BF)

    eye = jnp.where(lax.broadcasted_iota(jnp.int32, (tq, tq), 0)
                    == lax.broadcasted_iota(jnp.int32, (tq, tq), 1), 1.0, 0.0).astype(BF)
    p_bf = p_t.astype(BF)
    p_rows = jnp.concatenate([_dot_nt(eye, p_bf[:, h * tq:(h + 1) * tq]) for h in range(HPG)], axis=0)
    o_cmp = _dot(p_rows.astype(BF), vc)
    sel = _dot_nt(eye, sel_t).astype(BF)

    t_k = q0 + lax.broadcasted_iota(jnp.int32, (tq, tq), 0)
    k_off = lax.broadcasted_iota(jnp.int32, (tq, tq), 1)
    e_blk = lax.broadcasted_iota(jnp.int32, (nblk, tq), 0)
    e_off = lax.broadcasted_iota(jnp.int32, (nblk, tq), 1)

    def reset():
        m_ref[...] = jnp.full((rows, LANES), NEG, F32)
        l_ref[...] = jnp.zeros((rows, LANES), F32)
        acc_ref[...] = jnp.zeros((rows, HEAD_DIM), F32)

    def attend(c, k_ref, v_ref, mask):
        start = pl.multiple_of(c * tq, tq)
        k = k_ref[pl.ds(start, tq), :]
        v = v_ref[pl.ds(start, tq), :]
        s = jnp.where(mask[None], _dot_nt(qcat, k).reshape(HPG, tq, tq), NEG)
        _online_step(s.reshape(rows, tq), v, m_ref, l_ref, acc_ref)

    def sel_body(c, carry):
        expand = jnp.where(jnp.right_shift(c * tq + e_off, CMP_SHIFT) == e_blk, 1.0, 0.0).astype(BF)
        chosen = _dot(sel, expand) > 0.5
        attend(c, ks_ref, vs_ref, chosen & (c * tq + k_off <= t_k))
        return carry

    reset()
    lax.fori_loop(0, qi + 1, sel_body, 0)
    o_sel = acc_ref[...] / l_ref[...]

    def win_body(c, carry):
        dist = t_k - (c * tq + k_off)
        attend(c, kw_ref, vw_ref, (dist >= 0) & (dist < WINDOW))
        return carry

    reset()
    lax.fori_loop(jnp.maximum(qi - WINDOW // tq, 0), qi + 1, win_body, 0)
    o_win = acc_ref[...] / l_ref[...]

    g = jax.nn.sigmoid(g_ref[...])
    for h in range(HPG):
        r = slice(h * tq, (h + 1) * tq)
        o = (g[:, 3 * h:3 * h + 1] * o_cmp[r] + g[:, 3 * h + 1:3 * h + 2] * o_sel[r]
             + g[:, 3 * h + 2:3 * h + 3] * o_win[r])
        o_ref[:, h * HEAD_DIM:(h + 1) * HEAD_DIM] = o.astype(BF)


def _nsa_prompt(q, kcv, kvb, zg, *, tq, layer, side_cast):
    B, T, _ = q.shape
    nblk = kcv.shape[1]
    n_q = T // tq
    n_steps = B * N_KV * n_q
    col = lambda c: pl.BlockSpec((None, T, HEAD_DIM), lambda b, g, i, c=c: (b, 0, c + g))
    rows = HPG * tq
    in_specs = [pl.BlockSpec((None, tq, HPG * HEAD_DIM), lambda b, g, i: (b, i, g)),
                pl.BlockSpec((None, nblk, HEAD_DIM), lambda b, g, i: (b, 0, g)),
                pl.BlockSpec((None, nblk, HEAD_DIM), lambda b, g, i: (b, 0, N_KV + g)),
                col(2 * N_KV), col(3 * N_KV), col(4 * N_KV), col(5 * N_KV),
                pl.BlockSpec((None, tq, LANES), lambda b, g, i: (b, i, g))]
    out_specs = [pl.BlockSpec((None, tq, HPG * HEAD_DIM), lambda b, g, i: (b, i, g))]
    out_shape = [jax.ShapeDtypeStruct((B, T, N_HEADS * HEAD_DIM), BF)]
    for p in side_cast:
        _, n_rows, n_cols = p.shape
        rb = next(r for r in range(16, n_rows + 1, 16) if n_rows % r == 0 and n_rows // r <= n_steps)
        last = n_rows // rb - 1
        step = lambda b, g, i, last=last: jnp.minimum((b * N_KV + g) * n_q + i, last)
        in_specs.append(pl.BlockSpec((None, rb, n_cols), lambda b, g, i, step=step: (layer, step(b, g, i), 0)))
        out_specs.append(pl.BlockSpec((None, rb, n_cols), lambda b, g, i, step=step: (0, step(b, g, i), 0)))
        out_shape.append(jax.ShapeDtypeStruct((1, n_rows, n_cols), BF))
    return pl.pallas_call(
        functools.partial(_nsa_kernel, tq=tq, nblk=nblk, n_side=len(side_cast)),
        grid=(B, N_KV, n_q), in_specs=in_specs, out_specs=out_specs, out_shape=out_shape,
        scratch_shapes=[pltpu.VMEM((rows, LANES), F32), pltpu.VMEM((rows, LANES), F32),
                        pltpu.VMEM((rows, HEAD_DIM), F32)],
        compiler_params=_cparams(("arbitrary", "arbitrary", "arbitrary")),
    )(q, kcv, kcv, kvb, kvb, kvb, kvb, zg, *side_cast)


def _softplus(y):
    return jnp.maximum(y, 0.0) + jnp.log1p(jnp.exp(-jnp.abs(y)))


def _rg_gates(u, w_ref, brg_ref, lam_ref):
    ub = u.astype(BF)
    r = jax.nn.sigmoid(_dot(ub, w_ref[0].astype(BF)) + brg_ref[0:1, :])
    i = jax.nn.sigmoid(_dot(ub, w_ref[1].astype(BF)) + brg_ref[1:2, :])
    log_a = -RG_C * r * _softplus(-lam_ref[...])
    a = jnp.exp(log_a)
    th = jnp.tanh(log_a)
    b = jnp.sqrt(-2.0 * th / (1.0 - th)) * i * u
    return a, b


def _rglru_kernel(xr_ref, zg_ref, c0_ref, h0_ref, cw_ref, cb_ref, w_ref, brg_ref, lam_ref,
                  y_ref, hl_ref, cl_ref, xp_ref, a_ref, b_ref, h_ref, *, T):
    pad = 8
    xp_ref[0:pad, :] = jnp.zeros((pad, RNN_BW), F32)
    xp_ref[pad - (CONV_W - 1):pad, :] = c0_ref[...]
    xp_ref[pad:pad + T, :] = xr_ref[...]
    u = cb_ref[...] + xp_ref[pad:pad + T, :] * cw_ref[CONV_W - 1:CONV_W, :]
    for k in range(CONV_W - 1):
        off = pad - (CONV_W - 1) + k
        u = u + xp_ref[off:off + T, :] * cw_ref[k:k + 1, :]
    cl_ref[...] = xp_ref[pad + T - (CONV_W - 1):pad + T, :]

    a, b = _rg_gates(u, w_ref, brg_ref, lam_ref)
    a_ref[0:T, :] = a
    b_ref[0:T, :] = b

    levels = []
    n, off = T, 0
    while True:
        levels.append((n, off))
        if n <= 8:
            break
        off += n
        n //= 8
    for (n, off), (n2, off2) in zip(levels[:-1], levels[1:]):
        A = a_ref[pl.ds(off, n2, stride=8), :]
        Bv = b_ref[pl.ds(off, n2, stride=8), :]
        for r in range(1, 8):
            ar = a_ref[pl.ds(off + r, n2, stride=8), :]
            br = b_ref[pl.ds(off + r, n2, stride=8), :]
            Bv = ar * Bv + br
            A = ar * A
        a_ref[off2:off2 + n2, :] = A
        b_ref[off2:off2 + n2, :] = Bv
    n, off = levels[-1]
    h = h0_ref[...]
    for t in range(n):
        h = a_ref[off + t:off + t + 1, :] * h + b_ref[off + t:off + t + 1, :]
        h_ref[off + 8 + t:off + 9 + t, :] = h
    for (n, off), (n2, off2) in reversed(list(zip(levels[:-1], levels[1:]))):
        h_ref[off2 + 7:off2 + 8, :] = h0_ref[...]
        hp = h_ref[off2 + 7:off2 + 7 + n2, :]
        for r in range(8):
            ar = a_ref[pl.ds(off + r, n2, stride=8), :]
            br = b_ref[pl.ds(off + r, n2, stride=8), :]
            hp = ar * hp + br
            h_ref[pl.ds(off + 8 + r, n2, stride=8), :] = hp
    hfull = h_ref[8:8 + T, :]
    y_ref[...] = (jax.nn.gelu(zg_ref[...]) * hfull).astype(BF)
    hl_ref[...] = h_ref[8 + T - 1:8 + T, :]


def _rglru(zr, conv0, h0, conv_w, conv_b, w_rg, b_rg, lam, *, d_rnn):
    B, T, _ = zr.shape
    nb = d_rnn // RNN_BW
    assert T % 8 == 0
    total = T + T // 4 + 64
    return pl.pallas_call(
        functools.partial(_rglru_kernel, T=T),
        grid=(B, nb),
        in_specs=[pl.BlockSpec((None, T, RNN_BW), lambda b, n: (b, 0, n)),
                  pl.BlockSpec((None, T, RNN_BW), lambda b, n: (b, 0, nb + n)),
                  pl.BlockSpec((None, CONV_W - 1, RNN_BW), lambda b, n: (b, 0, n)),
                  pl.BlockSpec((None, 1, RNN_BW), lambda b, n: (b, 0, n)),
                  pl.BlockSpec((CONV_W, RNN_BW), lambda b, n: (0, n)),
                  pl.BlockSpec((1, RNN_BW), lambda b, n: (0, n)),
                  pl.BlockSpec((2, None, RNN_BW, RNN_BW), lambda b, n: (0, n, 0, 0)),
                  pl.BlockSpec((2, RNN_BW), lambda b, n: (0, n)),
                  pl.BlockSpec((1, RNN_BW), lambda b, n: (0, n))],
        out_specs=[pl.BlockSpec((None, T, RNN_BW), lambda b, n: (b, 0, n)),
                   pl.BlockSpec((None, 1, RNN_BW), lambda b, n: (b, 0, n)),
                   pl.BlockSpec((None, CONV_W - 1, RNN_BW), lambda b, n: (b, 0, n))],
        out_shape=[jax.ShapeDtypeStruct((B, T, d_rnn), BF),
                   jax.ShapeDtypeStruct((B, 1, d_rnn), F32),
                   jax.ShapeDtypeStruct((B, CONV_W - 1, d_rnn), F32)],
        scratch_shapes=[pltpu.VMEM((T + 8, RNN_BW), F32), pltpu.VMEM((total, RNN_BW), F32),
                        pltpu.VMEM((total, RNN_BW), F32), pltpu.VMEM((total + 8, RNN_BW), F32)],
        compiler_params=_cparams(("parallel", "arbitrary")),
    )(zr, zr, conv0, h0, conv_w, conv_b[None], w_rg, b_rg, lam[None])


def _rglru_step_kernel(xr_ref, zg_ref, c0_ref, h0_ref, cw_ref, cb_ref, w_ref, brg_ref, lam_ref,
                       y_ref, hl_ref):
    u = cb_ref[...] + xr_ref[...] * cw_ref[CONV_W - 1:CONV_W, :]
    for k in range(CONV_W - 1):
        u = u + c0_ref[k] * cw_ref[k:k + 1, :]
    a, b = _rg_gates(u, w_ref, brg_ref, lam_ref)
    h = a * h0_ref[...] + b
    hl_ref[...] = h
    y_ref[...] = (jax.nn.gelu(zg_ref[...]) * h).astype(BF)


def _rglru_step(zr, conv0_t, h0, conv_w, conv_b, w_rg, b_rg, lam, *, d_rnn):
    R = zr.shape[0]
    nb = d_rnn // RNN_BW
    return pl.pallas_call(
        _rglru_step_kernel,
        grid=(nb,),
        in_specs=[pl.BlockSpec((R, RNN_BW), lambda n: (0, n)),
                  pl.BlockSpec((R, RNN_BW), lambda n: (0, nb + n)),
                  pl.BlockSpec((CONV_W - 1, R, RNN_BW), lambda n: (0, 0, n)),
                  pl.BlockSpec((R, RNN_BW), lambda n: (0, n)),
                  pl.BlockSpec((CONV_W, RNN_BW), lambda n: (0, n)),
                  pl.BlockSpec((1, RNN_BW), lambda n: (0, n)),
                  pl.BlockSpec((2, None, RNN_BW, RNN_BW), lambda n: (0, n, 0, 0)),
                  pl.BlockSpec((2, RNN_BW), lambda n: (0, n)),
                  pl.BlockSpec((1, RNN_BW), lambda n: (0, n))],
        out_specs=[pl.BlockSpec((R, RNN_BW), lambda n: (0, n)),
                   pl.BlockSpec((R, RNN_BW), lambda n: (0, n))],
        out_shape=[jax.ShapeDtypeStruct((R, d_rnn), BF), jax.ShapeDtypeStruct((R, d_rnn), F32)],
        compiler_params=_cparams(("arbitrary",)),
    )(zr, zr, conv0_t, h0, conv_w, conv_b[None], w_rg, b_rg, lam[None])


def _dec_cmp_kernel(q_ref, kcv_ref, o_ref, idx_ref, *, n_past, t_pos):
    nlan = LANES * ((n_past + 1 + LANES - 1) // LANES)
    blk = lax.broadcasted_iota(jnp.int32, (1, n_past), 1)
    complete = (blk + 1) * CMP_BLOCK - 1 <= t_pos
    imps = []
    for g in range(N_KV):
        kc = kcv_ref[:, g * HEAD_DIM:(g + 1) * HEAD_DIM].astype(BF)
        vc = kcv_ref[:, (N_KV + g) * HEAD_DIM:(N_KV + g + 1) * HEAD_DIM].astype(BF)
        s = jnp.where(complete, _dot_nt(q_ref[g], kc), NEG)
        mx = jnp.max(s, axis=-1, keepdims=True)
        e = jnp.where(complete, jnp.exp(s - mx), 0.0)
        d = jnp.sum(e, axis=-1, keepdims=True)
        p = e / jnp.where(d > 0, d, 1.0)
        o_ref[g] = _dot(p.astype(BF), vc)
        head = lax.broadcasted_iota(jnp.int32, p.shape, 0) < HPG
        imps.append(jnp.sum(jnp.where(head, p, 0.0), axis=0, keepdims=True))
    imp = jnp.concatenate(imps, axis=0)

    cur = t_pos // CMP_BLOCK
    imp_all = jnp.concatenate([imp, jnp.zeros((N_KV, nlan - n_past), F32)], axis=1)
    blk_all = lax.broadcasted_iota(jnp.int32, (N_KV, nlan), 1)
    forced = (blk_all == 0) | (blk_all == cur) | (blk_all == cur - 1)
    score = jnp.where(forced, FORCED_SCORE, imp_all)
    score = jnp.where(blk_all <= cur, score, -1.0)
    score = jnp.where(blk_all <= n_past, score, -2.0)
    blk_f = blk_all.astype(F32)
    slot = lax.broadcasted_iota(jnp.int32, (N_KV, LANES), 1)
    res = jnp.full((N_KV, LANES), float(n_past), F32)
    for k in range(min(N_SELECT, n_past + 1)):
        top = jnp.max(score, axis=-1, keepdims=True)
        arg = jnp.min(jnp.where(score == top, blk_f, float(nlan)), axis=-1, keepdims=True)
        res = jnp.where(slot == k, jnp.where(top > -0.5, arg, float(n_past)), res)
        score = jnp.where(blk_f == arg, -3.0, score)
    idx_ref[...] = res.astype(jnp.int32)


def _dec_cmp(q16, kcv, *, t_pos):
    B = q16.shape[0]
    n_past = kcv.shape[1]
    return pl.pallas_call(
        functools.partial(_dec_cmp_kernel, n_past=n_past, t_pos=t_pos),
        grid=(B,),
        in_specs=[pl.BlockSpec((None, N_KV, 16, HEAD_DIM), lambda b: (b, 0, 0, 0)),
                  pl.BlockSpec((None, n_past, 2 * N_KV * HEAD_DIM), lambda b: (b, 0, 0))],
        out_specs=[pl.BlockSpec((None, N_KV, 16, HEAD_DIM), lambda b: (b, 0, 0, 0)),
                   pl.BlockSpec((None, N_KV, LANES), lambda b: (b, 0, 0))],
        out_shape=[jax.ShapeDtypeStruct((B, N_KV, 16, HEAD_DIM), F32),
                   jax.ShapeDtypeStruct((B, N_KV, LANES), jnp.int32)],
        compiler_params=_cparams(("parallel",)),
    )(q16, kcv)


def _dec_attn_kernel(pt_ref, ti_ref, q_ref, *refs, n_past, per_step):
    cache_refs = refs[:N_KV * per_step]
    new_ref, win_ref, oc_ref, g_ref, o_ref, m_ref, l_ref, acc_ref = refs[N_KV * per_step:]
    b, s = pl.program_id(0), pl.program_id(1)
    planes = 4 * N_KV

    @pl.when(s == 0)
    def _():
        m_ref[...] = jnp.full(m_ref.shape, NEG, F32)
        l_ref[...] = jnp.zeros(l_ref.shape, F32)
        acc_ref[...] = jnp.zeros(acc_ref.shape, F32)

    n_keys = per_step * CMP_BLOCK
    key_slot = lax.broadcasted_iota(jnp.int32, (1, n_keys), 1) // CMP_BLOCK
    for g in range(N_KV):
        ks, vs = [], []
        key_blk = jnp.full((1, n_keys), n_past, jnp.int32)
        for u in range(per_step):
            ref = cache_refs[u * N_KV + g]
            ks.append(ref[pl.ds(2 * N_KV + g, CMP_BLOCK, stride=planes), :].astype(BF))
            vs.append(ref[pl.ds(3 * N_KV + g, CMP_BLOCK, stride=planes), :].astype(BF))
            key_blk = jnp.where(key_slot == u, ti_ref[b * N_KV + g, s * per_step + u], key_blk)
        live = key_blk < n_past
        sc =jnp.where(live, _dot_nt(q_ref[g], jnp.concatenate(ks, axis=0)), NEG)
        m_prev = m_ref[g]
        m_next = jnp.maximum(m_prev, jnp.max(sc, axis=1, keepdims=True))
        alpha = jnp.exp(m_prev - m_next)
        p = jnp.where(live, jnp.exp(sc - m_next[:, :n_keys]), 0.0)
        l_ref[g] = alpha * l_ref[g] + jnp.sum(p, axis=1, keepdims=True)
        m_ref[g] = m_next
        acc_ref[g] = alpha * acc_ref[g] + _dot(p.astype(BF), jnp.concatenate(vs, axis=0))

    @pl.when(s == pl.num_programs(1) - 1)
    def _():
        n_win = win_ref.shape[0] // (2 * N_KV)
        keep = lax.broadcasted_iota(jnp.int32, (1, n_win), 1) >= n_win + 1 - WINDOW
        for g in range(N_KV):
            q = q_ref[g]
            qf = q.astype(F32)

            def self_score(k_row):
                return jnp.sum(qf * k_row.astype(BF).astype(F32), axis=1, keepdims=True)

            s_self = self_score(new_ref[g, 2:3, :])
            v_self = new_ref[g, 3:4, :].astype(BF).astype(F32)
            m_prev = m_ref[g]
            m_fin = jnp.maximum(m_prev, s_self)
            alpha = jnp.exp(m_prev - m_fin)
            p_self = jnp.exp(s_self - m_fin)
            o_sel = (alpha * acc_ref[g] + p_self * v_self) / (alpha * l_ref[g] + p_self)

            kw = win_ref[pl.ds(g, n_win, stride=2 * N_KV), :].astype(BF)
            vw = win_ref[pl.ds(N_KV + g, n_win, stride=2 * N_KV), :].astype(BF)
            sw = jnp.where(keep, _dot_nt(q, kw), NEG)
            sw_self = self_score(new_ref[g, 4:5, :])
            mw = jnp.maximum(jnp.max(sw, axis=1, keepdims=True), sw_self)
            pw = jnp.where(keep, jnp.exp(sw - mw), 0.0)
            pw_self = jnp.exp(sw_self - mw)
            num = _dot(pw.astype(BF), vw) + pw_self * new_ref[g, 5:6, :].astype(BF).astype(F32)
            o_win = num / (jnp.sum(pw, axis=1, keepdims=True) + pw_self)

            gt = jax.nn.sigmoid(g_ref[g])
            o_ref[g] = gt[:, 0:1] * oc_ref[g] + gt[:, 1:2] * o_sel + gt[:, 2:3] * o_win


def _dec_attn(q16, cache4, page_table, top_i, newkv, win3, o_cmp, g16, *, layer, n_past):
    B = q16.shape[0]
    n_slots = top_i.shape[1]
    per_step = 2 if n_slots % 2 == 0 else 1
    half_rows = CMP_BLOCK * 4 * N_KV

    def half_page(u, g):
        def index(b, s, pt, ti):
            blk = jnp.minimum(ti[b * N_KV + g, s * per_step + u], n_past - 1)
            return (layer, pt[b, blk // 2], blk % 2, 0)
        return pl.BlockSpec((None, None, half_rows, HEAD_DIM), index)

    per_seq = lambda b, s, pt, ti: (b, 0, 0, 0)
    grid_spec = pltpu.PrefetchScalarGridSpec(
        num_scalar_prefetch=2,
        grid=(B, n_slots // per_step),
        in_specs=[pl.BlockSpec((None, N_KV, 16, HEAD_DIM), per_seq)]
        + [half_page(u, g) for u in range(per_step) for g in range(N_KV)]
        + [pl.BlockSpec((None, N_KV, 6, HEAD_DIM), per_seq),
           pl.BlockSpec((None, win3.shape[1], HEAD_DIM), lambda b, s, pt, ti: (layer * B + b, 0, 0)),
           pl.BlockSpec((None, N_KV, 16, HEAD_DIM), per_seq),
           pl.BlockSpec((None, N_KV, 16, LANES), per_seq)],
        out_specs=pl.BlockSpec((None, N_KV, 16, HEAD_DIM), per_seq),
        scratch_shapes=[pltpu.VMEM((N_KV, 16, LANES), F32), pltpu.VMEM((N_KV, 16, LANES), F32),
                        pltpu.VMEM((N_KV, 16, HEAD_DIM), F32)],
    )
    return pl.pallas_call(
        functools.partial(_dec_attn_kernel, n_past=n_past, per_step=per_step),
        grid_spec=grid_spec,
        out_shape=jax.ShapeDtypeStruct((B, N_KV, 16, HEAD_DIM), F32),
        compiler_params=_cparams(("parallel", "arbitrary")),
    )(page_table, top_i, q16, *([cache4] * (N_KV * per_step)), newkv, win3, o_cmp, g16)


def _regroup_w_in(w_in):
    L, D, _ = w_in.shape
    o_gn = N_HEADS * HEAD_DIM + 6 * N_KV * HEAD_DIM
    o_rx = o_gn + 3 * N_HEADS
    wg = w_in[:, :, o_gn:o_rx].reshape(L, D, N_KV, 3 * HPG)
    wg = jnp.pad(wg, ((0, 0), (0, 0), (0, 0), (0, LANES - 3 * HPG))).reshape(L, D, N_KV * LANES)
    return jnp.swapaxes(w_in, 1, 2), wg.astype(BF)


def _row_tile(m):
    return 1024 if m % 1024 == 0 else 256


def _largest_divisor(n, cap):
    return max(d for d in range(1, cap + 1) if n % d == 0)


def _layer(l, xp, xs, mod_p, mod_s, P, rope_p, rope_s, cache_kv, page_table, state_win, state_rnn, state_conv):
    B, T, D = xp.shape
    M = B * T
    R = xs.shape[0]
    nb, n_pages = page_table.shape
    depth, n_phys, page = cache_kv.shape[:3]
    n_past = n_pages * page // CMP_BLOCK
    d_attn = N_HEADS * HEAD_DIM
    d_rnn = P["conv_w"].shape[2]
    o_gn = d_attn + 6 * N_KV * HEAD_DIM
    o_rx = o_gn + 3 * N_HEADS
    tm, tn = _row_tile(M), 512
    shp1, scp1, gap1, shp2, scp2, gap2 = mod_p
    shs1, scs1, gas1, shs2, scs2, gas2 = mod_s
    bmap = lambda j, i: (i // (T // tm), 0, j)
    g_ln = P["g_ln"][l]

    w2 = jnp.concatenate([P["w_phi"][l, 0], P["w_phi"][l, 1]], axis=-1)
    w2 = w2.reshape(CMP_BLOCK // 2, 2 * HEAD_DIM, 2 * HEAD_DIM).astype(BF)
    pe_t = jnp.concatenate([jnp.broadcast_to(P["pe_cmp"][l, j][:, None, :], (CMP_BLOCK, N_KV, HEAD_DIM))
                            for j in range(2)], axis=1)
    rnn = (P["conv_w"][l], P["conv_b"][l], P["w_rg"][l], P["b_rg"][l], P["lam"][l])

    hp = _ln_mod(xp, g_ln[0:1], scp1, shp1, tt=256).reshape(M, D)
    hs = _ln_mod(xs[None], g_ln[0:1], scs1, shs1, tt=R).reshape(R, D)
    w_in_t, w_gate = P["w_in_t"], P["w_gate"]
    zq_p, zq_s = _mm2(hp, hs, w_in_t, layer=l, tm=tm, tn=tn, w_rows=(0, o_gn))
    zg_p, zg_s = _mm2(hp, hs, w_gate, layer=l, tm=tm, tn=tn)
    zr_p, zr_s = _mm2(hp, hs, w_in_t, layer=l, tm=tm, tn=tn, w_rows=(o_rx, w_in_t.shape[1] - o_rx))

    q, kvrows, winrows, kvb = _qk_prep(zq_p.reshape(B, T, -1), rope_p[0], rope_p[1], P["qk_g"][l], tt=256)
    kcv = _compress(kvrows, pe_t, w2)
    o_nsa_p, w_out_bf, w_mlp1_bf, w_mlp2_bf = _nsa_prompt(
        q, kcv.reshape(B, T // CMP_BLOCK, 8 * HEAD_DIM), kvb, zg_p.reshape(B, T, -1), tq=256, layer=l,
        side_cast=(P["w_out"], P["w_mlp1"], P["w_mlp2"]))
    conv0 = jnp.zeros((B, CONV_W - 1, d_rnn), F32)
    h0 = jnp.zeros((B, 1, d_rnn), F32)
    y_rnn_p, h_last, conv_last = _rglru(zr_p.reshape(B, T, -1), conv0, h0, *rnn, d_rnn=d_rnn)

    qs, kvrows_s, winrows_s, _ = _qk_prep(zq_s[None], rope_s[0], rope_s[1], P["qk_g"][l], tt=R)
    q16 = jnp.pad(qs[0, :nb].reshape(nb, N_KV, HPG, HEAD_DIM), ((0, 0), (0, 0), (0, 16 - HPG), (0, 0)))
    g16 = jnp.pad(zg_s[:nb].reshape(nb, N_KV, LANES)[:, :, :3 * HPG].reshape(nb, N_KV, HPG, 3),
                  ((0, 0), (0, 0), (0, 16 - HPG), (0, LANES - 3)))
    kv_new = kvrows_s.reshape(R, 4, N_KV, HEAD_DIM)[:nb]
    win_new = winrows_s.reshape(R, 2, N_KV, HEAD_DIM)[:nb]
    new6 = jnp.concatenate([kv_new, win_new], axis=1).transpose(0, 2, 1, 3)
    cache5 = cache_kv.reshape(depth, n_phys, page, 4 * N_KV, HEAD_DIM)
    kcv_s = _compress_paged(cache5, page_table, l, pe_t, w2, pb=_largest_divisor(n_pages, 32))
    o_cmp, top = _dec_cmp(q16, kcv_s.reshape(nb, n_past, 8 * HEAD_DIM), t_pos=n_pages * page)
    top_i = top[:, :, :N_SELECT].reshape(nb * N_KV, N_SELECT)
    cache4 = cache_kv.reshape(depth, n_phys, page * 4 * N_KV, HEAD_DIM)
    n_win = state_win.shape[2]
    win3 = state_win.reshape(depth * nb, n_win * 2 * N_KV, HEAD_DIM)
    o16 = _dec_attn(q16, cache4, page_table, top_i, new6, win3, o_cmp, g16, layer=l, n_past=n_past)
    o_nsa_s = jnp.pad(o16[:, :, :HPG].reshape(nb, -1), ((0, R - nb), (0, 0))).astype(BF)
    pad_r = lambda a: jnp.pad(a, ((0, R - nb),) + ((0, 0),) * (a.ndim - 1))
    conv0_t = pad_r(state_conv[l]).transpose(1, 0, 2)
    y_rnn_s, h_new = _rglru_step(zr_s, conv0_t, pad_r(state_rnn[l]), *rnn, d_rnn=d_rnn)

    mg_p, mg_s = _merge2(o_nsa_p.reshape(M, -1), y_rnn_p.reshape(M, -1), zr_p, o_nsa_s, y_rnn_s, zr_s,
                         P["w_branch_nsa"], P["w_branch_rnn"], layer=l, col0=2 * d_rnn, tm=tm, tn=tn)
    x1p, x1s = _mm2(mg_p, mg_s, w_out_bf, layer=0, tm=tm, tn=tn, epi="resid",
                    res=(xp.reshape(M, D), gap1, bmap, xs, gas1))
    h2p = _ln_mod(x1p.reshape(B, T, D), g_ln[1:2], scp2, shp2, tt=256).reshape(M, D)
    h2s = _ln_mod(x1s[None], g_ln[1:2], scs2, shs2, tt=R).reshape(R, D)
    up, us = _mm2(h2p, h2s, w_mlp1_bf, layer=0, tm=tm, tn=1024, out_dtype=BF, epi="relu2")
    x2p = _mm_kres(up, w_mlp2_bf, x1p, gap2, layer=0, rows_per_batch=T, tm=tm, tn=tn, tk=4096)
    x2s = _mm(us, w_mlp2_bf, layer=0, tm=R, tn=1024, tk=4096, epi="resid", x_res=x1s, ga=gas2,
              ga_map=lambda i, j, k: (0, 0, j))

    n_keep = min(WINDOW, T)
    outs_p = (kvrows.reshape(B, T, 4, N_KV, HEAD_DIM),
              winrows.reshape(B, T, 2, N_KV, HEAD_DIM)[:, T - n_keep:],
              h_last.reshape(B, -1), conv_last)
    outs_s = (kv_new[:, None],
              jnp.concatenate([state_win[l][:, 1:], win_new[:, None]], axis=1),
              h_new[:nb],
              jnp.concatenate([state_conv[l][:, 1:], zr_s[:nb, None, :d_rnn]], axis=1))
    return x2p.reshape(B, T, D), x2s, outs_p, outs_s


def kernel(x_prompt, x_sample, cache_kv, state_win, state_rnn, state_conv, page_table, c_prompt, c_sample,
           w_ada, b_ada, g_ln, w_in, qk_g, w_phi, pe_cmp, conv_w, conv_b, w_rg, b_rg, lam,
           w_branch_nsa, w_branch_rnn, w_out, w_mlp1, w_mlp2):
    depth = w_ada.shape[0]
    B, T, D = x_prompt.shape
    nb = x_sample.shape[0]
    R = 16
    past_len = page_table.shape[1] * cache_kv.shape[2]

    c_all = jnp.zeros((2 * R, D), F32).at[:B].set(c_prompt).at[R:R + nb].set(c_sample)
    rope_p = _rope_tables(0.0, T)
    rope_s = tuple(jnp.broadcast_to(t, (R, HEAD_DIM)) for t in _rope_tables(float(past_len), 1))

    w_in_t, w_gate = _regroup_w_in(w_in)
    P = dict(g_ln=g_ln, w_in_t=w_in_t, w_gate=w_gate, qk_g=qk_g, w_phi=w_phi, pe_cmp=pe_cmp, conv_w=conv_w, conv_b=conv_b,
             w_rg=w_rg, b_rg=b_rg, lam=lam, w_branch_nsa=w_branch_nsa, w_branch_rnn=w_branch_rnn,
             w_out=w_out, w_mlp1=w_mlp1, w_mlp2=w_mlp2)
    yp = x_prompt
    ys = jnp.pad(x_sample.reshape(nb, D), ((0, R - nb), (0, 0)))
    outs = [[] for _ in range(8)]
    for l in range(depth):
        mod = _mm(c_all, w_ada, layer=l, tm=2 * R, tn=512, epi="silu_bias", bias=b_ada[l][None])
        mod = mod.reshape(2 * R, 6, D)
        mod_p = [mod[:B, k][:, None, :] for k in range(6)]
        mod_s = [mod[R:, k][None] for k in range(6)]
        yp, ys, (a1, a2, a3, a4), (b1, b2, b3, b4) = _layer(
            l, yp, ys, mod_p, mod_s, P, rope_p, rope_s, cache_kv, page_table, state_win, state_rnn, state_conv)
        for lst, v in zip(outs, (a1, b1, a2, b2, a3, b3, a4, b4)):
            lst.append(v)
    kv_p, kv_s, win_p, win_s, h_p, h_s, cv_p, cv_s = (jnp.stack(v) for v in outs)
    return (yp, ys[:nb].reshape(nb, 1, D), kv_p, kv_s, win_p, win_s, h_p, h_s, cv_p, cv_s)
```

```python
import functools
import math

import jax
import jax.numpy as jnp
from jax import lax
from jax.experimental import pallas as pl
from jax.experimental.pallas import tpu as pltpu

BF = jnp.bfloat16
F32 = jnp.float32

HEAD_DIM = 128
N_KV = 4
HPG = 4
N_HEADS = N_KV * HPG
CMP_BLOCK = 64
CMP_SHIFT = 6
N_SELECT = 16
WINDOW = 512
FORCED_SCORE = 1e4
RNN_BW = 128
CONV_W = 4
RG_C = 8.0
ROPE_THETA = 10000.0
EPS = 1e-6
NEG = -1e30
LANES = 128
VMEM_LIMIT = 56 * 1024 * 1024


def _cparams(sem):
    return pltpu.CompilerParams(dimension_semantics=sem, vmem_limit_bytes=VMEM_LIMIT)


def _dot(a, b):
    return jnp.dot(a, b, preferred_element_type=F32)


def _dot_nt(a, b):
    return lax.dot_general(a, b, (((1,), (1,)), ((), ())), preferred_element_type=F32)


def _rep_lanes(x, n):
    return x if n == LANES else jnp.concatenate([x] * (n // LANES), axis=1)


def _mm_kernel(*refs, nk, epi):
    if epi == "resid":
        a_ref, w_ref, x_ref, ga_ref, o_ref = refs[:5]
        scratch = refs[5:]
    elif epi == "silu_bias":
        a_ref, w_ref, bias_ref, o_ref = refs[:4]
        scratch = refs[4:]
    else:
        a_ref, w_ref, o_ref = refs[:3]
        scratch = refs[3:]

    def finish(acc):
        if epi == "relu2":
            r = jnp.maximum(acc, 0.0)
            acc = r * r
        elif epi == "resid":
            acc = x_ref[...] + ga_ref[...] * acc
        elif epi == "silu_bias":
            acc = acc + bias_ref[...]
        o_ref[...] = acc.astype(o_ref.dtype)

    a = a_ref[...]
    if epi == "silu_bias":
        a = (a * jax.nn.sigmoid(a)).astype(BF)
    part = _dot(a, w_ref[...].astype(BF))
    if nk == 1:
        finish(part)
    else:
        acc_ref = scratch[0]
        k = pl.program_id(2)

        @pl.when(k == 0)
        def _():
            acc_ref[...] = part

        @pl.when(k > 0)
        def _():
            acc_ref[...] += part

        @pl.when(k == nk - 1)
        def _():
            finish(acc_ref[...])


def _mm(a, w, *, tm, tn, tk=None, out_dtype=F32, epi=None, x_res=None, ga=None, ga_map=None, bias=None,
        layer=0):
    M, K = a.shape
    N = w.shape[2]
    tk = K if tk is None else tk
    nk = K // tk
    grid = (M // tm, N // tn, nk)
    in_specs = [pl.BlockSpec((tm, tk), lambda i, j, k: (i, k)),
                pl.BlockSpec((None, tk, tn), lambda i, j, k: (layer, k, j))]
    args = [a, w]
    if epi == "resid":
        in_specs.append(pl.BlockSpec((tm, tn), lambda i, j, k: (i, j)))
        in_specs.append(pl.BlockSpec((None, ga.shape[1], tn), ga_map))
        args += [x_res, ga]
    elif epi == "silu_bias":
        in_specs.append(pl.BlockSpec((1, tn), lambda i, j, k: (0, j)))
        args.append(bias)
    scratch =[pltpu.VMEM((tm, tn), F32)] if nk > 1 else []
    return pl.pallas_call(
        functools.partial(_mm_kernel, nk=nk, epi=epi),
        grid=grid, in_specs=in_specs,
        out_specs=pl.BlockSpec((tm, tn), lambda i, j, k: (i, j)),
        out_shape=jax.ShapeDtypeStruct((M, N), out_dtype),
        scratch_shapes=scratch,
        compiler_params=_cparams(("parallel", "parallel", "arbitrary")),
    )(*args)


def _mm_kres_kernel(a_ref, w_ref, x_ref, ga_ref, o_ref, acc_ref, *, nk):
    k, j = pl.program_id(1), pl.program_id(2)
    part = _dot(a_ref[...], w_ref[...])

    @pl.when(k == 0)
    def _():
        acc_ref[j] = part

    @pl.when((k > 0) & (k < nk - 1))
    def _():
        acc_ref[j] += part

    @pl.when(k == nk - 1)
    def _():
        o_ref[...] = x_ref[...] + ga_ref[...] * (acc_ref[j] + part)


def _mm_kres(a, w, x_res, ga, *, layer, rows_per_batch, tm, tn, tk):
    M, K = a.shape
    N = w.shape[2]
    nk, nj = K // tk, N // tn
    assert nk >= 2
    last = lambda k, j: jnp.where(k == nk - 1, j, 0)
    return pl.pallas_call(
        functools.partial(_mm_kres_kernel, nk=nk),
        grid=(M // tm, nk, nj),
        in_specs=[pl.BlockSpec((tm, tk), lambda i, k, j: (i, k)),
                  pl.BlockSpec((None, tk, tn), lambda i, k, j: (layer, k, j)),
                  pl.BlockSpec((tm, tn), lambda i, k, j: (i, last(k, j))),
                  pl.BlockSpec((None, 1, tn), lambda i, k, j: (i * tm // rows_per_batch, 0, last(k, j)))],
        out_specs=pl.BlockSpec((tm, tn), lambda i, k, j: (i, last(k, j))),
        out_shape=jax.ShapeDtypeStruct((M, N), F32),
        scratch_shapes=[pltpu.VMEM((nj, tm, tn), F32)],
        compiler_params=_cparams(("parallel", "arbitrary", "arbitrary")),
    )(a, w, x_res, ga)


def _cached_bf16(w_ref, wbf_ref, first):
    if wbf_ref is None:
        return w_ref[...]

    @pl.when(first)
    def _():
        wbf_ref[...] = w_ref[...].astype(BF)

    return wbf_ref[...]


def _mm2_kernel(*refs, epi, cast_w, w_transposed):
    n_in = 7 if epi == "resid" else 3
    a_ref, a2_ref, w_ref = refs[:3]
    o_ref, o2_ref = refs[n_in:n_in + 2]
    wbf_ref = refs[n_in + 2] if cast_w else None
    first = pl.program_id(1) == 0
    w = _cached_bf16(w_ref, wbf_ref, first)
    mul = _dot_nt if w_transposed else _dot

    def finish(acc, x_ref, ga_ref, out_ref):
        if epi == "relu2":
            r = jnp.maximum(acc, 0.0)
            acc = r * r
        elif epi == "resid":
            acc = x_ref[...] + ga_ref[...] * acc
        out_ref[...] = acc.astype(out_ref.dtype)

    extra = refs[3:7] if epi == "resid" else (None,) * 4
    finish(mul(a_ref[...], w), extra[0], extra[1], o_ref)

    @pl.when(first)
    def _():
        finish(mul(a2_ref[...], w), extra[2], extra[3], o2_ref)


def _mm2(a, a2, w, *, layer, tm, tn, out_dtype=F32, epi=None, res=None, w_rows=None):
    M, K = a.shape
    R2 = a2.shape[0]
    cast_w = w.dtype != BF
    if w_rows is None:
        N = w.shape[2]
        w_spec = pl.BlockSpec((None, K, tn), lambda j, i: (layer, 0, j))
        wbf_shape = (K, tn)
    else:
        row0, N = w_rows
        sublanes = 8 * 4 // w.dtype.itemsize
        assert row0 % sublanes == 0 and tn % sublanes == 0
        w_spec = pl.BlockSpec((None, pl.Element(tn), pl.Element(K)),
                              lambda j, i: (layer, pl.multiple_of(row0 + j * tn, sublanes), 0))
        wbf_shape = (tn, K)
    in_specs = [pl.BlockSpec((tm, K), lambda j, i: (i, 0)),
                pl.BlockSpec((R2, K), lambda j, i: (0, 0)),
                w_spec]
    args = [a, a2, w]
    if epi == "resid":
        x, ga, ga_map, x2, ga2 = res
        in_specs += [pl.BlockSpec((tm, tn), lambda j, i: (i, j)),
                     pl.BlockSpec((None, ga.shape[1], tn), ga_map),
                     pl.BlockSpec((R2, tn), lambda j, i: (0, j)),
                     pl.BlockSpec((None, R2, tn), lambda j, i: (0, 0, j))]
        args += [x, ga, x2, ga2]
    return pl.pallas_call(
        functools.partial(_mm2_kernel, epi=epi, cast_w=cast_w, w_transposed=w_rows is not None),
        grid=(N // tn, M // tm), in_specs=in_specs,
        out_specs=[pl.BlockSpec((tm, tn), lambda j, i: (i, j)),
                   pl.BlockSpec((R2, tn), lambda j, i: (0, j))],
        out_shape=[jax.ShapeDtypeStruct((M, N), out_dtype), jax.ShapeDtypeStruct((R2, N), out_dtype)],
        scratch_shapes=[pltpu.VMEM(wbf_shape, BF)] if cast_w else [],
        compiler_params=_cparams(("arbitrary", "arbitrary")),
    )(*args)


def _merge2_kernel(an_ref, ar_ref, gn_ref, gr_ref, an2_ref, ar2_ref, gn2_ref, gr2_ref, wn_ref, wr_ref,
                   o_ref, o2_ref, wnb_ref, wrb_ref):
    first = pl.program_id(1) == 0
    wn = _cached_bf16(wn_ref, wnb_ref, first)
    wr = _cached_bf16(wr_ref, wrb_ref, first)

    def finish(a_n, a_r, g_n, g_r, out_ref):
        o = jax.nn.sigmoid(g_n[...]) * _dot(a_n[...], wn) + jax.nn.sigmoid(g_r[...]) * _dot(a_r[...], wr)
        out_ref[...] = o.astype(out_ref.dtype)

    finish(an_ref, ar_ref, gn_ref, gr_ref, o_ref)

    @pl.when(first)
    def _():
        finish(an2_ref, ar2_ref, gn2_ref, gr2_ref, o2_ref)


def _merge2(o_nsa, o_rnn, z, o_nsa2, o_rnn2, z2, w_nsa, w_rnn, *, layer, col0, tm, tn):
    M, K = o_nsa.shape
    R2 = o_nsa2.shape[0]
    D = w_nsa.shape[2]
    c0, c1 = col0 // tn, (col0 + D) // tn
    row = lambda j, i: (i, 0)
    one = lambda j, i: (0, 0)
    return pl.pallas_call(
        _merge2_kernel,
        grid=(D // tn, M // tm),
        in_specs=[pl.BlockSpec((tm, K), row), pl.BlockSpec((tm, K), row),
                  pl.BlockSpec((tm, tn), lambda j, i: (i, j + c0)),
                  pl.BlockSpec((tm, tn), lambda j, i: (i, j + c1)),
                  pl.BlockSpec((R2, K), one), pl.BlockSpec((R2, K), one),
                  pl.BlockSpec((R2, tn), lambda j, i: (0, j + c0)),
                  pl.BlockSpec((R2, tn), lambda j, i: (0, j + c1)),
                  pl.BlockSpec((None, K, tn), lambda j, i: (layer, 0, j)),
                  pl.BlockSpec((None, K, tn), lambda j, i: (layer, 0, j))],
        out_specs=[pl.BlockSpec((tm, tn), lambda j, i: (i, j)),
                   pl.BlockSpec((R2, tn), lambda j, i: (0, j))],
        out_shape=[jax.ShapeDtypeStruct((M, D), BF), jax.ShapeDtypeStruct((R2, D), BF)],
        scratch_shapes=[pltpu.VMEM((K, tn), BF), pltpu.VMEM((K, tn), BF)],
        compiler_params=_cparams(("arbitrary", "arbitrary")),
    )(o_nsa, o_rnn, z, z, o_nsa2, o_rnn2, z2, z2, w_nsa, w_rnn)


def _ln_kernel(x_ref, g_ref, sc_ref, sh_ref, o_ref):
    x = x_ref[...]
    ms = jnp.mean(x * x, axis=-1, keepdims=True)
    y = x * lax.rsqrt(ms + EPS) * g_ref[...]
    o_ref[...] = (y * (1.0 + sc_ref[...]) + sh_ref[...]).astype(o_ref.dtype)


def _ln_mod(x, g, sc, sh, *, tt):
    B, T, D = x.shape
    R = sc.shape[1]
    rr = tt if R == T else 1
    mod_map = (lambda b, i: (b, i, 0)) if R == T else (lambda b, i: (b, 0, 0))
    return pl.pallas_call(
        _ln_kernel,
        grid=(B, T // tt),
        in_specs=[pl.BlockSpec((None, tt, D), lambda b, i: (b, i, 0)),
                  pl.BlockSpec((1, D), lambda b, i: (0, 0)),
                  pl.BlockSpec((None, rr, D), mod_map),
                  pl.BlockSpec((None, rr, D), mod_map)],
        out_specs=pl.BlockSpec((None, tt, D), lambda b, i: (b, i, 0)),
        out_shape=jax.ShapeDtypeStruct((B, T, D), BF),
        compiler_params=_cparams(("parallel", "arbitrary")),
    )(x, g, sc, sh)


def _qk_kernel(z_ref, c_ref, s_ref, g_ref, q_ref, kv_ref, win_ref, kvb_ref, *, scale):
    cosf = c_ref[...]
    sinf = s_ref[...]

    def norm_rope(col, gi):
        x = z_ref[:, col:col + HEAD_DIM]
        ms = jnp.mean(x * x, axis=-1, keepdims=True)
        y = x * lax.rsqrt(ms + EPS) * g_ref[gi:gi + 1, :]
        return y * cosf + pltpu.roll(y, HEAD_DIM // 2, 1) * sinf

    nq = N_HEADS * HEAD_DIM
    grp = N_KV * HEAD_DIM
    for h in range(N_HEADS):
        q_ref[:, h * HEAD_DIM:(h + 1) * HEAD_DIM] = (norm_rope(h * HEAD_DIM, 0) * scale).astype(BF)
    for j in range(3):
        for gi in range(N_KV):
            kcol = nq + (2 * j) * grp + gi * HEAD_DIM
            vcol = nq + (2 * j + 1) * grp + gi * HEAD_DIM
            k = norm_rope(kcol, 1 + j)
            v = z_ref[:, vcol:vcol + HEAD_DIM]
            ko = (2 * j) * grp + gi * HEAD_DIM
            vo = (2 * j + 1) * grp + gi * HEAD_DIM
            kvb_ref[:, ko:ko + HEAD_DIM] = k.astype(BF)
            kvb_ref[:, vo:vo + HEAD_DIM] = v.astype(BF)
            tt = k.shape[0]
            if j < 2:
                planes = 4 * N_KV
                kv_ref[pl.ds(2 * j * N_KV + gi, tt, stride=planes), :] = k
                kv_ref[pl.ds((2 * j + 1) * N_KV + gi, tt, stride=planes), :] = v
            else:
                planes = 2 * N_KV
                win_ref[pl.ds(gi, tt, stride=planes), :] = k
                win_ref[pl.ds(N_KV + gi, tt, stride=planes), :] = v


def _qk_prep(z, cosf, sinf, qk_g, *, tt):
    B, T, _ = z.shape
    nq = N_HEADS * HEAD_DIM
    grp = N_KV * HEAD_DIM
    row = lambda b, i: (b, i, 0)
    return pl.pallas_call(
        functools.partial(_qk_kernel, scale=HEAD_DIM ** -0.5),
        grid=(B, T // tt),
        in_specs=[pl.BlockSpec((None, tt, z.shape[2]), row),
                  pl.BlockSpec((tt, HEAD_DIM), lambda b, i: (i, 0)),
                  pl.BlockSpec((tt, HEAD_DIM), lambda b, i: (i, 0)),
                  pl.BlockSpec((4, HEAD_DIM), lambda b, i: (0, 0))],
        out_specs=[pl.BlockSpec((None, tt, nq), row),
                   pl.BlockSpec((None, tt * 4 * N_KV, HEAD_DIM), row),
                   pl.BlockSpec((None, tt * 2 * N_KV, HEAD_DIM), row),
                   pl.BlockSpec((None, tt, 6 * grp), row)],
        out_shape=[jax.ShapeDtypeStruct((B, T, nq), BF),
                   jax.ShapeDtypeStruct((B, T * 4 * N_KV, HEAD_DIM), F32),
                   jax.ShapeDtypeStruct((B, T * 2 * N_KV, HEAD_DIM), F32),
                   jax.ShapeDtypeStruct((B, T, 6 * grp), BF)],
        compiler_params=_cparams(("parallel", "arbitrary")),
    )(z, cosf, sinf, qk_g)


def _rope_tables(pos0, T):
    half = HEAD_DIM // 2
    inv = ROPE_THETA ** (-jnp.arange(half, dtype=F32) / half)
    ang = (pos0 + jnp.arange(T, dtype=F32))[:, None] * inv[None, :]
    cos, sin = jnp.cos(ang), jnp.sin(ang)
    return jnp.concatenate([cos, cos], axis=1), jnp.concatenate([-sin, sin], axis=1)


def _compress_acc(load_pair, pe_ref, w_ref, rows):
    acc = jnp.zeros((rows, 2 * HEAD_DIM), F32)
    for lp in range(CMP_BLOCK // 2):
        xs = []
        for l in (2 * lp, 2 * lp + 1):
            x = load_pair(l) + pe_ref[l][None]
            xs.append(x.reshape(rows, HEAD_DIM).astype(BF))
        acc = acc + _dot(jnp.concatenate(xs, axis=1), w_ref[lp])
    is_k = (lax.broadcasted_iota(jnp.int32, (rows, HEAD_DIM), 0) % 8) < N_KV
    return jnp.where(is_k, acc[:, :HEAD_DIM], acc[:, HEAD_DIM:])


def _compress_kernel(x_ref, pe_ref, w_ref, o_ref, *, nblk):
    rows = nblk * 8
    out = _compress_acc(lambda l: x_ref[pl.ds(l, nblk, stride=CMP_BLOCK), :, :], pe_ref, w_ref, rows)
    o_ref[...] = out.reshape(nblk, 8, HEAD_DIM)


def _compress(kvrows, pe_t, w2):
    B = kvrows.shape[0]
    T = kvrows.shape[1] // (4 * N_KV)
    nblk = T // CMP_BLOCK
    x = kvrows.reshape(B, T, 4 * N_KV, HEAD_DIM)
    return pl.pallas_call(
        functools.partial(_compress_kernel, nblk=nblk),
        grid=(B,),
        in_specs=[pl.BlockSpec((None, T, 8, HEAD_DIM), lambda b: (b, 0, 0, 0)),
                  pl.BlockSpec((CMP_BLOCK, 8, HEAD_DIM), lambda b: (0, 0, 0)),
                  pl.BlockSpec((CMP_BLOCK // 2, 2 * HEAD_DIM, 2 * HEAD_DIM), lambda b: (0, 0, 0))],
        out_specs=pl.BlockSpec((None, nblk, 8, HEAD_DIM), lambda b: (b, 0, 0, 0)),
        out_shape=jax.ShapeDtypeStruct((B, nblk, 8, HEAD_DIM), F32),
        compiler_params=_cparams(("parallel",)),
    )(x, pe_t, w2)


def _compress_paged_kernel(pt_ref, *refs, pb):
    page_refs = refs[:pb]
    pe_ref, w_ref, o_ref = refs[pb:pb + 3]
    rows = pb * 2 * 8

    def load(l):
        tiles = []
        for p in range(pb):
            for h in range(2):
                tiles.append(page_refs[p][h * CMP_BLOCK + l])
        return jnp.stack(tiles, axis=0)

    out = _compress_acc(load, pe_ref, w_ref, rows)
    o_ref[...] = out.reshape(pb * 2, 8, HEAD_DIM)


def _compress_paged(cache5, page_table, layer, pe_t, w2, *, pb):
    n_phys, page = cache5.shape[1], cache5.shape[2]
    B, n_pages = page_table.shape
    assert page == 2 * CMP_BLOCK and n_pages % pb == 0

    def page_spec(p):
        return pl.BlockSpec((None, None, page, 8, HEAD_DIM),
                            lambda b, i, pt: (layer, pt[b, i * pb + p], 0, 0, 0))

    grid_spec = pltpu.PrefetchScalarGridSpec(
        num_scalar_prefetch=1,
        grid=(B, n_pages // pb),
        in_specs=[page_spec(p) for p in range(pb)] + [
            pl.BlockSpec((CMP_BLOCK, 8, HEAD_DIM), lambda b, i, pt: (0, 0, 0)),
            pl.BlockSpec((CMP_BLOCK // 2, 2 * HEAD_DIM, 2 * HEAD_DIM), lambda b, i, pt: (0, 0, 0))],
        out_specs=pl.BlockSpec((None, pb * 2, 8, HEAD_DIM), lambda b, i, pt: (b, i, 0, 0)),
    )
    return pl.pallas_call(
        functools.partial(_compress_paged_kernel, pb=pb),
        grid_spec=grid_spec,
        out_shape=jax.ShapeDtypeStruct((B, n_pages * 2, 8, HEAD_DIM), F32),
        compiler_params=_cparams(("parallel", "arbitrary")),
    )(page_table, *([cache5] * pb), pe_t, w2)


def _online_step(s2, v, m_ref, l_ref, acc_ref):
    tk = s2.shape[1]
    m_prev = m_ref[...]
    m_next = jnp.maximum(m_prev, jnp.max(s2, axis=1, keepdims=True))
    alpha = jnp.exp(m_prev - m_next)
    p = jnp.exp(s2 - _rep_lanes(m_next, tk))
    l_ref[...] = alpha * l_ref[...] + jnp.sum(p, axis=1, keepdims=True)
    m_ref[...] = m_next
    acc_ref[...] = alpha * acc_ref[...] + _dot(p.astype(BF), v)


def _nsa_kernel(*refs, tq, nblk, n_side):
    q_ref, kc_ref, vc_ref, ks_ref, vs_ref, kw_ref, vw_ref, g_ref = refs[:8]
    o_ref = refs[8 + n_side]
    m_ref, l_ref, acc_ref = refs[9 + 2 * n_side:]
    for src, dst in zip(refs[8:8 + n_side], refs[9 + n_side:9 + 2 * n_side]):
        dst[...] = src[...].astype(BF)

    qi = pl.program_id(2)
    q0 = qi * tq
    rows = HPG * tq
    q = q_ref[...]
    qcat = jnp.concatenate([q[:, h * HEAD_DIM:(h + 1) * HEAD_DIM] for h in range(HPG)], axis=0)

    kc = kc_ref[...].astype(BF)
    vc = vc_ref[...].astype(BF)
    t_b = q0 + lax.broadcasted_iota(jnp.int32, (nblk, tq), 1)
    blk = lax.broadcasted_iota(jnp.int32, (nblk, tq), 0)
    complete = jnp.concatenate([(blk + 1) * CMP_BLOCK - 1 <= t_b] * HPG, axis=1)
    s_t = jnp.where(complete, _dot_nt(kc, qcat), NEG)
    mx = jnp.max(s_t, axis=0, keepdims=True)
    e = jnp.where(complete, jnp.exp(s_t - mx), 0.0)
    d = jnp.sum(e, axis=0, keepdims=True)
    p_t = e / jnp.where(d > 0, d, 1.0)

    imp = p_t[:, 0:tq]
    for h in range(1, HPG):
        imp = imp + p_t[:, h * tq:(h + 1) * tq]
    cur = jnp.right_shift(t_b, CMP_SHIFT)
    forced = (blk == 0) | (blk == cur) | (blk == cur - 1)
    score = jnp.where(forced, FORCED_SCORE, imp)
    score = jnp.where(blk <= cur, score, -1.0)
    rank = jnp.zeros((nblk, tq), F32)
    for i in range(nblk):
        si = score[i:i + 1, :]
        beats = (si > score) | ((si == score) & (blk > i))
        rank = rank + jnp.where(beats, 1.0, 0.0)
    sel_t = jnp.where((rank < min(N_SELECT, nblk)) & (score > -0.5), 1.0, 0.0).astype(BF)

    eye = jnp.where(lax.broadcasted_iota(jnp.int32, (tq, tq), 0)
                    == lax.broadcasted_iota(jnp.int32, (tq, tq), 1), 1.0, 0.0).astype(BF)
    p_bf = p_t.astype(BF)
    p_rows = jnp.concatenate([_dot_nt(eye, p_bf[:, h * tq:(h + 1) * tq]) for h in range(HPG)], axis=0)
    o_cmp = _dot(p_rows.astype(BF), vc)
    sel = _dot_nt(eye, sel_t).astype(BF)

    t_k = q0 + lax.broadcasted_iota(jnp.int32, (tq, tq), 0)
    k_off = lax.broadcasted_iota(jnp.int32, (tq, tq), 1)
    e_blk = lax.broadcasted_iota(jnp.int32, (nblk, tq), 0)
    e_off = lax.broadcasted_iota(jnp.int32, (nblk, tq), 1)

    def reset():
        m_ref[...] = jnp.full((rows, LANES), NEG, F32)
        l_ref[...] = jnp.zeros((rows, LANES), F32)
        acc_ref[...] = jnp.zeros((rows, HEAD_DIM), F32)

    def attend(c, k_ref, v_ref, mask):
        start = pl.multiple_of(c * tq, tq)
        k = k_ref[pl.ds(start, tq), :]
        v = v_ref[pl.ds(start, tq), :]
        s = jnp.where(mask[None], _dot_nt(qcat, k).reshape(HPG, tq, tq), NEG)
        _online_step(s.reshape(rows, tq), v, m_ref, l_ref, acc_ref)

    def sel_body(c, carry):
        expand = jnp.where(jnp.right_shift(c * tq + e_off, CMP_SHIFT) == e_blk, 1.0, 0.0).astype(BF)
        chosen = _dot(sel, expand) > 0.5
        attend(c, ks_ref, vs_ref, chosen & (c * tq + k_off <= t_k))
        return carry

    reset()
    lax.fori_loop(0, qi + 1, sel_body, 0)
    o_sel = acc_ref[...] / l_ref[...]

    def win_body(c, carry):
        dist = t_k - (c * tq + k_off)
        attend(c, kw_ref, vw_ref, (dist >= 0) & (dist < WINDOW))
        return carry

    reset()
    lax.fori_loop(jnp.maximum(qi - WINDOW // tq, 0), qi + 1, win_body, 0)
    o_win = acc_ref[...] / l_ref[...]

    g = jax.nn.sigmoid(g_ref[...])
    for h in range(HPG):
        r = slice(h * tq, (h + 1) * tq)
        o = (g[:, 3 * h:3 * h + 1] * o_cmp[r] + g[:, 3 * h + 1:3 * h + 2] * o_sel[r]
             + g[:, 3 * h + 2:3 * h + 3] * o_win[r])
        o_ref[:, h * HEAD_DIM:(h + 1) * HEAD_DIM] = o.astype(BF)


def _nsa_prompt(q, kcv, kvb, zg, *, tq, side_cast):
    B, T, _ = q.shape
    nblk = kcv.shape[1]
    n_q = T // tq
    n_steps = B * N_KV * n_q
    col = lambda c: pl.BlockSpec((None, T, HEAD_DIM), lambda b, g, i, c=c: (b, 0, c + g))
    rows = HPG * tq
    in_specs = [pl.BlockSpec((None, tq, HPG * HEAD_DIM), lambda b, g, i: (b, i, g)),
                pl.BlockSpec((None, nblk, HEAD_DIM), lambda b, g, i: (b, 0, g)),
                pl.BlockSpec((None, nblk, HEAD_DIM), lambda b, g, i: (b, 0, N_KV + g)),
                col(2 * N_KV), col(3 * N_KV), col(4 * N_KV), col(5 * N_KV),
                pl.BlockSpec((None, tq, LANES), lambda b, g, i: (b, i, g))]
    out_specs = [pl.BlockSpec((None, tq, HPG * HEAD_DIM), lambda b, g, i: (b, i, g))]
    out_shape = [jax.ShapeDtypeStruct((B, T, N_HEADS * HEAD_DIM), BF)]
    for p, p_layer in side_cast:
        _, n_rows, n_cols = p.shape
        rb = next(r for r in range(16, n_rows + 16, 16) if pl.cdiv(n_rows, r) <= n_steps)
        last = pl.cdiv(n_rows, rb) - 1
        step = lambda b, g, i, last=last: jnp.minimum((b * N_KV + g) * n_q + i, last)
        in_specs.append(pl.BlockSpec((None, rb, n_cols),
                                     lambda b, g, i, step=step, p_layer=p_layer: (p_layer, step(b, g, i), 0)))
        out_specs.append(pl.BlockSpec((None, rb, n_cols), lambda b, g, i, step=step: (0, step(b, g, i), 0)))
        out_shape.append(jax.ShapeDtypeStruct((1, n_rows, n_cols), BF))
    return pl.pallas_call(
        functools.partial(_nsa_kernel, tq=tq, nblk=nblk, n_side=len(side_cast)),
        grid=(B, N_KV, n_q), in_specs=in_specs, out_specs=out_specs, out_shape=out_shape,
        scratch_shapes=[pltpu.VMEM((rows, LANES), F32), pltpu.VMEM((rows, LANES), F32),
                        pltpu.VMEM((rows, HEAD_DIM), F32)],
        compiler_params=_cparams(("arbitrary", "arbitrary", "arbitrary")),
    )(q, kcv, kcv, kvb, kvb, kvb, kvb, zg, *[p for p, _ in side_cast])


def _softplus(y):
    return jnp.maximum(y, 0.0) + jnp.log1p(jnp.exp(-jnp.abs(y)))


def _rg_gates(u, w_ref, brg_ref, lam_ref):
    ub = u.astype(BF)
    r = jax.nn.sigmoid(_dot(ub, w_ref[0].astype(BF)) + brg_ref[0:1, :])
    i = jax.nn.sigmoid(_dot(ub, w_ref[1].astype(BF)) + brg_ref[1:2, :])
    log_a = -RG_C * r * _softplus(-lam_ref[...])
    a = jnp.exp(log_a)
    th = jnp.tanh(log_a)
    b = jnp.sqrt(-2.0 * th / (1.0 - th)) * i * u
    return a, b


def _rglru_kernel(xr_ref, zg_ref, c0_ref, h0_ref, cw_ref, cb_ref, w_ref, brg_ref, lam_ref,
                  y_ref, hl_ref, cl_ref, xp_ref, a_ref, b_ref, h_ref, *, T):
    pad = 8
    xp_ref[0:pad, :] = jnp.zeros((pad, RNN_BW), F32)
    xp_ref[pad - (CONV_W - 1):pad, :] = c0_ref[...]
    xp_ref[pad:pad + T, :] = xr_ref[...]
    u = cb_ref[...] + xp_ref[pad:pad + T, :] * cw_ref[CONV_W - 1:CONV_W, :]
    for k in range(CONV_W - 1):
        off = pad - (CONV_W - 1) + k
        u = u + xp_ref[off:off + T, :] * cw_ref[k:k + 1, :]
    cl_ref[...] = xp_ref[pad + T - (CONV_W - 1):pad + T, :]

    a, b = _rg_gates(u, w_ref, brg_ref, lam_ref)
    a_ref[0:T, :] = a
    b_ref[0:T, :] = b

    levels = []
    n, off = T, 0
    while True:
        levels.append((n, off))
        if n <= 8:
            break
        off += n
        n //= 8
    for (n, off), (n2, off2) in zip(levels[:-1], levels[1:]):
        A = a_ref[pl.ds(off, n2, stride=8), :]
        Bv = b_ref[pl.ds(off, n2, stride=8), :]
        for r in range(1, 8):
            ar = a_ref[pl.ds(off + r, n2, stride=8), :]
            br = b_ref[pl.ds(off + r, n2, stride=8), :]
            Bv = ar * Bv + br
            A = ar * A
        a_ref[off2:off2 + n2, :] = A
        b_ref[off2:off2 + n2, :] = Bv
    n, off = levels[-1]
    h = h0_ref[...]
    for t in range(n):
        h = a_ref[off + t:off + t + 1, :] * h + b_ref[off + t:off + t + 1, :]
        h_ref[off + 8 + t:off + 9 + t, :] = h
    for (n, off), (n2, off2) in reversed(list(zip(levels[:-1], levels[1:]))):
        h_ref[off2 + 7:off2 + 8, :] = h0_ref[...]
        hp = h_ref[off2 + 7:off2 + 7 + n2, :]
        for r in range(8):
            ar = a_ref[pl.ds(off + r, n2, stride=8), :]
            br = b_ref[pl.ds(off + r, n2, stride=8), :]
            hp = ar * hp + br
            h_ref[pl.ds(off + 8 + r, n2, stride=8), :] = hp
    hfull = h_ref[8:8 + T, :]
    y_ref[...] = (jax.nn.gelu(zg_ref[...]) * hfull).astype(BF)
    hl_ref[...] = h_ref[8 + T - 1:8 + T, :]


def _rglru(zr, conv0, h0, conv_w, conv_b, w_rg, b_rg, lam, *, d_rnn):
    B, T, _ = zr.shape
    nb = d_rnn // RNN_BW
    assert T % 8 == 0
    total = T + T // 4 + 64
    return pl.pallas_call(
        functools.partial(_rglru_kernel, T=T),
        grid=(B, nb),
        in_specs=[pl.BlockSpec((None, T, RNN_BW), lambda b, n: (b, 0, n)),
                  pl.BlockSpec((None, T, RNN_BW), lambda b, n: (b, 0, nb + n)),
                  pl.BlockSpec((None, CONV_W - 1, RNN_BW), lambda b, n: (b, 0, n)),
                  pl.BlockSpec((None, 1, RNN_BW), lambda b, n: (b, 0, n)),
                  pl.BlockSpec((CONV_W, RNN_BW), lambda b, n: (0, n)),
                  pl.BlockSpec((1, RNN_BW), lambda b, n: (0, n)),
                  pl.BlockSpec((2, None, RNN_BW, RNN_BW), lambda b, n: (0, n, 0, 0)),
                  pl.BlockSpec((2, RNN_BW), lambda b, n: (0, n)),
                  pl.BlockSpec((1, RNN_BW), lambda b, n: (0, n))],
        out_specs=[pl.BlockSpec((None, T, RNN_BW), lambda b, n: (b, 0, n)),
                   pl.BlockSpec((None, 1, RNN_BW), lambda b, n: (b, 0, n)),
                   pl.BlockSpec((None, CONV_W - 1, RNN_BW), lambda b, n: (b, 0, n))],
        out_shape=[jax.ShapeDtypeStruct((B, T, d_rnn), BF),
                   jax.ShapeDtypeStruct((B, 1, d_rnn), F32),
                   jax.ShapeDtypeStruct((B, CONV_W - 1, d_rnn), F32)],
        scratch_shapes=[pltpu.VMEM((T + 8, RNN_BW), F32), pltpu.VMEM((total, RNN_BW), F32),
                        pltpu.VMEM((total, RNN_BW), F32), pltpu.VMEM((total + 8, RNN_BW), F32)],
        compiler_params=_cparams(("parallel", "arbitrary")),
    )(zr, zr, conv0, h0, conv_w, conv_b[None], w_rg, b_rg, lam[None])


def _rglru_step_kernel(xr_ref, zg_ref, c0_ref, h0_ref, cw_ref, cb_ref, w_ref, brg_ref, lam_ref,
                       y_ref, hl_ref):
    u = cb_ref[...] + xr_ref[...] * cw_ref[CONV_W - 1:CONV_W, :]
    for k in range(CONV_W - 1):
        u = u + c0_ref[k] * cw_ref[k:k + 1, :]
    a, b = _rg_gates(u, w_ref, brg_ref, lam_ref)
    h = a * h0_ref[...] + b
    hl_ref[...] = h
    y_ref[...] = (jax.nn.gelu(zg_ref[...]) * h).astype(BF)


def _rglru_step(zr, conv0_t, h0, conv_w, conv_b, w_rg, b_rg, lam, *, d_rnn):
    R = zr.shape[0]
    nb = d_rnn // RNN_BW
    return pl.pallas_call(
        _rglru_step_kernel,
        grid=(nb,),
        in_specs=[pl.BlockSpec((R, RNN_BW), lambda n: (0, n)),
                  pl.BlockSpec((R, RNN_BW), lambda n: (0, nb + n)),
                  pl.BlockSpec((CONV_W - 1, R, RNN_BW), lambda n: (0, 0, n)),
                  pl.BlockSpec((R, RNN_BW), lambda n: (0, n)),
                  pl.BlockSpec((CONV_W, RNN_BW), lambda n: (0, n)),
                  pl.BlockSpec((1, RNN_BW), lambda n: (0, n)),
                  pl.BlockSpec((2, None, RNN_BW, RNN_BW), lambda n: (0, n, 0, 0)),
                  pl.BlockSpec((2, RNN_BW), lambda n: (0, n)),
                  pl.BlockSpec((1, RNN_BW), lambda n: (0, n))],
        out_specs=[pl.BlockSpec((R, RNN_BW), lambda n: (0, n)),
                   pl.BlockSpec((R, RNN_BW), lambda n: (0, n))],
        out_shape=[jax.ShapeDtypeStruct((R, d_rnn), BF), jax.ShapeDtypeStruct((R, d_rnn), F32)],
        compiler_params=_cparams(("arbitrary",)),
    )(zr, zr, conv0_t, h0, conv_w, conv_b[None], w_rg, b_rg, lam[None])


def _dec_cmp_kernel(q_ref, kcv_ref, o_ref, idx_ref, *, n_past, t_pos):
    nlan = LANES * ((n_past + 1 + LANES - 1) // LANES)
    blk = lax.broadcasted_iota(jnp.int32, (1, n_past), 1)
    complete = (blk + 1) * CMP_BLOCK - 1 <= t_pos
    imps = []
    for g in range(N_KV):
        kc = kcv_ref[:, g * HEAD_DIM:(g + 1) * HEAD_DIM].astype(BF)
        vc = kcv_ref[:, (N_KV + g) * HEAD_DIM:(N_KV + g + 1) * HEAD_DIM].astype(BF)
        s = jnp.where(complete, _dot_nt(q_ref[g], kc), NEG)
        mx = jnp.max(s, axis=-1, keepdims=True)
        e = jnp.where(complete, jnp.exp(s - mx), 0.0)
        d = jnp.sum(e, axis=-1, keepdims=True)
        p = e / jnp.where(d > 0, d, 1.0)
        o_ref[g] = _dot(p.astype(BF), vc)
        head = lax.broadcasted_iota(jnp.int32, p.shape, 0) < HPG
        imps.append(jnp.sum(jnp.where(head, p, 0.0), axis=0, keepdims=True))
    imp = jnp.concatenate(imps, axis=0)

    cur = t_pos // CMP_BLOCK
    imp_all = jnp.concatenate([imp, jnp.zeros((N_KV, nlan - n_past), F32)], axis=1)
    blk_all = lax.broadcasted_iota(jnp.int32, (N_KV, nlan), 1)
    forced = (blk_all == 0) | (blk_all == cur) | (blk_all == cur - 1)
    score = jnp.where(forced, FORCED_SCORE, imp_all)
    score = jnp.where(blk_all <= cur, score, -1.0)
    score = jnp.where(blk_all <= n_past, score, -2.0)
    blk_f = blk_all.astype(F32)
    slot = lax.broadcasted_iota(jnp.int32, (N_KV, LANES), 1)
    res = jnp.full((N_KV, LANES), float(n_past), F32)
    for k in range(min(N_SELECT, n_past + 1)):
        top = jnp.max(score, axis=-1, keepdims=True)
        arg = jnp.min(jnp.where(score == top, blk_f, float(nlan)), axis=-1, keepdims=True)
        res = jnp.where(slot == k, jnp.where(top > -0.5, arg, float(n_past)), res)
        score = jnp.where(blk_f == arg, -3.0, score)
    idx_ref[...] = res.astype(jnp.int32)


def _dec_cmp(q16, kcv, *, t_pos):
    B = q16.shape[0]
    n_past = kcv.shape[1]
    return pl.pallas_call(
        functools.partial(_dec_cmp_kernel, n_past=n_past, t_pos=t_pos),
        grid=(B,),
        in_specs=[pl.BlockSpec((None, N_KV, 16, HEAD_DIM), lambda b: (b, 0, 0, 0)),
                  pl.BlockSpec((None, n_past, 2 * N_KV * HEAD_DIM), lambda b: (b, 0, 0))],
        out_specs=[pl.BlockSpec((None, N_KV, 16, HEAD_DIM), lambda b: (b, 0, 0, 0)),
                   pl.BlockSpec((None, N_KV, LANES), lambda b: (b, 0, 0))],
        out_shape=[jax.ShapeDtypeStruct((B, N_KV, 16, HEAD_DIM), F32),
                   jax.ShapeDtypeStruct((B, N_KV, LANES), jnp.int32)],
        compiler_params=_cparams(("parallel",)),
    )(q16, kcv)


def _dec_attn_kernel(pt_ref, ti_ref, q_ref, *refs, n_past, per_step):
    cache_refs = refs[:N_KV * per_step]
    new_ref, win_ref, oc_ref, g_ref, o_ref, m_ref, l_ref, acc_ref = refs[N_KV * per_step:]
    b, s = pl.program_id(0), pl.program_id(1)
    planes = 4 * N_KV

    @pl.when(s == 0)
    def _():
        m_ref[...] = jnp.full(m_ref.shape, NEG, F32)
        l_ref[...] = jnp.zeros(l_ref.shape, F32)
        acc_ref[...] = jnp.zeros(acc_ref.shape, F32)

    n_keys = per_step * CMP_BLOCK
    key_slot = lax.broadcasted_iota(jnp.int32, (1, n_keys), 1) // CMP_BLOCK
    for g in range(N_KV):
        ks, vs = [], []
        key_blk = jnp.full((1, n_keys), n_past, jnp.int32)
        for u in range(per_step):
            ref = cache_refs[u * N_KV + g]
            ks.append(ref[pl.ds(2 * N_KV + g, CMP_BLOCK, stride=planes), :].astype(BF))
            vs.append(ref[pl.ds(3 * N_KV + g, CMP_BLOCK, stride=planes), :].astype(BF))
            key_blk = jnp.where(key_slot == u, ti_ref[b * N_KV + g, s * per_step + u], key_blk)
        live = key_blk < n_past
        sc =jnp.where(live, _dot_nt(q_ref[g], jnp.concatenate(ks, axis=0)), NEG)
        m_prev = m_ref[g]
        m_next = jnp.maximum(m_prev, jnp.max(sc, axis=1, keepdims=True))
        alpha = jnp.exp(m_prev - m_next)
        p = jnp.where(live, jnp.exp(sc - m_next[:, :n_keys]), 0.0)
        l_ref[g] = alpha * l_ref[g] + jnp.sum(p, axis=1, keepdims=True)
        m_ref[g] = m_next
        acc_ref[g] = alpha * acc_ref[g] + _dot(p.astype(BF), jnp.concatenate(vs, axis=0))

    @pl.when(s == pl.num_programs(1) - 1)
    def _():
        n_win = win_ref.shape[0] // (2 * N_KV)
        keep = lax.broadcasted_iota(jnp.int32, (1, n_win), 1) >= n_win + 1 - WINDOW
        for g in range(N_KV):
            q = q_ref[g]
            qf = q.astype(F32)

            def self_score(k_row):
                return jnp.sum(qf * k_row.astype(BF).astype(F32), axis=1, keepdims=True)

            s_self = self_score(new_ref[g, 2:3, :])
            v_self = new_ref[g, 3:4, :].astype(BF).astype(F32)
            m_prev = m_ref[g]
            m_fin = jnp.maximum(m_prev, s_self)
            alpha = jnp.exp(m_prev - m_fin)
            p_self = jnp.exp(s_self - m_fin)
            o_sel = (alpha * acc_ref[g] + p_self * v_self) / (alpha * l_ref[g] + p_self)

            kw = win_ref[pl.ds(g, n_win, stride=2 * N_KV), :].astype(BF)
            vw = win_ref[pl.ds(N_KV + g, n_win, stride=2 * N_KV), :].astype(BF)
            sw = jnp.where(keep, _dot_nt(q, kw), NEG)
            sw_self = self_score(new_ref[g, 4:5, :])
            mw = jnp.maximum(jnp.max(sw, axis=1, keepdims=True), sw_self)
            pw = jnp.where(keep, jnp.exp(sw - mw), 0.0)
            pw_self = jnp.exp(sw_self - mw)
            num = _dot(pw.astype(BF), vw) + pw_self * new_ref[g, 5:6, :].astype(BF).astype(F32)
            o_win = num / (jnp.sum(pw, axis=1, keepdims=True) + pw_self)

            gt = jax.nn.sigmoid(g_ref[g])
            o_ref[g] = gt[:, 0:1] * oc_ref[g] + gt[:, 1:2] * o_sel + gt[:, 2:3] * o_win


def _dec_attn(q16, cache4, page_table, top_i, newkv, win3, o_cmp, g16, *, layer, n_past):
    B = q16.shape[0]
    n_slots = top_i.shape[1]
    per_step = 2 if n_slots % 2 == 0 else 1
    half_rows = CMP_BLOCK * 4 * N_KV

    def half_page(u, g):
        def index(b, s, pt, ti):
            blk = jnp.minimum(ti[b * N_KV + g, s * per_step + u], n_past - 1)
            return (layer, pt[b, blk // 2], blk % 2, 0)
        return pl.BlockSpec((None, None, half_rows, HEAD_DIM), index)

    per_seq = lambda b, s, pt, ti: (b, 0, 0, 0)
    grid_spec = pltpu.PrefetchScalarGridSpec(
        num_scalar_prefetch=2,
        grid=(B, n_slots // per_step),
        in_specs=[pl.BlockSpec((None, N_KV, 16, HEAD_DIM), per_seq)]
        + [half_page(u, g) for u in range(per_step) for g in range(N_KV)]
        + [pl.BlockSpec((None, N_KV, 6, HEAD_DIM), per_seq),
           pl.BlockSpec((None, win3.shape[1], HEAD_DIM), lambda b, s, pt, ti: (layer * B + b, 0, 0)),
           pl.BlockSpec((None, N_KV, 16, HEAD_DIM), per_seq),
           pl.BlockSpec((None, N_KV, 16, LANES), per_seq)],
        out_specs=pl.BlockSpec((None, N_KV, 16, HEAD_DIM), per_seq),
        scratch_shapes=[pltpu.VMEM((N_KV, 16, LANES), F32), pltpu.VMEM((N_KV, 16, LANES), F32),
                        pltpu.VMEM((N_KV, 16, HEAD_DIM), F32)],
    )
    return pl.pallas_call(
        functools.partial(_dec_attn_kernel, n_past=n_past, per_step=per_step),
        grid_spec=grid_spec,
        out_shape=jax.ShapeDtypeStruct((B, N_KV, 16, HEAD_DIM), F32),
        compiler_params=_cparams(("parallel", "arbitrary")),
    )(page_table, top_i, q16, *([cache4] * (N_KV * per_step)), newkv, win3, o_cmp, g16)


def _regroup_w_in(w_in):
    L, D, _ = w_in.shape
    o_gn = N_HEADS * HEAD_DIM + 6 * N_KV * HEAD_DIM
    o_rx = o_gn + 3 * N_HEADS
    wg = w_in[:, :, o_gn:o_rx].reshape(L, D, N_KV, 3 * HPG)
    wg = jnp.pad(wg, ((0, 0), (0, 0), (0, 0), (0, LANES - 3 * HPG))).reshape(L, D, N_KV * LANES)
    return jnp.swapaxes(w_in, 1, 2), wg.astype(BF)


def _row_tile(m):
    return 1024 if m % 1024 == 0 else 256


def _largest_divisor(n, cap):
    return max(d for d in range(1, cap + 1) if n % d == 0)


def _layer(l, xp, xs, mod_p, mod_s, P, w_in_bf, rope_p, rope_s, cache_kv, page_table, state_win, state_rnn,
           state_conv):
    B, T, D = xp.shape
    M = B * T
    R = xs.shape[0]
    nb, n_pages = page_table.shape
    depth, n_phys, page = cache_kv.shape[:3]
    n_past = n_pages * page // CMP_BLOCK
    d_attn = N_HEADS * HEAD_DIM
    d_rnn = P["conv_w"].shape[2]
    o_gn = d_attn + 6 * N_KV * HEAD_DIM
    o_rx = o_gn + 3 * N_HEADS
    tm, tn = _row_tile(M), 512
    shp1, scp1, gap1, shp2, scp2, gap2 = mod_p
    shs1, scs1, gas1, shs2, scs2, gas2 = mod_s
    bmap = lambda j, i: (i // (T // tm), 0, j)
    g_ln = P["g_ln"][l]

    w2 = jnp.concatenate([P["w_phi"][l, 0], P["w_phi"][l, 1]], axis=-1)
    w2 = w2.reshape(CMP_BLOCK // 2, 2 * HEAD_DIM, 2 * HEAD_DIM).astype(BF)
    pe_t = jnp.concatenate([jnp.broadcast_to(P["pe_cmp"][l, j][:, None, :], (CMP_BLOCK, N_KV, HEAD_DIM))
                            for j in range(2)], axis=1)
    rnn = (P["conv_w"][l], P["conv_b"][l], P["w_rg"][l], P["b_rg"][l], P["lam"][l])

    hp = _ln_mod(xp, g_ln[0:1], scp1, shp1, tt=256).reshape(M, D)
    hs = _ln_mod(xs[None], g_ln[0:1], scs1, shs1, tt=R).reshape(R, D)
    zq_p, zq_s = _mm2(hp, hs, w_in_bf, layer=0, tm=tm, tn=1024, w_rows=(0, o_gn))
    zg_p, zg_s = _mm2(hp, hs, P["w_gate"], layer=l, tm=tm, tn=tn)
    zr_p, zr_s = _mm2(hp, hs, w_in_bf, layer=0, tm=tm, tn=1024, w_rows=(o_rx, w_in_bf.shape[1] - o_rx))

    q, kvrows, winrows, kvb = _qk_prep(zq_p.reshape(B, T, -1), rope_p[0], rope_p[1], P["qk_g"][l], tt=256)
    kcv = _compress(kvrows, pe_t, w2)
    side = [(P["w_out"], l), (P["w_mlp1"], l), (P["w_mlp2"], l)]
    if l + 1 < P["w_in_t"].shape[0]:
        side.append((P["w_in_t"], l + 1))
    o_nsa_p, w_out_bf, w_mlp1_bf, w_mlp2_bf, *w_in_next = _nsa_prompt(
        q, kcv.reshape(B, T // CMP_BLOCK, 8 * HEAD_DIM), kvb, zg_p.reshape(B, T, -1), tq=256, side_cast=side)
    conv0 = jnp.zeros((B, CONV_W - 1, d_rnn), F32)
    h0 = jnp.zeros((B, 1, d_rnn), F32)
    y_rnn_p, h_last, conv_last = _rglru(zr_p.reshape(B, T, -1), conv0, h0, *rnn, d_rnn=d_rnn)

    qs, kvrows_s, winrows_s, _ = _qk_prep(zq_s[None], rope_s[0], rope_s[1], P["qk_g"][l], tt=R)
    q16 = jnp.pad(qs[0, :nb].reshape(nb, N_KV, HPG, HEAD_DIM), ((0, 0), (0, 0), (0, 16 - HPG), (0, 0)))
    g16 = jnp.pad(zg_s[:nb].reshape(nb, N_KV, LANES)[:, :, :3 * HPG].reshape(nb, N_KV, HPG, 3),
                  ((0, 0), (0, 0), (0, 16 - HPG), (0, LANES - 3)))
    kv_new = kvrows_s.reshape(R, 4, N_KV, HEAD_DIM)[:nb]
    win_new = winrows_s.reshape(R, 2, N_KV, HEAD_DIM)[:nb]
    new6 = jnp.concatenate([kv_new, win_new], axis=1).transpose(0, 2, 1, 3)
    cache5 = cache_kv.reshape(depth, n_phys, page, 4 * N_KV, HEAD_DIM)
    kcv_s = _compress_paged(cache5, page_table, l, pe_t, w2, pb=_largest_divisor(n_pages, 32))
    o_cmp, top = _dec_cmp(q16, kcv_s.reshape(nb, n_past, 8 * HEAD_DIM), t_pos=n_pages * page)
    top_i = top[:, :, :N_SELECT].reshape(nb * N_KV, N_SELECT)
    cache4 = cache_kv.reshape(depth, n_phys, page * 4 * N_KV, HEAD_DIM)
    n_win = state_win.shape[2]
    win3 = state_win.reshape(depth * nb, n_win * 2 * N_KV, HEAD_DIM)
    o16 = _dec_attn(q16, cache4, page_table, top_i, new6, win3, o_cmp, g16, layer=l, n_past=n_past)
    o_nsa_s = jnp.pad(o16[:, :, :HPG].reshape(nb, -1), ((0, R - nb), (0, 0))).astype(BF)
    pad_r = lambda a: jnp.pad(a, ((0, R - nb),) + ((0, 0),) * (a.ndim - 1))
    conv0_t = pad_r(state_conv[l]).transpose(1, 0, 2)
    y_rnn_s, h_new = _rglru_step(zr_s, conv0_t, pad_r(state_rnn[l]), *rnn, d_rnn=d_rnn)

    mg_p, mg_s = _merge2(o_nsa_p.reshape(M, -1), y_rnn_p.reshape(M, -1), zr_p, o_nsa_s, y_rnn_s, zr_s,
                         P["w_branch_nsa"], P["w_branch_rnn"], layer=l, col0=2 * d_rnn, tm=tm, tn=tn)
    x1p, x1s = _mm2(mg_p, mg_s, w_out_bf, layer=0, tm=tm, tn=tn, epi="resid",
                    res=(xp.reshape(M, D), gap1, bmap, xs, gas1))
    h2p = _ln_mod(x1p.reshape(B, T, D), g_ln[1:2], scp2, shp2, tt=256).reshape(M, D)
    h2s = _ln_mod(x1s[None], g_ln[1:2], scs2, shs2, tt=R).reshape(R, D)
    up, us = _mm2(h2p, h2s, w_mlp1_bf, layer=0, tm=tm, tn=1024, out_dtype=BF, epi="relu2")
    x2p = _mm_kres(up, w_mlp2_bf, x1p, gap2, layer=0, rows_per_batch=T, tm=tm, tn=tn, tk=4096)
    x2s = _mm(us, w_mlp2_bf, layer=0, tm=R, tn=1024, tk=4096, epi="resid", x_res=x1s, ga=gas2,
              ga_map=lambda i, j, k: (0, 0, j))

    n_keep = min(WINDOW, T)
    outs_p = (kvrows.reshape(B, T, 4, N_KV, HEAD_DIM),
              winrows.reshape(B, T, 2, N_KV, HEAD_DIM)[:, T - n_keep:],
              h_last.reshape(B, -1), conv_last)
    outs_s = (kv_new[:, None],
              jnp.concatenate([state_win[l][:, 1:], win_new[:, None]], axis=1),
              h_new[:nb],
              jnp.concatenate([state_conv[l][:, 1:], zr_s[:nb, None, :d_rnn]], axis=1))
    return x2p.reshape(B, T, D), x2s, outs_p, outs_s, (w_in_next[0] if w_in_next else None)


def kernel(x_prompt, x_sample, cache_kv, state_win, state_rnn, state_conv, page_table, c_prompt, c_sample,
           w_ada, b_ada, g_ln, w_in, qk_g, w_phi, pe_cmp, conv_w, conv_b, w_rg, b_rg, lam,
           w_branch_nsa, w_branch_rnn, w_out, w_mlp1, w_mlp2):
    depth = w_ada.shape[0]
    B, T, D = x_prompt.shape
    nb = x_sample.shape[0]
    R = 16
    past_len = page_table.shape[1] * cache_kv.shape[2]

    c_all = jnp.zeros((2 * R, D), F32).at[:B].set(c_prompt).at[R:R + nb].set(c_sample)
    rope_p = _rope_tables(0.0, T)
    rope_s = tuple(jnp.broadcast_to(t, (R, HEAD_DIM)) for t in _rope_tables(float(past_len), 1))

    w_in_t, w_gate = _regroup_w_in(w_in)
    P = dict(g_ln=g_ln, w_in_t=w_in_t, w_gate=w_gate, qk_g=qk_g, w_phi=w_phi, pe_cmp=pe_cmp, conv_w=conv_w, conv_b=conv_b,
             w_rg=w_rg, b_rg=b_rg, lam=lam, w_branch_nsa=w_branch_nsa, w_branch_rnn=w_branch_rnn,
             w_out=w_out, w_mlp1=w_mlp1, w_mlp2=w_mlp2)
    yp = x_prompt
    ys = jnp.pad(x_sample.reshape(nb, D), ((0, R - nb), (0, 0)))
    outs = [[] for _ in range(8)]
    w_in_bf = w_in_t[0:1].astype(BF)
    for l in range(depth):
        mod = _mm(c_all, w_ada, layer=l, tm=2 * R, tn=512, epi="silu_bias", bias=b_ada[l][None])
        mod = mod.reshape(2 * R, 6, D)
        mod_p = [mod[:B, k][:, None, :] for k in range(6)]
        mod_s = [mod[R:, k][None] for k in range(6)]
        yp, ys, (a1, a2, a3, a4), (b1, b2, b3, b4), w_in_bf = _layer(
            l, yp, ys, mod_p, mod_s, P, w_in_bf, rope_p, rope_s, cache_kv, page_table, state_win, state_rnn,
            state_conv)
        for lst, v in zip(outs, (a1, b1, a2, b2, a3, b3, a4, b4)):
            lst.append(v)
    kv_p, kv_s, win_p, win_s, h_p, h_s, cv_p, cv_s = (jnp.stack(v) for v in outs)
    return (yp, ys[:nb].reshape(nb, 1, D), kv_p, kv_s, win_p, win_s, h_p, h_s, cv_p, cv_s)
```

```python
import functools

import jax
import jax.numpy as jnp
from jax import lax
from jax.experimental import pallas as pl
from jax.experimental.pallas import tpu as pltpu

BF = jnp.bfloat16
F32 = jnp.float32

HEAD_DIM = 128
N_KV = 4
HPG = 4
N_HEADS = N_KV * HPG
CMP_BLOCK = 64
CMP_SHIFT = 6
N_SELECT = 16
WINDOW = 512
FORCED_SCORE = 1e4
RNN_BW = 128
CONV_W = 4
RG_C = 8.0
ROPE_THETA = 10000.0
EPS = 1e-6
NEG = -1e30
LANES = 128
VMEM_LIMIT = 56 * 1024 * 1024


def _cparams(sem):
    return pltpu.CompilerParams(dimension_semantics=sem, vmem_limit_bytes=VMEM_LIMIT)


def _dot(a, b):
    return jnp.dot(a, b, preferred_element_type=F32)


def _dot_nt(a, b):
    return lax.dot_general(a, b, (((1,), (1,)), ((), ())), preferred_element_type=F32)


def _rep_lanes(x, n):
    return x if n == LANES else jnp.concatenate([x] * (n // LANES), axis=1)


def _mm_kernel(*refs, nk, epi):
    if epi == "resid":
        a_ref, w_ref, x_ref, ga_ref, o_ref = refs[:5]
        scratch = refs[5:]
    else:
        a_ref, w_ref, bias_ref, o_ref = refs[:4]
        scratch = refs[4:]

    def finish(acc):
        if epi == "resid":
            acc = x_ref[...] + ga_ref[...] * acc
        else:
            acc = acc + bias_ref[...]
        o_ref[...] = acc.astype(o_ref.dtype)

    a = a_ref[...]
    if epi == "silu_bias":
        a = (a * jax.nn.sigmoid(a)).astype(BF)
    part = _dot(a, w_ref[...].astype(BF))
    if nk == 1:
        finish(part)
    else:
        acc_ref = scratch[0]
        k = pl.program_id(2)

        @pl.when(k == 0)
        def _():
            acc_ref[...] = part

        @pl.when(k > 0)
        def _():
            acc_ref[...] += part

        @pl.when(k == nk - 1)
        def _():
            finish(acc_ref[...])


def _mm(a, w, *, tm, tn, tk=None, out_dtype=F32, epi=None, x_res=None, ga=None, ga_map=None, bias=None,
        layer=0):
    assert epi in ("resid", "silu_bias")
    M, K = a.shape
    N = w.shape[2]
    tk = K if tk is None else tk
    nk = K // tk
    grid = (M // tm, N // tn, nk)
    in_specs = [pl.BlockSpec((tm, tk), lambda i, j, k: (i, k)),
                pl.BlockSpec((None, tk, tn), lambda i, j, k: (layer, k, j))]
    args = [a, w]
    if epi == "resid":
        in_specs.append(pl.BlockSpec((tm, tn), lambda i, j, k: (i, j)))
        in_specs.append(pl.BlockSpec((None, ga.shape[1], tn), ga_map))
        args += [x_res, ga]
    elif epi == "silu_bias":
        in_specs.append(pl.BlockSpec((1, tn), lambda i, j, k: (0, j)))
        args.append(bias)
    scratch =[pltpu.VMEM((tm, tn), F32)] if nk > 1 else []
    return pl.pallas_call(
        functools.partial(_mm_kernel, nk=nk, epi=epi),
        grid=grid, in_specs=in_specs,
        out_specs=pl.BlockSpec((tm, tn), lambda i, j, k: (i, j)),
        out_shape=jax.ShapeDtypeStruct((M, N), out_dtype),
        scratch_shapes=scratch,
        compiler_params=_cparams(("parallel", "parallel", "arbitrary")),
    )(*args)


def _mm_kres_kernel(a_ref, w_ref, x_ref, ga_ref, o_ref, acc_ref, *, nk):
    k, j = pl.program_id(1), pl.program_id(2)
    part = _dot(a_ref[...], w_ref[...])

    @pl.when(k == 0)
    def _():
        acc_ref[j] = part

    @pl.when((k > 0) & (k < nk - 1))
    def _():
        acc_ref[j] += part

    @pl.when(k == nk - 1)
    def _():
        o_ref[...] = x_ref[...] + ga_ref[...] * (acc_ref[j] + part)


def _mm_kres(a, w, x_res, ga, *, layer, rows_per_batch, tm, tn, tk):
    M, K = a.shape
    N = w.shape[2]
    nk, nj = K // tk, N // tn
    assert nk >= 2
    last = lambda k, j: jnp.where(k == nk - 1, j, 0)
    return pl.pallas_call(
        functools.partial(_mm_kres_kernel, nk=nk),
        grid=(M // tm, nk, nj),
        in_specs=[pl.BlockSpec((tm, tk), lambda i, k, j: (i, k)),
                  pl.BlockSpec((None, tk, tn), lambda i, k, j: (layer, k, j)),
                  pl.BlockSpec((tm, tn), lambda i, k, j: (i, last(k, j))),
                  pl.BlockSpec((None, 1, tn), lambda i, k, j: (i * tm // rows_per_batch, 0, last(k, j)))],
        out_specs=pl.BlockSpec((tm, tn), lambda i, k, j: (i, last(k, j))),
        out_shape=jax.ShapeDtypeStruct((M, N), F32),
        scratch_shapes=[pltpu.VMEM((nj, tm, tn), F32)],
        compiler_params=_cparams(("parallel", "arbitrary", "arbitrary")),
    )(a, w, x_res, ga)


def _cached_bf16(w_ref, wbf_ref, first):
    if wbf_ref is None:
        return w_ref[...]

    @pl.when(first)
    def _():
        wbf_ref[...] = w_ref[...].astype(BF)

    return wbf_ref[...]


def _mm2_kernel(*refs, epi, cast_w, w_transposed):
    n_in = 7 if epi == "resid" else 3
    a_ref, a2_ref, w_ref = refs[:3]
    o_ref, o2_ref = refs[n_in:n_in + 2]
    wbf_ref = refs[n_in + 2] if cast_w else None
    first = pl.program_id(1) == 0
    w = _cached_bf16(w_ref, wbf_ref, first)
    mul = _dot_nt if w_transposed else _dot

    def finish(acc, x_ref, ga_ref, out_ref):
        if epi == "relu2":
            r = jnp.maximum(acc, 0.0)
            acc = r * r
        elif epi == "resid":
            acc = x_ref[...] + ga_ref[...] * acc
        out_ref[...] = acc.astype(out_ref.dtype)

    extra = refs[3:7] if epi == "resid" else (None,) * 4
    finish(mul(a_ref[...], w), extra[0], extra[1], o_ref)

    @pl.when(first)
    def _():
        finish(mul(a2_ref[...], w), extra[2], extra[3], o2_ref)


def _mm2(a, a2, w, *, layer, tm, tn, out_dtype=F32, epi=None, res=None, w_rows=None):
    M, K = a.shape
    R2 = a2.shape[0]
    cast_w = w.dtype != BF
    if w_rows is None:
        N = w.shape[2]
        w_spec = pl.BlockSpec((None, K, tn), lambda j, i: (layer, 0, j))
        wbf_shape = (K, tn)
    else:
        row0, N = w_rows
        sublanes = 8 * 4 // w.dtype.itemsize
        assert row0 % sublanes == 0 and tn % sublanes == 0
        w_spec = pl.BlockSpec((None, pl.Element(tn), pl.Element(K)),
                              lambda j, i: (layer, pl.multiple_of(row0 + j * tn, sublanes), 0))
        wbf_shape = (tn, K)
    in_specs = [pl.BlockSpec((tm, K), lambda j, i: (i, 0)),
                pl.BlockSpec((R2, K), lambda j, i: (0, 0)),
                w_spec]
    args = [a, a2, w]
    if epi == "resid":
        x, ga, ga_map, x2, ga2 = res
        in_specs += [pl.BlockSpec((tm, tn), lambda j, i: (i, j)),
                     pl.BlockSpec((None, ga.shape[1], tn), ga_map),
                     pl.BlockSpec((R2, tn), lambda j, i: (0, j)),
                     pl.BlockSpec((None, R2, tn), lambda j, i: (0, 0, j))]
        args += [x, ga, x2, ga2]
    return pl.pallas_call(
        functools.partial(_mm2_kernel, epi=epi, cast_w=cast_w, w_transposed=w_rows is not None),
        grid=(N // tn, M // tm), in_specs=in_specs,
        out_specs=[pl.BlockSpec((tm, tn), lambda j, i: (i, j)),
                   pl.BlockSpec((R2, tn), lambda j, i: (0, j))],
        out_shape=[jax.ShapeDtypeStruct((M, N), out_dtype), jax.ShapeDtypeStruct((R2, N), out_dtype)],
        scratch_shapes=[pltpu.VMEM(wbf_shape, BF)] if cast_w else [],
        compiler_params=_cparams(("arbitrary", "arbitrary")),
    )(*args)


def _merge2_kernel(an_ref, ar_ref, gn_ref, gr_ref, an2_ref, ar2_ref, gn2_ref, gr2_ref, wn_ref, wr_ref,
                   o_ref, o2_ref, wnb_ref, wrb_ref):
    first = pl.program_id(1) == 0
    wn = _cached_bf16(wn_ref, wnb_ref, first)
    wr = _cached_bf16(wr_ref, wrb_ref, first)

    def finish(a_n, a_r, g_n, g_r, out_ref):
        o = jax.nn.sigmoid(g_n[...]) * _dot(a_n[...], wn) + jax.nn.sigmoid(g_r[...]) * _dot(a_r[...], wr)
        out_ref[...] = o.astype(out_ref.dtype)

    finish(an_ref, ar_ref, gn_ref, gr_ref, o_ref)

    @pl.when(first)
    def _():
        finish(an2_ref, ar2_ref, gn2_ref, gr2_ref, o2_ref)


def _merge2(o_nsa, o_rnn, z, o_nsa2, o_rnn2, z2, w_nsa, w_rnn, *, layer, col0, tm, tn):
    M, K = o_nsa.shape
    R2 = o_nsa2.shape[0]
    D = w_nsa.shape[2]
    c0, c1 = col0 // tn, (col0 + D) // tn
    row = lambda j, i: (i, 0)
    one = lambda j, i: (0, 0)
    return pl.pallas_call(
        _merge2_kernel,
        grid=(D // tn, M // tm),
        in_specs=[pl.BlockSpec((tm, K), row), pl.BlockSpec((tm, K), row),
                  pl.BlockSpec((tm, tn), lambda j, i: (i, j + c0)),
                  pl.BlockSpec((tm, tn), lambda j, i: (i, j + c1)),
                  pl.BlockSpec((R2, K), one), pl.BlockSpec((R2, K), one),
                  pl.BlockSpec((R2, tn), lambda j, i: (0, j + c0)),
                  pl.BlockSpec((R2, tn), lambda j, i: (0, j + c1)),
                  pl.BlockSpec((None, K, tn), lambda j, i: (layer, 0, j)),
                  pl.BlockSpec((None, K, tn), lambda j, i: (layer, 0, j))],
        out_specs=[pl.BlockSpec((tm, tn), lambda j, i: (i, j)),
                   pl.BlockSpec((R2, tn), lambda j, i: (0, j))],
        out_shape=[jax.ShapeDtypeStruct((M, D), BF), jax.ShapeDtypeStruct((R2, D), BF)],
        scratch_shapes=[pltpu.VMEM((K, tn), BF), pltpu.VMEM((K, tn), BF)],
        compiler_params=_cparams(("arbitrary", "arbitrary")),
    )(o_nsa, o_rnn, z, z, o_nsa2, o_rnn2, z2, z2, w_nsa, w_rnn)


def _ln_kernel(x_ref, g_ref, sc_ref, sh_ref, o_ref):
    x = x_ref[...]
    ms = jnp.mean(x * x, axis=-1, keepdims=True)
    y = x * lax.rsqrt(ms + EPS) * g_ref[...]
    o_ref[...] = (y * (1.0 + sc_ref[...]) + sh_ref[...]).astype(o_ref.dtype)


def _ln_mod(x, g, sc, sh, *, tt):
    B, T, D = x.shape
    R = sc.shape[1]
    rr = tt if R == T else 1
    mod_map = (lambda b, i: (b, i, 0)) if R == T else (lambda b, i: (b, 0, 0))
    return pl.pallas_call(
        _ln_kernel,
        grid=(B, T // tt),
        in_specs=[pl.BlockSpec((None, tt, D), lambda b, i: (b, i, 0)),
                  pl.BlockSpec((1, D), lambda b, i: (0, 0)),
                  pl.BlockSpec((None, rr, D), mod_map),
                  pl.BlockSpec((None, rr, D), mod_map)],
        out_specs=pl.BlockSpec((None, tt, D), lambda b, i: (b, i, 0)),
        out_shape=jax.ShapeDtypeStruct((B, T, D), BF),
        compiler_params=_cparams(("parallel", "arbitrary")),
    )(x, g, sc, sh)


def _qk_kernel(z_ref, c_ref, s_ref, g_ref, *refs, scale):
    q_ref, kv_ref, win_ref, kvb_ref = refs[-4:]
    cosf = c_ref[...]
    sinf = s_ref[...]

    def norm_rope(col, gi):
        x = z_ref[:, col:col + HEAD_DIM]
        ms = jnp.mean(x * x, axis=-1, keepdims=True)
        y = x * lax.rsqrt(ms + EPS) * g_ref[gi:gi + 1, :]
        return y * cosf + pltpu.roll(y, HEAD_DIM // 2, 1) * sinf

    nq = N_HEADS * HEAD_DIM
    grp = N_KV * HEAD_DIM
    for h in range(N_HEADS):
        q_ref[:, h * HEAD_DIM:(h + 1) * HEAD_DIM] = (norm_rope(h * HEAD_DIM, 0) * scale).astype(BF)
    for j in range(3):
        for gi in range(N_KV):
            kcol = nq + (2 * j) * grp + gi * HEAD_DIM
            vcol = nq + (2 * j + 1) * grp + gi * HEAD_DIM
            k = norm_rope(kcol, 1 + j)
            v = z_ref[:, vcol:vcol + HEAD_DIM]
            ko = (2 * j) * grp + gi * HEAD_DIM
            vo = (2 * j + 1) * grp + gi * HEAD_DIM
            kvb_ref[:, ko:ko + HEAD_DIM] = k.astype(BF)
            kvb_ref[:, vo:vo + HEAD_DIM] = v.astype(BF)
            tt = k.shape[0]
            if j < 2:
                planes = 4 * N_KV
                kv_ref[pl.ds(2 * j * N_KV + gi, tt, stride=planes), :] = k
                kv_ref[pl.ds((2 * j + 1) * N_KV + gi, tt, stride=planes), :] = v
            else:
                planes = 2 * N_KV
                win_ref[pl.ds(gi, tt, stride=planes), :] = k
                win_ref[pl.ds(N_KV + gi, tt, stride=planes), :] = v


def _qk_prep(z, cosf, sinf, qk_g, *, tt, layer=0, depth=1, prev=None):
    B, T, _ = z.shape
    nq = N_HEADS * HEAD_DIM
    grp = N_KV * HEAD_DIM
    row = lambda b, i: (b, i, 0)
    state = lambda b, i: (layer, b, i, 0)
    in_specs = [pl.BlockSpec((None, tt, z.shape[2]), row),
                pl.BlockSpec((tt, HEAD_DIM), lambda b, i: (i, 0)),
                pl.BlockSpec((tt, HEAD_DIM), lambda b, i: (i, 0)),
                pl.BlockSpec((4, HEAD_DIM), lambda b, i: (0, 0))]
    args = [z, cosf, sinf, qk_g]
    aliases = {}
    if prev is not None:
        in_specs += [pl.BlockSpec(memory_space=pl.ANY)] * 2
        args += list(prev)
        aliases = {4: 1, 5: 2}
    return pl.pallas_call(
        functools.partial(_qk_kernel, scale=HEAD_DIM ** -0.5),
        grid=(B, T // tt),
        in_specs=in_specs,
        out_specs=[pl.BlockSpec((None, tt, nq), row),
                   pl.BlockSpec((None, None, tt * 4 * N_KV, HEAD_DIM), state),
                   pl.BlockSpec((None, None, tt * 2 * N_KV, HEAD_DIM), state),
                   pl.BlockSpec((None, tt, 6 * grp), row)],
        out_shape=[jax.ShapeDtypeStruct((B, T, nq), BF),
                   jax.ShapeDtypeStruct((depth, B, T * 4 * N_KV, HEAD_DIM), F32),
                   jax.ShapeDtypeStruct((depth, B, T * 2 * N_KV, HEAD_DIM), F32),
                   jax.ShapeDtypeStruct((B, T, 6 * grp), BF)],
        input_output_aliases=aliases,
        compiler_params=_cparams(("parallel", "arbitrary")),
    )(*args)


def _rope_tables(pos0, T):
    half = HEAD_DIM // 2
    inv = ROPE_THETA ** (-jnp.arange(half, dtype=F32) / half)
    ang = (pos0 + jnp.arange(T, dtype=F32))[:, None] * inv[None, :]
    cos, sin = jnp.cos(ang), jnp.sin(ang)
    return jnp.concatenate([cos, cos], axis=1), jnp.concatenate([-sin, sin], axis=1)


def _compress_acc(load_pair, pe_ref, w_ref, rows):
    acc = jnp.zeros((rows, 2 * HEAD_DIM), F32)
    for lp in range(CMP_BLOCK // 2):
        xs = []
        for l in (2 * lp, 2 * lp + 1):
            x = load_pair(l) + pe_ref[l][None]
            xs.append(x.reshape(rows, HEAD_DIM).astype(BF))
        acc = acc + _dot(jnp.concatenate(xs, axis=1), w_ref[lp])
    is_k = (lax.broadcasted_iota(jnp.int32, (rows, HEAD_DIM), 0) % 8) < N_KV
    return jnp.where(is_k, acc[:, :HEAD_DIM], acc[:, HEAD_DIM:])


def _compress_kernel(x_ref, pe_ref, w_ref, o_ref, *, nblk):
    rows = nblk * 8
    out = _compress_acc(lambda l: x_ref[pl.ds(l, nblk, stride=CMP_BLOCK), :, :], pe_ref, w_ref, rows)
    o_ref[...] = out.reshape(nblk, 8, HEAD_DIM)


def _compress(kvrows, pe_t, w2, *, layer):
    L, B = kvrows.shape[:2]
    T = kvrows.shape[2] // (4 * N_KV)
    nblk = T // CMP_BLOCK
    x = kvrows.reshape(L, B, T, 4 * N_KV, HEAD_DIM)
    return pl.pallas_call(
        functools.partial(_compress_kernel, nblk=nblk),
        grid=(B,),
        in_specs=[pl.BlockSpec((None, None, T, 8, HEAD_DIM), lambda b: (layer, b, 0, 0, 0)),
                  pl.BlockSpec((CMP_BLOCK, 8, HEAD_DIM), lambda b: (0, 0, 0)),
                  pl.BlockSpec((CMP_BLOCK // 2, 2 * HEAD_DIM, 2 * HEAD_DIM), lambda b: (0, 0, 0))],
        out_specs=pl.BlockSpec((None, nblk, 8, HEAD_DIM), lambda b: (b, 0, 0, 0)),
        out_shape=jax.ShapeDtypeStruct((B, nblk, 8, HEAD_DIM), F32),
        compiler_params=_cparams(("parallel",)),
    )(x, pe_t, w2)


def _compress_paged_kernel(pt_ref, *refs, pb):
    page_refs = refs[:pb]
    pe_ref, w_ref, o_ref = refs[pb:pb + 3]
    rows = pb * 2 * 8

    def load(l):
        tiles = []
        for p in range(pb):
            for h in range(2):
                tiles.append(page_refs[p][h * CMP_BLOCK + l])
        return jnp.stack(tiles, axis=0)

    out = _compress_acc(load, pe_ref, w_ref, rows)
    o_ref[...] = out.reshape(pb * 2, 8, HEAD_DIM)


def _compress_paged(cache5, page_table, layer, pe_t, w2, *, pb):
    n_phys, page = cache5.shape[1], cache5.shape[2]
    B, n_pages = page_table.shape
    assert page == 2 * CMP_BLOCK and n_pages % pb == 0

    def page_spec(p):
        return pl.BlockSpec((None, None, page, 8, HEAD_DIM),
                            lambda b, i, pt: (layer, pt[b, i * pb + p], 0, 0, 0))

    grid_spec = pltpu.PrefetchScalarGridSpec(
        num_scalar_prefetch=1,
        grid=(B, n_pages // pb),
        in_specs=[page_spec(p) for p in range(pb)] + [
            pl.BlockSpec((CMP_BLOCK, 8, HEAD_DIM), lambda b, i, pt: (0, 0, 0)),
            pl.BlockSpec((CMP_BLOCK // 2, 2 * HEAD_DIM, 2 * HEAD_DIM), lambda b, i, pt: (0, 0, 0))],
        out_specs=pl.BlockSpec((None, pb * 2, 8, HEAD_DIM), lambda b, i, pt: (b, i, 0, 0)),
    )
    return pl.pallas_call(
        functools.partial(_compress_paged_kernel, pb=pb),
        grid_spec=grid_spec,
        out_shape=jax.ShapeDtypeStruct((B, n_pages * 2, 8, HEAD_DIM), F32),
        compiler_params=_cparams(("parallel", "arbitrary")),
    )(page_table, *([cache5] * pb), pe_t, w2)


def _online_step(s2, v, m_ref, l_ref, acc_ref):
    tk = s2.shape[1]
    m_prev = m_ref[...]
    m_next = jnp.maximum(m_prev, jnp.max(s2, axis=1, keepdims=True))
    alpha = jnp.exp(m_prev - m_next)
    p = jnp.exp(s2 - _rep_lanes(m_next, tk))
    l_ref[...] = alpha * l_ref[...] + jnp.sum(p, axis=1, keepdims=True)
    m_ref[...] = m_next
    acc_ref[...] = alpha * acc_ref[...] + _dot(p.astype(BF), v)


def _nsa_kernel(*refs, tq, nblk, n_side):
    q_ref, kc_ref, vc_ref, ks_ref, vs_ref, kw_ref, vw_ref, g_ref = refs[:8]
    o_ref = refs[8 + n_side]
    m_ref, l_ref, acc_ref = refs[9 + 2 * n_side:]
    for src, dst in zip(refs[8:8 + n_side], refs[9 + n_side:9 + 2 * n_side]):
        dst[...] = src[...].astype(BF)

    qi = pl.program_id(2)
    q0 = qi * tq
    rows = HPG * tq
    q = q_ref[...]
    qcat = jnp.concatenate([q[:, h * HEAD_DIM:(h + 1) * HEAD_DIM] for h in range(HPG)], axis=0)

    kc = kc_ref[...].astype(BF)
    vc = vc_ref[...].astype(BF)
    t_b = q0 + lax.broadcasted_iota(jnp.int32, (nblk, tq), 1)
    blk = lax.broadcasted_iota(jnp.int32, (nblk, tq), 0)
    complete = jnp.concatenate([(blk + 1) * CMP_BLOCK - 1 <= t_b] * HPG, axis=1)
    s_t = jnp.where(complete, _dot_nt(kc, qcat), NEG)
    mx = jnp.max(s_t, axis=0, keepdims=True)
    e = jnp.where(complete, jnp.exp(s_t - mx), 0.0)
    d = jnp.sum(e, axis=0, keepdims=True)
    p_t = e / jnp.where(d > 0, d, 1.0)

    imp = p_t[:, 0:tq]
    for h in range(1, HPG):
        imp = imp + p_t[:, h * tq:(h + 1) * tq]
    cur = jnp.right_shift(t_b, CMP_SHIFT)
    forced = (blk == 0) | (blk == cur) | (blk == cur - 1)
    score = jnp.where(forced, FORCED_SCORE, imp)
    score = jnp.where(blk <= cur, score, -1.0)
    rank = jnp.zeros((nblk, tq), F32)
    for i in range(nblk):
        si = score[i:i + 1, :]
        beats = (si > score) | ((si == score) & (blk > i))
        rank = rank + jnp.where(beats, 1.0, 0.0)
    sel_t = jnp.where((rank < min(N_SELECT, nblk)) & (score > -0.5), 1.0, 0.0).astype(BF)

    eye = jnp.where(lax.broadcasted_iota(jnp.int32, (tq, tq), 0)
                    == lax.broadcasted_iota(jnp.int32, (tq, tq), 1), 1.0, 0.0).astype(BF)
    p_bf = p_t.astype(BF)
    p_rows = jnp.concatenate([_dot_nt(eye, p_bf[:, h * tq:(h + 1) * tq]) for h in range(HPG)], axis=0)
    o_cmp = _dot(p_rows.astype(BF), vc)
    sel = _dot_nt(eye, sel_t).astype(BF)

    t_k = q0 + lax.broadcasted_iota(jnp.int32, (tq, tq), 0)
    k_off = lax.broadcasted_iota(jnp.int32, (tq, tq), 1)
    e_blk = lax.broadcasted_iota(jnp.int32, (nblk, tq), 0)
    e_off = lax.broadcasted_iota(jnp.int32, (nblk, tq), 1)

    def reset():
        m_ref[...] = jnp.full((rows, LANES), NEG, F32)
        l_ref[...] = jnp.zeros((rows, LANES), F32)
        acc_ref[...] = jnp.zeros((rows, HEAD_DIM), F32)

    def attend(c, k_ref, v_ref, mask):
        start = pl.multiple_of(c * tq, tq)
        k = k_ref[pl.ds(start, tq), :]
        v = v_ref[pl.ds(start, tq), :]
        s = jnp.where(mask[None], _dot_nt(qcat, k).reshape(HPG, tq, tq), NEG)
        _online_step(s.reshape(rows, tq), v, m_ref, l_ref, acc_ref)

    def sel_body(c, carry):
        expand = jnp.where(jnp.right_shift(c * tq + e_off, CMP_SHIFT) == e_blk, 1.0, 0.0).astype(BF)
        chosen = _dot(sel, expand) > 0.5
        attend(c, ks_ref, vs_ref, chosen & (c * tq + k_off <= t_k))
        return carry

    reset()
    lax.fori_loop(0, qi + 1, sel_body, 0)
    o_sel = acc_ref[...] / l_ref[...]

    def win_body(c, carry):
        dist = t_k - (c * tq + k_off)
        attend(c, kw_ref, vw_ref, (dist >= 0) & (dist < WINDOW))
        return carry

    reset()
    lax.fori_loop(jnp.maximum(qi - WINDOW // tq, 0), qi + 1, win_body, 0)
    o_win = acc_ref[...] / l_ref[...]

    g = jax.nn.sigmoid(g_ref[...])
    for h in range(HPG):
        r = slice(h * tq, (h + 1) * tq)
        o = (g[:, 3 * h:3 * h + 1] * o_cmp[r] + g[:, 3 * h + 1:3 * h + 2] * o_sel[r]
             + g[:, 3 * h + 2:3 * h + 3] * o_win[r])
        o_ref[:, h * HEAD_DIM:(h + 1) * HEAD_DIM] = o.astype(BF)


def _nsa_prompt(q, kcv, kvb, zg, *, tq, side_cast):
    B, T, _ = q.shape
    nblk = kcv.shape[1]
    n_q = T // tq
    n_steps = B * N_KV * n_q
    col = lambda c: pl.BlockSpec((None, T, HEAD_DIM), lambda b, g, i, c=c: (b, 0, c + g))
    rows = HPG * tq
    in_specs = [pl.BlockSpec((None, tq, HPG * HEAD_DIM), lambda b, g, i: (b, i, g)),
                pl.BlockSpec((None, nblk, HEAD_DIM), lambda b, g, i: (b, 0, g)),
                pl.BlockSpec((None, nblk, HEAD_DIM), lambda b, g, i: (b, 0, N_KV + g)),
                col(2 * N_KV), col(3 * N_KV), col(4 * N_KV), col(5 * N_KV),
                pl.BlockSpec((None, tq, LANES), lambda b, g, i: (b, i, g))]
    out_specs = [pl.BlockSpec((None, tq, HPG * HEAD_DIM), lambda b, g, i: (b, i, g))]
    out_shape = [jax.ShapeDtypeStruct((B, T, N_HEADS * HEAD_DIM), BF)]
    for p, p_layer in side_cast:
        _, n_rows, n_cols = p.shape
        rb = next(r for r in range(16, n_rows + 16, 16) if pl.cdiv(n_rows, r) <= n_steps)
        last = pl.cdiv(n_rows, rb) - 1
        step = lambda b, g, i, last=last: jnp.minimum((b * N_KV + g) * n_q + i, last)
        in_specs.append(pl.BlockSpec((None, rb, n_cols),
                                     lambda b, g, i, step=step, p_layer=p_layer: (p_layer, step(b, g, i), 0)))
        out_specs.append(pl.BlockSpec((None, rb, n_cols), lambda b, g, i, step=step: (0, step(b, g, i), 0)))
        out_shape.append(jax.ShapeDtypeStruct((1, n_rows, n_cols), BF))
    return pl.pallas_call(
        functools.partial(_nsa_kernel, tq=tq, nblk=nblk, n_side=len(side_cast)),
        grid=(B, N_KV, n_q), in_specs=in_specs, out_specs=out_specs, out_shape=out_shape,
        scratch_shapes=[pltpu.VMEM((rows, LANES), F32), pltpu.VMEM((rows, LANES), F32),
                        pltpu.VMEM((rows, HEAD_DIM), F32)],
        compiler_params=_cparams(("arbitrary", "arbitrary", "arbitrary")),
    )(q, kcv, kcv, kvb, kvb, kvb, kvb, zg, *[p for p, _ in side_cast])


def _softplus(y):
    return jnp.maximum(y, 0.0) + jnp.log1p(jnp.exp(-jnp.abs(y)))


def _rg_gates(u, w_ref, brg_ref, lam_ref):
    ub = u.astype(BF)
    r = jax.nn.sigmoid(_dot(ub, w_ref[0].astype(BF)) + brg_ref[0:1, :])
    i = jax.nn.sigmoid(_dot(ub, w_ref[1].astype(BF)) + brg_ref[1:2, :])
    log_a = -RG_C * r * _softplus(-lam_ref[...])
    a = jnp.exp(log_a)
    th = jnp.tanh(log_a)
    b = jnp.sqrt(-2.0 * th / (1.0 - th)) * i * u
    return a, b


def _rglru_kernel(xr_ref, zg_ref, c0_ref, h0_ref, cw_ref, cb_ref, w_ref, brg_ref, lam_ref,
                  y_ref, hl_ref, cl_ref, xp_ref, a_ref, b_ref, h_ref, *, T):
    pad = 8
    xp_ref[0:pad, :] = jnp.zeros((pad, RNN_BW), F32)
    xp_ref[pad - (CONV_W - 1):pad, :] = c0_ref[...]
    xp_ref[pad:pad + T, :] = xr_ref[...]
    u = cb_ref[...] + xp_ref[pad:pad + T, :] * cw_ref[CONV_W - 1:CONV_W, :]
    for k in range(CONV_W - 1):
        off = pad - (CONV_W - 1) + k
        u = u + xp_ref[off:off + T, :] * cw_ref[k:k + 1, :]
    cl_ref[...] = xp_ref[pad + T - (CONV_W - 1):pad + T, :]

    a, b = _rg_gates(u, w_ref, brg_ref, lam_ref)
    a_ref[0:T, :] = a
    b_ref[0:T, :] = b

    levels = []
    n, off = T, 0
    while True:
        levels.append((n, off))
        if n <= 8:
            break
        off += n
        n //= 8
    for (n, off), (n2, off2) in zip(levels[:-1], levels[1:]):
        A = a_ref[pl.ds(off, n2, stride=8), :]
        Bv = b_ref[pl.ds(off, n2, stride=8), :]
        for r in range(1, 8):
            ar = a_ref[pl.ds(off + r, n2, stride=8), :]
            br = b_ref[pl.ds(off + r, n2, stride=8), :]
            Bv = ar * Bv + br
            A = ar * A
        a_ref[off2:off2 + n2, :] = A
        b_ref[off2:off2 + n2, :] = Bv
    n, off = levels[-1]
    h = h0_ref[...]
    for t in range(n):
        h = a_ref[off + t:off + t + 1, :] * h + b_ref[off + t:off + t + 1, :]
        h_ref[off + 8 + t:off + 9 + t, :] = h
    for (n, off), (n2, off2) in reversed(list(zip(levels[:-1], levels[1:]))):
        h_ref[off2 + 7:off2 + 8, :] = h0_ref[...]
        hp = h_ref[off2 + 7:off2 + 7 + n2, :]
        for r in range(8):
            ar = a_ref[pl.ds(off + r, n2, stride=8), :]
            br = b_ref[pl.ds(off + r, n2, stride=8), :]
            hp = ar * hp + br
            h_ref[pl.ds(off + 8 + r, n2, stride=8), :] = hp
    hfull = h_ref[8:8 + T, :]
    y_ref[...] = (jax.nn.gelu(zg_ref[...]) * hfull).astype(BF)
    hl_ref[...] = h_ref[8 + T - 1:8 + T, :]


def _rglru(zr, conv0, h0, conv_w, conv_b, w_rg, b_rg, lam, *, d_rnn):
    B, T, _ = zr.shape
    nb = d_rnn // RNN_BW
    assert T % 8 == 0
    total = T + T // 4 + 64
    return pl.pallas_call(
        functools.partial(_rglru_kernel, T=T),
        grid=(B, nb),
        in_specs=[pl.BlockSpec((None, T, RNN_BW), lambda b, n: (b, 0, n)),
                  pl.BlockSpec((None, T, RNN_BW), lambda b, n: (b, 0, nb + n)),
                  pl.BlockSpec((None, CONV_W - 1, RNN_BW), lambda b, n: (b, 0, n)),
                  pl.BlockSpec((None, 1, RNN_BW), lambda b, n: (b, 0, n)),
                  pl.BlockSpec((CONV_W, RNN_BW), lambda b, n: (0, n)),
                  pl.BlockSpec((1, RNN_BW), lambda b, n: (0, n)),
                  pl.BlockSpec((2, None, RNN_BW, RNN_BW), lambda b, n: (0, n, 0, 0)),
                  pl.BlockSpec((2, RNN_BW), lambda b, n: (0, n)),
                  pl.BlockSpec((1, RNN_BW), lambda b, n: (0, n))],
        out_specs=[pl.BlockSpec((None, T, RNN_BW), lambda b, n: (b, 0, n)),
                   pl.BlockSpec((None, 1, RNN_BW), lambda b, n: (b, 0, n)),
                   pl.BlockSpec((None, CONV_W - 1, RNN_BW), lambda b, n: (b, 0, n))],
        out_shape=[jax.ShapeDtypeStruct((B, T, d_rnn), BF),
                   jax.ShapeDtypeStruct((B, 1, d_rnn), F32),
                   jax.ShapeDtypeStruct((B, CONV_W - 1, d_rnn), F32)],
        scratch_shapes=[pltpu.VMEM((T + 8, RNN_BW), F32), pltpu.VMEM((total, RNN_BW), F32),
                        pltpu.VMEM((total, RNN_BW), F32), pltpu.VMEM((total + 8, RNN_BW), F32)],
        compiler_params=_cparams(("parallel", "arbitrary")),
    )(zr, zr, conv0, h0, conv_w, conv_b[None], w_rg, b_rg, lam[None])


def _rglru_step_kernel(xr_ref, zg_ref, c0_ref, h0_ref, cw_ref, cb_ref, w_ref, brg_ref, lam_ref,
                       y_ref, hl_ref):
    u = cb_ref[...] + xr_ref[...] * cw_ref[CONV_W - 1:CONV_W, :]
    for k in range(CONV_W - 1):
        u = u + c0_ref[k] * cw_ref[k:k + 1, :]
    a, b = _rg_gates(u, w_ref, brg_ref, lam_ref)
    h = a * h0_ref[...] + b
    hl_ref[...] = h
    y_ref[...] = (jax.nn.gelu(zg_ref[...]) * h).astype(BF)


def _rglru_step(zr, conv0_t, h0, conv_w, conv_b, w_rg, b_rg, lam, *, d_rnn):
    R = zr.shape[0]
    nb = d_rnn // RNN_BW
    return pl.pallas_call(
        _rglru_step_kernel,
        grid=(nb,),
        in_specs=[pl.BlockSpec((R, RNN_BW), lambda n: (0, n)),
                  pl.BlockSpec((R, RNN_BW), lambda n: (0, nb + n)),
                  pl.BlockSpec((CONV_W - 1, R, RNN_BW), lambda n: (0, 0, n)),
                  pl.BlockSpec((R, RNN_BW), lambda n: (0, n)),
                  pl.BlockSpec((CONV_W, RNN_BW), lambda n: (0, n)),
                  pl.BlockSpec((1, RNN_BW), lambda n: (0, n)),
                  pl.BlockSpec((2, None, RNN_BW, RNN_BW), lambda n: (0, n, 0, 0)),
                  pl.BlockSpec((2, RNN_BW), lambda n: (0, n)),
                  pl.BlockSpec((1, RNN_BW), lambda n: (0, n))],
        out_specs=[pl.BlockSpec((R, RNN_BW), lambda n: (0, n)),
                   pl.BlockSpec((R, RNN_BW), lambda n: (0, n))],
        out_shape=[jax.ShapeDtypeStruct((R, d_rnn), BF), jax.ShapeDtypeStruct((R, d_rnn), F32)],
        compiler_params=_cparams(("arbitrary",)),
    )(zr, zr, conv0_t, h0, conv_w, conv_b[None], w_rg, b_rg, lam[None])


def _dec_cmp_kernel(q_ref, kcv_ref, o_ref, idx_ref, *, n_past, t_pos):
    nlan = LANES * ((n_past + 1 + LANES - 1) // LANES)
    blk = lax.broadcasted_iota(jnp.int32, (1, n_past), 1)
    complete = (blk + 1) * CMP_BLOCK - 1 <= t_pos
    imps = []
    for g in range(N_KV):
        kc = kcv_ref[:, g * HEAD_DIM:(g + 1) * HEAD_DIM].astype(BF)
        vc = kcv_ref[:, (N_KV + g) * HEAD_DIM:(N_KV + g + 1) * HEAD_DIM].astype(BF)
        s = jnp.where(complete, _dot_nt(q_ref[g], kc), NEG)
        mx = jnp.max(s, axis=-1, keepdims=True)
        e = jnp.where(complete, jnp.exp(s - mx), 0.0)
        d = jnp.sum(e, axis=-1, keepdims=True)
        p = e / jnp.where(d > 0, d, 1.0)
        o_ref[g] = _dot(p.astype(BF), vc)
        head = lax.broadcasted_iota(jnp.int32, p.shape, 0) < HPG
        imps.append(jnp.sum(jnp.where(head, p, 0.0), axis=0, keepdims=True))
    imp = jnp.concatenate(imps, axis=0)

    cur = t_pos // CMP_BLOCK
    imp_all = jnp.concatenate([imp, jnp.zeros((N_KV, nlan - n_past), F32)], axis=1)
    blk_all = lax.broadcasted_iota(jnp.int32, (N_KV, nlan), 1)
    forced = (blk_all == 0) | (blk_all == cur) | (blk_all == cur - 1)
    score = jnp.where(forced, FORCED_SCORE, imp_all)
    score = jnp.where(blk_all <= cur, score, -1.0)
    score = jnp.where(blk_all <= n_past, score, -2.0)
    blk_f = blk_all.astype(F32)
    slot = lax.broadcasted_iota(jnp.int32, (N_KV, LANES), 1)
    res = jnp.full((N_KV, LANES), float(n_past), F32)
    for k in range(min(N_SELECT, n_past + 1)):
        top = jnp.max(score, axis=-1, keepdims=True)
        arg = jnp.min(jnp.where(score == top, blk_f, float(nlan)), axis=-1, keepdims=True)
        res = jnp.where(slot == k, jnp.where(top > -0.5, arg, float(n_past)), res)
        score = jnp.where(blk_f == arg, -3.0, score)
    idx_ref[...] = res.astype(jnp.int32)


def _dec_cmp(q16, kcv, *, t_pos):
    B = q16.shape[0]
    n_past = kcv.shape[1]
    return pl.pallas_call(
        functools.partial(_dec_cmp_kernel, n_past=n_past, t_pos=t_pos),
        grid=(B,),
        in_specs=[pl.BlockSpec((None, N_KV, 16, HEAD_DIM), lambda b: (b, 0, 0, 0)),
                  pl.BlockSpec((None, n_past, 2 * N_KV * HEAD_DIM), lambda b: (b, 0, 0))],
        out_specs=[pl.BlockSpec((None, N_KV, 16, HEAD_DIM), lambda b: (b, 0, 0, 0)),
                   pl.BlockSpec((None, N_KV, LANES), lambda b: (b, 0, 0))],
        out_shape=[jax.ShapeDtypeStruct((B, N_KV, 16, HEAD_DIM), F32),
                   jax.ShapeDtypeStruct((B, N_KV, LANES), jnp.int32)],
        compiler_params=_cparams(("parallel",)),
    )(q16, kcv)


def _dec_attn_kernel(pt_ref, ti_ref, q_ref, *refs, n_past, per_step):
    cache_refs = refs[:N_KV * per_step]
    new_ref, win_ref, oc_ref, g_ref, o_ref, m_ref, l_ref, acc_ref = refs[N_KV * per_step:]
    b, s = pl.program_id(0), pl.program_id(1)
    planes = 4 * N_KV

    @pl.when(s == 0)
    def _():
        m_ref[...] = jnp.full(m_ref.shape, NEG, F32)
        l_ref[...] = jnp.zeros(l_ref.shape, F32)
        acc_ref[...] = jnp.zeros(acc_ref.shape, F32)

    n_keys = per_step * CMP_BLOCK
    key_slot = lax.broadcasted_iota(jnp.int32, (1, n_keys), 1) // CMP_BLOCK
    for g in range(N_KV):
        ks, vs = [], []
        key_blk = jnp.full((1, n_keys), n_past, jnp.int32)
        for u in range(per_step):
            ref = cache_refs[u * N_KV + g]
            ks.append(ref[pl.ds(2 * N_KV + g, CMP_BLOCK, stride=planes), :].astype(BF))
            vs.append(ref[pl.ds(3 * N_KV + g, CMP_BLOCK, stride=planes), :].astype(BF))
            key_blk = jnp.where(key_slot == u, ti_ref[b * N_KV + g, s * per_step + u], key_blk)
        live = key_blk < n_past
        sc =jnp.where(live, _dot_nt(q_ref[g], jnp.concatenate(ks, axis=0)), NEG)
        m_prev = m_ref[g]
        m_next = jnp.maximum(m_prev, jnp.max(sc, axis=1, keepdims=True))
        alpha = jnp.exp(m_prev - m_next)
        p = jnp.where(live, jnp.exp(sc - m_next[:, :n_keys]), 0.0)
        l_ref[g] = alpha * l_ref[g] + jnp.sum(p, axis=1, keepdims=True)
        m_ref[g] = m_next
        acc_ref[g] = alpha * acc_ref[g] + _dot(p.astype(BF), jnp.concatenate(vs, axis=0))

    @pl.when(s == pl.num_programs(1) - 1)
    def _():
        n_win = win_ref.shape[0] // (2 * N_KV)
        keep = lax.broadcasted_iota(jnp.int32, (1, n_win), 1) >= n_win + 1 - WINDOW
        for g in range(N_KV):
            q = q_ref[g]
            qf = q.astype(F32)

            def self_score(k_row):
                return jnp.sum(qf * k_row.astype(BF).astype(F32), axis=1, keepdims=True)

            s_self = self_score(new_ref[g, 2:3, :])
            v_self = new_ref[g, 3:4, :].astype(BF).astype(F32)
            m_prev = m_ref[g]
            m_fin = jnp.maximum(m_prev, s_self)
            alpha = jnp.exp(m_prev - m_fin)
            p_self = jnp.exp(s_self - m_fin)
            o_sel = (alpha * acc_ref[g] + p_self * v_self) / (alpha * l_ref[g] + p_self)

            kw = win_ref[pl.ds(g, n_win, stride=2 * N_KV), :].astype(BF)
            vw = win_ref[pl.ds(N_KV + g, n_win, stride=2 * N_KV), :].astype(BF)
            sw = jnp.where(keep, _dot_nt(q, kw), NEG)
            sw_self = self_score(new_ref[g, 4:5, :])
            mw = jnp.maximum(jnp.max(sw, axis=1, keepdims=True), sw_self)
            pw = jnp.where(keep, jnp.exp(sw - mw), 0.0)
            pw_self = jnp.exp(sw_self - mw)
            num = _dot(pw.astype(BF), vw) + pw_self * new_ref[g, 5:6, :].astype(BF).astype(F32)
            o_win = num / (jnp.sum(pw, axis=1, keepdims=True) + pw_self)

            gt = jax.nn.sigmoid(g_ref[g])
            o_ref[g] = gt[:, 0:1] * oc_ref[g] + gt[:, 1:2] * o_sel + gt[:, 2:3] * o_win


def _dec_attn(q16, cache4, page_table, top_i, newkv, win3, o_cmp, g16, *, layer, n_past):
    B = q16.shape[0]
    n_slots = top_i.shape[1]
    per_step = 2 if n_slots % 2 == 0 else 1
    half_rows = CMP_BLOCK * 4 * N_KV

    def half_page(u, g):
        def index(b, s, pt, ti):
            blk = jnp.minimum(ti[b * N_KV + g, s * per_step + u], n_past - 1)
            return (layer, pt[b, blk // 2], blk % 2, 0)
        return pl.BlockSpec((None, None, half_rows, HEAD_DIM), index)

    per_seq = lambda b, s, pt, ti: (b, 0, 0, 0)
    grid_spec = pltpu.PrefetchScalarGridSpec(
        num_scalar_prefetch=2,
        grid=(B, n_slots // per_step),
        in_specs=[pl.BlockSpec((None, N_KV, 16, HEAD_DIM), per_seq)]
        + [half_page(u, g) for u in range(per_step) for g in range(N_KV)]
        + [pl.BlockSpec((None, N_KV, 6, HEAD_DIM), per_seq),
           pl.BlockSpec((None, win3.shape[1], HEAD_DIM), lambda b, s, pt, ti: (layer * B + b, 0, 0)),
           pl.BlockSpec((None, N_KV, 16, HEAD_DIM), per_seq),
           pl.BlockSpec((None, N_KV, 16, LANES), per_seq)],
        out_specs=pl.BlockSpec((None, N_KV, 16, HEAD_DIM), per_seq),
        scratch_shapes=[pltpu.VMEM((N_KV, 16, LANES), F32), pltpu.VMEM((N_KV, 16, LANES), F32),
                        pltpu.VMEM((N_KV, 16, HEAD_DIM), F32)],
    )
    return pl.pallas_call(
        functools.partial(_dec_attn_kernel, n_past=n_past, per_step=per_step),
        grid_spec=grid_spec,
        out_shape=jax.ShapeDtypeStruct((B, N_KV, 16, HEAD_DIM), F32),
        compiler_params=_cparams(("parallel", "arbitrary")),
    )(page_table, top_i, q16, *([cache4] * (N_KV * per_step)), newkv, win3, o_cmp, g16)


def _regroup_w_in(w_in):
    L, D, _ = w_in.shape
    o_gn = N_HEADS * HEAD_DIM + 6 * N_KV * HEAD_DIM
    o_rx = o_gn + 3 * N_HEADS
    wg = w_in[:, :, o_gn:o_rx].reshape(L, D, N_KV, 3 * HPG)
    wg = jnp.pad(wg, ((0, 0), (0, 0), (0, 0), (0, LANES - 3 * HPG))).reshape(L, D, N_KV * LANES)
    return jnp.swapaxes(w_in, 1, 2), wg.astype(BF)


def _row_tile(m):
    return 1024 if m % 1024 == 0 else 256


def _largest_divisor(n, cap):
    return max(d for d in range(1, cap + 1) if n % d == 0)


def _layer(l, xp, xs, mod_p, mod_s, P, w_in_bf, prompt_state, rope_p, rope_s, cache_kv, page_table, state_win,
           state_rnn, state_conv):
    B, T, D = xp.shape
    M = B * T
    R = xs.shape[0]
    nb, n_pages = page_table.shape
    depth, n_phys, page = cache_kv.shape[:3]
    n_past = n_pages * page // CMP_BLOCK
    d_attn = N_HEADS * HEAD_DIM
    d_rnn = P["conv_w"].shape[2]
    o_gn = d_attn + 6 * N_KV * HEAD_DIM
    o_rx = o_gn + 3 * N_HEADS
    tm, tn = _row_tile(M), 512
    shp1, scp1, gap1, shp2, scp2, gap2 = mod_p
    shs1, scs1, gas1, shs2, scs2, gas2 = mod_s
    bmap = lambda j, i: (i // (T // tm), 0, j)
    g_ln = P["g_ln"][l]

    w2 = jnp.concatenate([P["w_phi"][l, 0], P["w_phi"][l, 1]], axis=-1)
    w2 = w2.reshape(CMP_BLOCK // 2, 2 * HEAD_DIM, 2 * HEAD_DIM).astype(BF)
    pe_t = jnp.concatenate([jnp.broadcast_to(P["pe_cmp"][l, j][:, None, :], (CMP_BLOCK, N_KV, HEAD_DIM))
                            for j in range(2)], axis=1)
    rnn = (P["conv_w"][l], P["conv_b"][l], P["w_rg"][l], P["b_rg"][l], P["lam"][l])

    hp = _ln_mod(xp, g_ln[0:1], scp1, shp1, tt=256).reshape(M, D)
    hs = _ln_mod(xs[None], g_ln[0:1], scs1, shs1, tt=R).reshape(R, D)
    zq_p, zq_s = _mm2(hp, hs, w_in_bf, layer=0, tm=tm, tn=1024, w_rows=(0, o_gn))
    zg_p, zg_s = _mm2(hp, hs, P["w_gate"], layer=l, tm=tm, tn=tn)
    zr_p, zr_s = _mm2(hp, hs, w_in_bf, layer=0, tm=tm, tn=1024, w_rows=(o_rx, w_in_bf.shape[1] - o_rx))

    n_layers = P["w_in_t"].shape[0]
    q, kvrows, winrows, kvb = _qk_prep(zq_p.reshape(B, T, -1), rope_p[0], rope_p[1], P["qk_g"][l], tt=256,
                                       layer=l, depth=n_layers, prev=prompt_state)
    kcv = _compress(kvrows, pe_t, w2, layer=l)
    side = [(P["w_out"], l), (P["w_mlp1"], l), (P["w_mlp2"], l)]
    if l + 1 < P["w_in_t"].shape[0]:
        side.append((P["w_in_t"], l + 1))
    o_nsa_p, w_out_bf, w_mlp1_bf, w_mlp2_bf, *w_in_next = _nsa_prompt(
        q, kcv.reshape(B, T // CMP_BLOCK, 8 * HEAD_DIM), kvb, zg_p.reshape(B, T, -1), tq=256, side_cast=side)
    conv0 = jnp.zeros((B, CONV_W - 1, d_rnn), F32)
    h0 = jnp.zeros((B, 1, d_rnn), F32)
    y_rnn_p, h_last, conv_last = _rglru(zr_p.reshape(B, T, -1), conv0, h0, *rnn, d_rnn=d_rnn)

    qs, kvrows_s, winrows_s, _ = _qk_prep(zq_s[None], rope_s[0], rope_s[1], P["qk_g"][l], tt=R)
    q16 = jnp.pad(qs[0, :nb].reshape(nb, N_KV, HPG, HEAD_DIM), ((0, 0), (0, 0), (0, 16 - HPG), (0, 0)))
    g16 = jnp.pad(zg_s[:nb].reshape(nb, N_KV, LANES)[:, :, :3 * HPG].reshape(nb, N_KV, HPG, 3),
                  ((0, 0), (0, 0), (0, 16 - HPG), (0, LANES - 3)))
    kv_new = kvrows_s.reshape(R, 4, N_KV, HEAD_DIM)[:nb]
    win_new = winrows_s.reshape(R, 2, N_KV, HEAD_DIM)[:nb]
    new6 = jnp.concatenate([kv_new, win_new], axis=1).transpose(0, 2, 1, 3)
    cache5 = cache_kv.reshape(depth, n_phys, page, 4 * N_KV, HEAD_DIM)
    kcv_s = _compress_paged(cache5, page_table, l, pe_t, w2, pb=_largest_divisor(n_pages, 32))
    o_cmp, top = _dec_cmp(q16, kcv_s.reshape(nb, n_past, 8 * HEAD_DIM), t_pos=n_pages * page)
    top_i = top[:, :, :N_SELECT].reshape(nb * N_KV, N_SELECT)
    cache4 = cache_kv.reshape(depth, n_phys, page * 4 * N_KV, HEAD_DIM)
    n_win = state_win.shape[2]
    win3 = state_win.reshape(depth * nb, n_win * 2 * N_KV, HEAD_DIM)
    o16 = _dec_attn(q16, cache4, page_table, top_i, new6, win3, o_cmp, g16, layer=l, n_past=n_past)
    o_nsa_s = jnp.pad(o16[:, :, :HPG].reshape(nb, -1), ((0, R - nb), (0, 0))).astype(BF)
    pad_r = lambda a: jnp.pad(a, ((0, R - nb),) + ((0, 0),) * (a.ndim - 1))
    conv0_t = pad_r(state_conv[l]).transpose(1, 0, 2)
    y_rnn_s, h_new = _rglru_step(zr_s, conv0_t, pad_r(state_rnn[l]), *rnn, d_rnn=d_rnn)

    mg_p, mg_s = _merge2(o_nsa_p.reshape(M, -1), y_rnn_p.reshape(M, -1), zr_p, o_nsa_s, y_rnn_s, zr_s,
                         P["w_branch_nsa"], P["w_branch_rnn"], layer=l, col0=2 * d_rnn, tm=tm, tn=tn)
    x1p, x1s = _mm2(mg_p, mg_s, w_out_bf, layer=0, tm=tm, tn=tn, epi="resid",
                    res=(xp.reshape(M, D), gap1, bmap, xs, gas1))
    h2p = _ln_mod(x1p.reshape(B, T, D), g_ln[1:2], scp2, shp2, tt=256).reshape(M, D)
    h2s = _ln_mod(x1s[None], g_ln[1:2], scs2, shs2, tt=R).reshape(R, D)
    up, us = _mm2(h2p, h2s, w_mlp1_bf, layer=0, tm=tm, tn=1024, out_dtype=BF, epi="relu2")
    x2p = _mm_kres(up, w_mlp2_bf, x1p, gap2, layer=0, rows_per_batch=T, tm=tm, tn=tn, tk=4096)
    x2s = _mm(us, w_mlp2_bf, layer=0, tm=R, tn=1024, tk=4096, epi="resid", x_res=x1s, ga=gas2,
              ga_map=lambda i, j, k: (0, 0, j))

    outs_p = (h_last.reshape(B, -1), conv_last)
    outs_s = (kv_new[:, None],
              jnp.concatenate([state_win[l][:, 1:], win_new[:, None]], axis=1),
              h_new[:nb],
              jnp.concatenate([state_conv[l][:, 1:], zr_s[:nb, None, :d_rnn]], axis=1))
    return (x2p.reshape(B, T, D), x2s, outs_p, outs_s, (w_in_next[0] if w_in_next else None),
            (kvrows, winrows))


def kernel(x_prompt, x_sample, cache_kv, state_win, state_rnn, state_conv, page_table, c_prompt, c_sample,
           w_ada, b_ada, g_ln, w_in, qk_g, w_phi, pe_cmp, conv_w, conv_b, w_rg, b_rg, lam,
           w_branch_nsa, w_branch_rnn, w_out, w_mlp1, w_mlp2):
    depth = w_ada.shape[0]
    B, T, D = x_prompt.shape
    nb = x_sample.shape[0]
    R = 16
    past_len = page_table.shape[1] * cache_kv.shape[2]

    c_all = jnp.zeros((2 * R, D), F32).at[:B].set(c_prompt).at[R:R + nb].set(c_sample)
    rope_p = _rope_tables(0.0, T)
    rope_s = tuple(jnp.broadcast_to(t, (R, HEAD_DIM)) for t in _rope_tables(float(past_len), 1))

    w_in_t, w_gate = _regroup_w_in(w_in)
    P = dict(g_ln=g_ln, w_in_t=w_in_t, w_gate=w_gate, qk_g=qk_g, w_phi=w_phi, pe_cmp=pe_cmp, conv_w=conv_w, conv_b=conv_b,
             w_rg=w_rg, b_rg=b_rg, lam=lam, w_branch_nsa=w_branch_nsa, w_branch_rnn=w_branch_rnn,
             w_out=w_out, w_mlp1=w_mlp1, w_mlp2=w_mlp2)
    yp = x_prompt
    ys = jnp.pad(x_sample.reshape(nb, D), ((0, R - nb), (0, 0)))
    outs = [[] for _ in range(6)]
    w_in_bf = w_in_t[0:1].astype(BF)
    prompt_state = None
    for l in range(depth):
        mod = _mm(c_all, w_ada, layer=l, tm=2 * R, tn=512, epi="silu_bias", bias=b_ada[l][None])
        mod = mod.reshape(2 * R, 6, D)
        mod_p = [mod[:B, k][:, None, :] for k in range(6)]
        mod_s = [mod[R:, k][None] for k in range(6)]
        yp, ys, (a3, a4), (b1, b2, b3, b4), w_in_bf, prompt_state = _layer(
            l, yp, ys, mod_p, mod_s, P, w_in_bf, prompt_state, rope_p, rope_s, cache_kv, page_table, state_win,
            state_rnn, state_conv)
        for lst, v in zip(outs, (b1, b2, a3, b3, a4, b4)):
            lst.append(v)
    kv_s, win_s, h_p, h_s, cv_p, cv_s = (jnp.stack(v) for v in outs)
    n_keep = min(WINDOW, T)
    kv_p = prompt_state[0].reshape(depth, B, T, 4, N_KV, HEAD_DIM)
    win_p = prompt_state[1].reshape(depth, B, T, 2, N_KV, HEAD_DIM)[:, :, T - n_keep:]
    return (yp, ys[:nb].reshape(nb, 1, D), kv_p, kv_s, win_p, win_s, h_p, h_s, cv_p, cv_s)
```
